```python
import jax, jax.numpy as jnp
from jax import lax
import numpy as np

D_MODEL = 1024
BATCH = 8
SEQ = 8192
DEPTH = 2
DEC_BATCH = 4
DEC_SEQ = 4096
PAST_LEN = 128

HEAD_DIM = 64
N_HEADS_A = 4
DILATED_PATTERNS = ((128, 1), (512, 4), (2048, 16))
N_HEADS_B = 8
N_KV_B = 2
GQA_GROUP = N_HEADS_B // N_KV_B
N_GROUPS_C = 4
D_A = N_HEADS_A * HEAD_DIM
D_B = N_HEADS_B * HEAD_DIM
D_KV_B = N_KV_B * HEAD_DIM
D_C = N_GROUPS_C * HEAD_DIM
D_MIX = D_A + D_B + D_C
N_MIX_HEADS = D_MIX // HEAD_DIM
D_IN = 3 * D_A + D_B + 2 * D_KV_B + D_C
IN_SPLITS = (D_A, 2 * D_A, 3 * D_A, 3 * D_A + D_B, 3 * D_A + D_B + D_KV_B, 3 * D_A + D_B + 2 * D_KV_B)
D_FF = 4 * D_MODEL
CONV_W = 3
ROPE_THETA_PARTIAL = 500000.0
ROT_DIM_PARTIAL = HEAD_DIM // 4
ROPE_THETA_AXIAL = 10000.0
GRID_W = 64
Q_BLOCK = 128
RMS_EPS = 1e-6
NEG_INF = -1e30

kernel_name = "hybrid_dilated_grid_fourier_encoder"


def _rms_norm(x, g):
    xf = x.astype(jnp.float32)
    y = xf * lax.rsqrt(jnp.mean(xf * xf, axis=-1, keepdims=True) + RMS_EPS)
    return (y * g.astype(jnp.float32)).astype(x.dtype)


def _rope_tables(pos, rot_dim, theta):
    half = rot_dim // 2
    inv = jnp.power(jnp.float32(theta), -jnp.arange(half, dtype=jnp.float32) / half)
    ang = pos.astype(jnp.float32)[:, None] * inv[None, :]
    return jnp.cos(ang), jnp.sin(ang)


def _apply_rope(x, cos, sin):
    half = x.shape[-1] // 2
    x1 = x[..., :half].astype(jnp.float32)
    x2 = x[..., half:].astype(jnp.float32)
    c = cos[:, None, :]
    s = sin[:, None, :]
    return jnp.concatenate([x1 * c - x2 * s, x2 * c + x1 * s], axis=-1).astype(x.dtype)


def _partial_rope(x, cos, sin):
    return jnp.concatenate([_apply_rope(x[..., :ROT_DIM_PARTIAL], cos, sin), x[..., ROT_DIM_PARTIAL:]], axis=-1)


def _axial_rope(x, cos_r, sin_r, cos_c, sin_c):
    half = HEAD_DIM // 2
    return jnp.concatenate([_apply_rope(x[..., :half], cos_r, sin_r),
                            _apply_rope(x[..., half:], cos_c, sin_c)], axis=-1)


def _position_tables(seq_len):
    rows = seq_len // GRID_W
    t = jnp.arange(seq_len, dtype=jnp.int32)
    row = jnp.broadcast_to(jnp.arange(rows, dtype=jnp.int32)[:, None], (rows, GRID_W)).reshape(-1)
    col = jnp.broadcast_to(jnp.arange(GRID_W, dtype=jnp.int32)[None, :], (rows, GRID_W)).reshape(-1)
    cos_t, sin_t = _rope_tables(t, ROT_DIM_PARTIAL, ROPE_THETA_PARTIAL)
    cos_r, sin_r = _rope_tables(row, HEAD_DIM // 2, ROPE_THETA_AXIAL)
    cos_c, sin_c = _rope_tables(col, HEAD_DIM // 2, ROPE_THETA_AXIAL)
    return (cos_t, sin_t, cos_r, sin_r, cos_c, sin_c)


def _dilated_attention(q, k, v):
    B, S, H, Dh = q.shape
    scale = Dh ** -0.5
    n_blocks = S // Q_BLOCK

    def block(i):
        q0 = i * Q_BLOCK
        qb = lax.dynamic_slice_in_dim(q, q0, Q_BLOCK, axis=1)
        qpos = q0 + jnp.arange(Q_BLOCK, dtype=jnp.int32)
        scores, idxs, n_keys = [], [], []
        for window, dil in DILATED_PATTERNS:
            side = window // (2 * dil)
            offs = jnp.arange(-side, side + 1, dtype=jnp.int32) * dil
            pos = qpos[:, None] + offs[None, :]
            valid = (pos >= 0) & (pos < S)
            idx = jnp.clip(pos, 0, S - 1)
            kg = jnp.take(k, idx.reshape(-1), axis=1).reshape(B, Q_BLOCK, offs.shape[0], H, Dh)
            s = jnp.einsum("bqhd,bqkhd->bhqk", qb, kg, preferred_element_type=jnp.float32) * scale
            scores.append(jnp.where(valid[None, None], s, NEG_INF))
            idxs.append(idx)
            n_keys.append(offs.shape[0])
        p = jax.nn.softmax(jnp.concatenate(scores, axis=-1), axis=-1)
        out = jnp.zeros((B, Q_BLOCK, H, Dh), jnp.float32)
        start = 0
        for idx, nk in zip(idxs, n_keys):
            vg = jnp.take(v, idx.reshape(-1), axis=1).reshape(B, Q_BLOCK, nk, H, Dh)
            out = out + jnp.einsum("bhqk,bqkhd->bqhd", p[..., start:start + nk],
                                   vg.astype(jnp.float32))
            start += nk
        return out.astype(v.dtype)

    o = lax.map(block, jnp.arange(n_blocks, dtype=jnp.int32))
    return o.transpose(1, 0, 2, 3, 4).reshape(B, S, H, Dh)


def _grid_attention(q, k, v):
    B, S, HQ, Dh = q.shape
    scale = Dh ** -0.5
    qg = q.reshape(B, S, N_KV_B, GQA_GROUP, Dh)

    def block(i):
        qb = lax.dynamic_slice_in_dim(qg, i * Q_BLOCK, Q_BLOCK, axis=1)
        s = jnp.einsum("bqgrd,bkgd->bgrqk", qb, k, preferred_element_type=jnp.float32) * scale
        p = jax.nn.softmax(s, axis=-1)
        return jnp.einsum("bgrqk,bkgd->bqgrd", p.astype(v.dtype), v)

    o = lax.map(block, jnp.arange(S // Q_BLOCK, dtype=jnp.int32))
    return o.transpose(1, 0, 2, 3, 4, 5).reshape(B, S, HQ, Dh)


def _fourier_mix(u):
    return jnp.real(jnp.fft.fft2(u.astype(jnp.float32), axes=(1, 3))).astype(u.dtype)


def _token_mixer(h, w_in, g_q, g_k, g_heads, w_out, tables):
    cos_t, sin_t, cos_r, sin_r, cos_c, sin_c = tables
    B, S, _ = h.shape
    proj = h @ w_in
    qa, ka, va, qb, kb, vb, uc = jnp.split(proj, IN_SPLITS, axis=-1)

    def heads(u, n):
        return u.reshape(B, S, n, HEAD_DIM)

    qa = _partial_rope(heads(qa, N_HEADS_A), cos_t, sin_t)
    ka = _partial_rope(heads(ka, N_HEADS_A), cos_t, sin_t)
    oa = _dilated_attention(qa, ka, heads(va, N_HEADS_A))
    qb = _axial_rope(_rms_norm(heads(qb, N_HEADS_B), g_q), cos_r, sin_r, cos_c, sin_c)
    kb = _axial_rope(_rms_norm(heads(kb, N_KV_B), g_k), cos_r, sin_r, cos_c, sin_c)
    ob = _grid_attention(qb, kb, heads(vb, N_KV_B))
    oc = _fourier_mix(heads(uc, N_GROUPS_C))
    o = jnp.concatenate([oa, ob, oc], axis=2)
    o = _rms_norm(o, g_heads.reshape(N_MIX_HEADS, HEAD_DIM))
    return o.reshape(B, S, D_MIX) @ w_out


def _conv_glu_ffn(h, w_gate, w_up, conv_w, conv_b, w_down):
    g = h @ w_gate
    gp = jnp.pad(g, ((0, 0), (1, 1), (0, 0)))
    g = gp[:, :-2] * conv_w[0] + gp[:, 1:-1] * conv_w[1] + gp[:, 2:] * conv_w[2] + conv_b
    return (jax.nn.gelu(g, approximate=True) * (h @ w_up)) @ w_down


def _trunk(x, g_mix_pre, g_mix_post, w_in, g_q, g_k, g_heads, w_out,
           g_ffn_pre, g_ffn_post, w_gate, w_up, conv_w, conv_b, w_down):
    tables = _position_tables(x.shape[1])
    for l in range(DEPTH):
        mix = _token_mixer(_rms_norm(x, g_mix_pre[l]), w_in[l], g_q[l], g_k[l], g_heads[l], w_out[l], tables)
        x = x + _rms_norm(mix, g_mix_post[l])
        ff = _conv_glu_ffn(_rms_norm(x, g_ffn_pre[l]), w_gate[l], w_up[l], conv_w[l], conv_b[l], w_down[l])
        x = x + _rms_norm(ff, g_ffn_post[l])
    return x


def setup_inputs(seed: int = 0) -> dict:
    key = jax.random.key(seed)
    ks = jax.random.split(key, 16)

    def nrm(k, shape, scale):
        return scale * jax.random.normal(k, shape, jnp.float32)

    def gain(k, shape):
        return 1.0 + 0.02 * jax.random.normal(k, shape, jnp.float32)

    return {
        "x_prompt": nrm(ks[0], (BATCH, SEQ, D_MODEL), 1.0),
        "x_sample": nrm(ks[1], (DEC_BATCH, DEC_SEQ, D_MODEL), 1.0),
        "g_mix_pre": gain(ks[2], (DEPTH, D_MODEL)),
        "g_mix_post": gain(ks[3], (DEPTH, D_MODEL)),
        "w_in": nrm(ks[4], (DEPTH, D_MODEL, D_IN), D_MODEL ** -0.5),
        "g_q": gain(ks[5], (DEPTH, HEAD_DIM)),
        "g_k": gain(ks[6], (DEPTH, HEAD_DIM)),
        "g_heads": gain(ks[7], (DEPTH, D_MIX)),
        "w_out": nrm(ks[8], (DEPTH, D_MIX, D_MODEL), D_MIX ** -0.5),
        "g_ffn_pre": gain(ks[9], (DEPTH, D_MODEL)),
        "g_ffn_post": gain(ks[10], (DEPTH, D_MODEL)),
        "w_gate": nrm(ks[11], (DEPTH, D_MODEL, D_FF), D_MODEL ** -0.5),
        "w_up": nrm(ks[12], (DEPTH, D_MODEL, D_FF), D_MODEL ** -0.5),
        "conv_w": nrm(ks[13], (DEPTH, CONV_W, D_FF), CONV_W ** -0.5),
        "conv_b": nrm(ks[14], (DEPTH, D_FF), 0.01),
        "w_down": nrm(ks[15], (DEPTH, D_FF, D_MODEL), D_FF ** -0.5),
    }


def reference(x_prompt, x_sample, g_mix_pre, g_mix_post, w_in, g_q, g_k, g_heads, w_out,
              g_ffn_pre, g_ffn_post, w_gate, w_up, conv_w, conv_b, w_down):
    y_prompt = _trunk(x_prompt, g_mix_pre, g_mix_post, w_in, g_q, g_k, g_heads, w_out,
                      g_ffn_pre, g_ffn_post, w_gate, w_up, conv_w, conv_b, w_down)
    y_sample = _trunk(x_sample, g_mix_pre, g_mix_post, w_in, g_q, g_k, g_heads, w_out,
                      g_ffn_pre, g_ffn_post, w_gate, w_up, conv_w, conv_b, w_down)
    return (y_prompt, y_sample)
```

```python
import functools

import numpy as np
import jax
import jax.numpy as jnp
from jax import lax
from jax.experimental import pallas as pl
from jax.experimental.pallas import tpu as pltpu

F32 = jnp.float32
BF16 = jnp.bfloat16

D_MODEL = 1024
HEAD_DIM = 64
N_HEADS_A = 4
DILATED_PATTERNS = ((128, 1), (512, 4), (2048, 16))
N_HEADS_B = 8
N_KV_B = 2
GQA_GROUP = N_HEADS_B // N_KV_B
N_GROUPS_C = 4
D_A = N_HEADS_A * HEAD_DIM
D_B = N_HEADS_B * HEAD_DIM
D_KV_B = N_KV_B * HEAD_DIM
D_C = N_GROUPS_C * HEAD_DIM
D_MIX = D_A + D_B + D_C
D_IN = 3 * D_A + D_B + 2 * D_KV_B + D_C
D_FF = 4 * D_MODEL
ROPE_THETA_PARTIAL = 500000.0
ROT_DIM_PARTIAL = HEAD_DIM // 4
ROPE_THETA_AXIAL = 10000.0
GRID_W = 64
RMS_EPS = 1e-6
NEG_INF = -1e30
SCALE = HEAD_DIM ** -0.5

LANES = 128
BF16_ROWS = 16
DIL_HALO = max(w // 2 for w, _ in DILATED_PATTERNS)
DIL_TQ = 128
FFT_N2 = 128
VMEM_LIMIT = 56 * 1024 * 1024


def _params(*sem):
    return pltpu.CompilerParams(dimension_semantics=sem, vmem_limit_bytes=VMEM_LIMIT)


def _rms(x, g):
    return x * lax.rsqrt(jnp.mean(x * x, axis=-1, keepdims=True) + RMS_EPS) * g


def _head_mean_sq(x, m_ref):
    sq = x * x
    hi = sq.astype(BF16)
    lo = (sq - hi.astype(F32)).astype(BF16)
    m = m_ref[...]
    return (jnp.dot(hi, m, preferred_element_type=F32) + jnp.dot(lo, m, preferred_element_type=F32))


def _rotate_half(x, half):
    n = x.shape[-1]
    lane = lax.broadcasted_iota(jnp.int32, (1, n), 1)
    fwd = pltpu.roll(x, n - half, axis=1)
    bwd = pltpu.roll(x, half, axis=1)
    return jnp.where((lane % (2 * half)) < half, fwd, bwd)


def _inproj_body(x_ref, g_ref, w_ref, gq_ref, gk_ref, ca_ref, sa_ref, cb_ref, sb_ref, m_ref,
                 qa_ref, ka_ref, va_ref, qb_ref, kb_ref, vb_ref, uc_ref):
    h = _rms(x_ref[...], g_ref[...]).astype(BF16)

    def proj(lo, width):
        return jnp.dot(h, w_ref[:, lo:lo + width], preferred_element_type=F32)

    ca, sa, cb, sb = ca_ref[...], sa_ref[...], cb_ref[...], sb_ref[...]

    def rope_a(p):
        return p * ca + _rotate_half(p, ROT_DIM_PARTIAL // 2) * sa

    def norm_rope_b(p, gain):
        y = p * lax.rsqrt(_head_mean_sq(p, m_ref) + RMS_EPS) * gain
        return y * cb + _rotate_half(y, HEAD_DIM // 4) * sb

    pq = proj(0, D_A)
    pk = proj(D_A, D_A)
    for c in range(D_A // LANES):
        sl = slice(c * LANES, (c + 1) * LANES)
        qa_ref[:, sl] = (rope_a(pq[:, sl]) * SCALE).astype(BF16)
        ka_ref[:, sl] = rope_a(pk[:, sl]).astype(BF16)
    va_ref[...] = proj(2 * D_A, D_A).astype(BF16)

    gq, gk = gq_ref[...], gk_ref[...]
    for c2 in range(D_B // (2 * LANES)):
        p = proj(3 * D_A + c2 * 2 * LANES, 2 * LANES)
        for c in range(2):
            sl = slice(c * LANES, (c + 1) * LANES)
            osl = slice((2 * c2 + c) * LANES, (2 * c2 + c + 1) * LANES)
            qb_ref[:, osl] = (norm_rope_b(p[:, sl], gq) * SCALE).astype(BF16)
    pkv = proj(3 * D_A + D_B, 2 * D_KV_B)
    kb_ref[...] = norm_rope_b(pkv[:, :D_KV_B], gk).astype(BF16)
    vb_ref[...] = pkv[:, D_KV_B:].astype(BF16)
    uc_ref[...] = proj(3 * D_A + D_B + 2 * D_KV_B, D_C)


def _inproj(x2d, seq, g_pre, w_in, gq, gk, tabs, head_mask, tm):
    n = x2d.shape[0]
    blocks_per_seq = seq // tm
    row = lambda width: pl.BlockSpec((tm, width), lambda i: (i, 0))
    const = lambda shape: pl.BlockSpec(shape, lambda i: (0, 0))
    tab = pl.BlockSpec((tm, LANES), lambda i: (i % blocks_per_seq, 0))
    widths = (D_A, D_A, D_A, D_B, D_KV_B, D_KV_B, D_C)
    dtypes = (BF16,) * 6 + (F32,)
    return pl.pallas_call(
        _inproj_body,
        grid=(n // tm,),
        in_specs=[row(D_MODEL), const((1, D_MODEL)), const((D_MODEL, D_IN)),
                  const((1, LANES)), const((1, LANES)), tab, tab, tab, tab, const((LANES, LANES))],
        out_specs=[row(w) for w in widths],
        out_shape=[jax.ShapeDtypeStruct((n, w), d) for w, d in zip(widths, dtypes)],
        compiler_params=_params("parallel"),
    )(x2d, g_pre, w_in, gq, gk, *tabs, head_mask)


def _dilated_bias():
    win = DIL_TQ + 2 * DIL_HALO
    delta = np.arange(win)[None, :] - DIL_HALO - np.arange(DIL_TQ)[:, None]
    mult = np.zeros(delta.shape, np.int64)
    for window, dil in DILATED_PATTERNS:
        mult += (np.abs(delta) <= (window // (2 * dil)) * dil) & (delta % dil == 0)
    return np.where(mult > 0, np.log(np.maximum(mult, 1)), NEG_INF).astype(np.float32)


def _dilated_body(q_ref, k_ref, v_ref, bias_ref, o_ref, *, seq):
    i = pl.program_id(1)
    win = DIL_TQ + 2 * DIL_HALO
    start = pl.multiple_of(i * DIL_TQ, DIL_TQ)
    kpos = i * DIL_TQ - DIL_HALO + lax.broadcasted_iota(jnp.int32, (1, win), 1)
    valid = (kpos >= 0) & (kpos < seq)
    bias = bias_ref[...]
    outs = []
    for h in range(N_HEADS_A):
        hs = slice(h * HEAD_DIM, (h + 1) * HEAD_DIM)
        q = q_ref[0, :, hs]
        kw = k_ref[0, pl.ds(start, win), hs]
        vw = v_ref[0, pl.ds(start, win), hs]
        s = lax.dot_general(q, kw, (((1,), (1,)), ((), ())), preferred_element_type=F32)
        s = jnp.where(valid, s + bias, NEG_INF)
        m = jnp.max(s, axis=-1, keepdims=True)
        p = jnp.exp(s - m)
        l = jnp.sum(p, axis=-1, keepdims=True)
        o = jnp.dot(p.astype(BF16), vw, preferred_element_type=F32)
        outs.append(o / l)
    o_ref[0] = jnp.concatenate(outs, axis=-1)


def _dilated(qa, ka_pad, va_pad, bias):
    b, seq, _ = qa.shape
    win = DIL_TQ + 2 * DIL_HALO
    kv = pl.BlockSpec((1, seq + 2 * DIL_HALO, D_A), lambda bi, i: (bi, 0, 0))
    blk = pl.BlockSpec((1, DIL_TQ, D_A), lambda bi, i: (bi, i, 0))
    return pl.pallas_call(
        functools.partial(_dilated_body, seq=seq),
        grid=(b, seq // DIL_TQ),
        in_specs=[blk, kv, kv, pl.BlockSpec((DIL_TQ, win), lambda bi, i: (0, 0))],
        out_specs=blk,
        out_shape=jax.ShapeDtypeStruct((b, seq, D_A), F32),
        compiler_params=_params("parallel", "arbitrary"),
    )(qa, ka_pad, va_pad, bias)


def _grid_attn_body(q_ref, k_ref, v_ref, o_ref, qs_ref, m_ref, l_ref, acc_ref, *, tq, tk, nk):
    for g in range(N_KV_B):
        gs = slice(g * HEAD_DIM, (g + 1) * HEAD_DIM)
        for r in range(GQA_GROUP):
            hq = g * GQA_GROUP + r
            qs_ref[r * tq:(r + 1) * tq, :] = q_ref[0, :, hq * HEAD_DIM:(hq + 1) * HEAD_DIM]
        m_ref[...] = jnp.full(m_ref.shape, NEG_INF, F32)
        l_ref[...] = jnp.zeros(l_ref.shape, F32)
        acc_ref[...] = jnp.zeros(acc_ref.shape, F32)

        def step(kb, carry):
            k0 = pl.multiple_of(kb * tk, tk)
            k = k_ref[0, pl.ds(k0, tk), gs]
            v = v_ref[0, pl.ds(k0, tk), gs]
            s = lax.dot_general(qs_ref[...], k, (((1,), (1,)), ((), ())), preferred_element_type=F32)
            m_prev = m_ref[...]
            m_new = jnp.maximum(m_prev, jnp.max(s, axis=-1, keepdims=True))
            alpha = jnp.exp(m_prev - m_new)
            p = jnp.exp(s - m_new)
            l_ref[...] = alpha * l_ref[...] + jnp.sum(p, axis=-1, keepdims=True)
            acc_ref[...] = alpha * acc_ref[...] + jnp.dot(p.astype(BF16), v, preferred_element_type=F32)
            m_ref[...] = m_new
            return carry

        lax.fori_loop(0, nk, step, 0)
        o = acc_ref[...] / l_ref[...]
        for r in range(GQA_GROUP):
            hq = g * GQA_GROUP + r
            o_ref[0, :, hq * HEAD_DIM:(hq + 1) * HEAD_DIM] = o[r * tq:(r + 1) * tq]


def _grid_attn(qb, kb, vb, tq, tk):
    b, seq, _ = qb.shape
    rows = GQA_GROUP * tq
    kv = pl.BlockSpec((1, seq, D_KV_B), lambda bi, i: (bi, 0, 0))
    blk = pl.BlockSpec((1, tq, D_B), lambda bi, i: (bi, i, 0))
    return pl.pallas_call(
        functools.partial(_grid_attn_body, tq=tq, tk=tk, nk=seq // tk),
        grid=(b, seq // tq),
        in_specs=[blk, kv, kv],
        out_specs=blk,
        out_shape=jax.ShapeDtypeStruct((b, seq, D_B), F32),
        scratch_shapes=[pltpu.VMEM((rows, HEAD_DIM), BF16), pltpu.VMEM((rows, 1), F32),
                        pltpu.VMEM((rows, 1), F32), pltpu.VMEM((rows, HEAD_DIM), F32)],
        compiler_params=_params("parallel", "arbitrary"),
    )(qb, kb, vb)


def _fft_tables(seq):
    n1, n2 = seq // FFT_N2, FFT_N2
    k1 = np.arange(n1)
    ang1 = 2.0 * np.pi * ((k1[:, None] * k1[None, :]) % n1) / n1
    f1 = np.concatenate([np.cos(ang1), -np.sin(ang1)], axis=0)
    k = k1[:, None, None] + n1 * np.arange(n2)[None, :, None]
    ang2 = 2.0 * np.pi * ((k * np.arange(n2)[None, None, :]) % seq) / seq
    g = np.concatenate([np.cos(ang2), np.sin(ang2)], axis=-1)
    c = np.arange(HEAD_DIM)
    ang3 = 2.0 * np.pi * ((c[:, None] * c[None, :]) % HEAD_DIM) / HEAD_DIM
    eye = np.eye(N_GROUPS_C)
    dc, ds = np.kron(eye, np.cos(ang3)), np.kron(eye, np.sin(ang3))
    d = np.concatenate([np.concatenate([dc, -ds], axis=1), np.concatenate([ds, dc], axis=1)], axis=0)
    return tuple(jnp.asarray(t, F32) for t in (f1, g, d))


def _fft1_body(u_ref, f_ref, a_ref):
    a_ref[0] = jnp.dot(f_ref[...], u_ref[0], precision=lax.Precision.HIGHEST, preferred_element_type=F32)


def _fft2_body(a_ref, g_ref, d_ref, o_ref, *, kb1):
    hp = lax.Precision.HIGHEST
    for j in range(kb1):
        ar, ai = a_ref[0, 0, j], a_ref[0, 1, j]
        p = (jnp.dot(ar, d_ref[:D_C, :], precision=hp, preferred_element_type=F32)
             + jnp.dot(ai, d_ref[D_C:, :], precision=hp, preferred_element_type=F32))
        gj = g_ref[j]
        z = (jnp.dot(gj[:, :FFT_N2], p[:, :D_C], precision=hp, preferred_element_type=F32)
             + jnp.dot(gj[:, FFT_N2:], p[:, D_C:], precision=hp, preferred_element_type=F32))
        o_ref[0, :, j * D_C:(j + 1) * D_C] = z


def _fourier(uc, tables):
    b, seq, _ = uc.shape
    f1, g, d = tables
    n1, n2 = seq // FFT_N2, FFT_N2
    cols = n2 * D_C
    tc = 4096
    a = pl.pallas_call(
        _fft1_body,
        grid=(b, cols // tc),
        in_specs=[pl.BlockSpec((1, n1, tc), lambda bi, j: (bi, 0, j)),
                  pl.BlockSpec((2 * n1, n1), lambda bi, j: (0, 0))],
        out_specs=pl.BlockSpec((1, 2 * n1, tc), lambda bi, j: (bi, 0, j)),
        out_shape=jax.ShapeDtypeStruct((b, 2 * n1, cols), F32),
        compiler_params=_params("parallel", "parallel"),
    )(uc.reshape(b, n1, cols), f1)
    kb1 = 8
    z = pl.pallas_call(
        functools.partial(_fft2_body, kb1=kb1),
        grid=(b, n1 // kb1),
        in_specs=[pl.BlockSpec((1, 2, kb1, n2, D_C), lambda bi, j: (bi, 0, j, 0, 0)),
                  pl.BlockSpec((kb1, n2, 2 * n2), lambda bi, j: (j, 0, 0)),
                  pl.BlockSpec((2 * D_C, 2 * D_C), lambda bi, j: (0, 0))],
        out_specs=pl.BlockSpec((1, n2, kb1 * D_C), lambda bi, j: (bi, 0, j)),
        out_shape=jax.ShapeDtypeStruct((b, n2, n1 * D_C), F32),
        compiler_params=_params("parallel", "parallel"),
    )(a.reshape(b, 2, n1, n2, D_C), g, d)
    return z.reshape(b * seq, D_C)


def _outproj_body(oa_ref, ob_ref, oc_ref, x_ref, gh_ref, w_ref, gpost_ref, gffn_ref, m_ref,
                  x1_ref, hn_ref):
    mix = None
    col = 0
    for o_ref in (oa_ref, ob_ref, oc_ref):
        for c in range(o_ref.shape[-1] // LANES):
            o = o_ref[:, c * LANES:(c + 1) * LANES]
            y = o * lax.rsqrt(_head_mean_sq(o, m_ref) + RMS_EPS) * gh_ref[:, col:col + LANES]
            part = jnp.dot(y.astype(BF16), w_ref[col:col + LANES, :], preferred_element_type=F32)
            mix = part if mix is None else mix + part
            col += LANES
    x1 = x_ref[...] + _rms(mix, gpost_ref[...])
    x1_ref[...] = x1
    hn_ref[...] = _rms(x1, gffn_ref[...]).astype(BF16)


def _outproj(oa, ob, oc, x2d, g_heads, w_out, g_post, g_ffn, head_mask, tm):
    n = x2d.shape[0]
    row = lambda width: pl.BlockSpec((tm, width), lambda i: (i, 0))
    const = lambda shape: pl.BlockSpec(shape, lambda i: (0, 0))
    return pl.pallas_call(
        _outproj_body,
        grid=(n // tm,),
        in_specs=[row(D_A), row(D_B), row(D_C), row(D_MODEL), const((1, D_MIX)),
                  const((D_MIX, D_MODEL)), const((1, D_MODEL)), const((1, D_MODEL)),
                  const((LANES, LANES))],
        out_specs=[row(D_MODEL), row(D_MODEL)],
        out_shape=[jax.ShapeDtypeStruct((n, D_MODEL), F32), jax.ShapeDtypeStruct((n, D_MODEL), BF16)],
        compiler_params=_params("parallel"),
    )(oa, ob, oc, x2d, g_heads, w_out, g_post, g_ffn, head_mask)


def _ffn_body(h_ref, hp_ref, hx_ref, x_ref, wg_ref, wu_ref, wd_ref, cw_ref, cb_ref, gpost_ref,
              o_ref, hext_ref, g_ref, acc_ref, *, tm, blocks_per_seq):
    i = pl.program_id(0)
    c = pl.program_id(1)
    pad = BF16_ROWS

    @pl.when(c == 0)
    def _():
        first = (i % blocks_per_seq) == 0
        last = (i % blocks_per_seq) == blocks_per_seq - 1
        hext_ref[:pad, :] = jnp.where(first, jnp.zeros_like(hp_ref[0]), hp_ref[0])
        hext_ref[pad:pad + tm, :] = h_ref[...]
        hext_ref[pad + tm:, :] = jnp.where(last, jnp.zeros_like(hx_ref[0]), hx_ref[0])
        acc_ref[...] = jnp.zeros(acc_ref.shape, F32)

    g_ref[...] = jnp.dot(hext_ref[...], wg_ref[0], preferred_element_type=F32)
    cw = cw_ref[0]
    g = (g_ref[pad - 1:pad - 1 + tm, :] * cw[0:1, :] + g_ref[pad:pad + tm, :] * cw[1:2, :]
         + g_ref[pad + 1:pad + 1 + tm, :] * cw[2:3, :] + cb_ref[0])
    up = jnp.dot(hext_ref[pad:pad + tm, :], wu_ref[0], preferred_element_type=F32)
    act = (jax.nn.gelu(g, approximate=True) * up).astype(BF16)
    acc_ref[...] += jnp.dot(act, wd_ref[0], preferred_element_type=F32)

    @pl.when(c == pl.num_programs(1) - 1)
    def _():
        o_ref[...] = x_ref[...] + _rms(acc_ref[...], gpost_ref[...])


def _ffn(hn, x1, seq, w_gate, w_up, w_down, conv_w, conv_b, g_post, tm, tf):
    n = x1.shape[0]
    nf = D_FF // tf
    blocks_per_seq = seq // tm
    halo_blocks = tm // BF16_ROWS
    n_halo = n // BF16_ROWS
    hn3 = hn.reshape(n_halo, BF16_ROWS, D_MODEL)
    row = lambda: pl.BlockSpec((tm, D_MODEL), lambda i, c: (i, 0))
    prev = pl.BlockSpec((1, BF16_ROWS, D_MODEL), lambda i, c: (jnp.maximum(i * halo_blocks - 1, 0), 0, 0))
    nxt = pl.BlockSpec((1, BF16_ROWS, D_MODEL),
                       lambda i, c: (jnp.minimum((i + 1) * halo_blocks, n_halo - 1), 0, 0))
    chunk = lambda shape: pl.BlockSpec((1,) + shape, lambda i, c: (c, 0, 0))
    return pl.pallas_call(
        functools.partial(_ffn_body, tm=tm, blocks_per_seq=blocks_per_seq),
        grid=(n // tm, nf),
        in_specs=[row(), prev, nxt, row(), chunk((D_MODEL, tf)), chunk((D_MODEL, tf)),
                  chunk((tf, D_MODEL)), chunk((3, tf)), chunk((1, tf)),
                  pl.BlockSpec((1, D_MODEL), lambda i, c: (0, 0))],
        out_specs=row(),
        out_shape=jax.ShapeDtypeStruct((n, D_MODEL), F32),
        scratch_shapes=[pltpu.VMEM((tm + 2 * BF16_ROWS, D_MODEL), BF16),
                        pltpu.VMEM((tm + 2 * BF16_ROWS, tf), F32),
                        pltpu.VMEM((tm, D_MODEL), F32)],
        compiler_params=_params("parallel", "arbitrary"),
    )(hn, hn3, hn3, x1, w_gate, w_up, w_down, conv_w, conv_b, g_post)


def _rope_tables(pos, rot_dim, theta):
    half = rot_dim // 2
    inv = jnp.power(jnp.float32(theta), -jnp.arange(half, dtype=jnp.float32) / half)
    ang = pos.astype(jnp.float32)[:, None] * inv[None, :]
    return jnp.cos(ang), jnp.sin(ang)


def _position_tables(seq):
    rows = seq // GRID_W
    t = jnp.arange(seq, dtype=jnp.int32)
    row = jnp.broadcast_to(jnp.arange(rows, dtype=jnp.int32)[:, None], (rows, GRID_W)).reshape(-1)
    col = jnp.broadcast_to(jnp.arange(GRID_W, dtype=jnp.int32)[None, :], (rows, GRID_W)).reshape(-1)
    cos_t, sin_t = _rope_tables(t, ROT_DIM_PARTIAL, ROPE_THETA_PARTIAL)
    cos_r, sin_r = _rope_tables(row, HEAD_DIM // 2, ROPE_THETA_AXIAL)
    cos_c, sin_c = _rope_tables(col, HEAD_DIM // 2, ROPE_THETA_AXIAL)
    rest = HEAD_DIM - ROT_DIM_PARTIAL
    ca = jnp.concatenate([cos_t, cos_t, jnp.ones((seq, rest), F32)], axis=-1)
    sa = jnp.concatenate([-sin_t, sin_t, jnp.zeros((seq, rest), F32)], axis=-1)
    cb = jnp.concatenate([cos_r, cos_r, cos_c, cos_c], axis=-1)
    sb = jnp.concatenate([-sin_r, sin_r, -sin_c, sin_c], axis=-1)
    reps = LANES // HEAD_DIM
    return tuple(jnp.tile(t_, (1, reps)) for t_ in (ca, sa, cb, sb))


def _head_mask():
    lane = np.arange(LANES)
    same = (lane[:, None] // HEAD_DIM) == (lane[None, :] // HEAD_DIM)
    return jnp.asarray(same / HEAD_DIM, BF16)


def _trunk(x, w):
    b, seq, _ = x.shape
    n = b * seq
    tm = 512
    tm_ffn = 1024
    tf = 512
    tabs = _position_tables(seq)
    fft_tabs = _fft_tables(seq)
    head_mask = _head_mask()
    bias = jnp.asarray(_dilated_bias())
    x2d = x.reshape(n, D_MODEL)
    depth = w["w_in"].shape[0]
    for l in range(depth):
        tile2 = lambda g_: jnp.tile(g_[l][None, :], (1, LANES // HEAD_DIM))
        qa, ka, va, qb, kb, vb, uc = _inproj(
            x2d, seq, w["g_mix_pre"][l][None], w["w_in"][l], tile2(w["g_q"]), tile2(w["g_k"]),
            tabs, head_mask, tm)
        pad = lambda t_: jnp.pad(t_.reshape(b, seq, D_A), ((0, 0), (DIL_HALO, DIL_HALO), (0, 0)))
        oa = _dilated(qa.reshape(b, seq, D_A), pad(ka), pad(va), bias).reshape(n, D_A)
        ob = _grid_attn(qb.reshape(b, seq, D_B), kb.reshape(b, seq, D_KV_B), vb.reshape(b, seq, D_KV_B),
                        tq=128, tk=512).reshape(n, D_B)
        oc = _fourier(uc.reshape(b, seq, D_C), fft_tabs)
        x1, hn = _outproj(oa, ob, oc, x2d, w["g_heads"][l][None], w["w_out"][l], w["g_mix_post"][l][None],
                          w["g_ffn_pre"][l][None], head_mask, tm)
        x2d = _ffn(hn, x1, seq, w["w_gate"][l], w["w_up"][l], w["w_down"][l], w["conv_w"][l],
                   w["conv_b"][l], w["g_ffn_post"][l][None], tm_ffn, tf)
    return x2d.reshape(b, seq, D_MODEL)


def _prep_weights(g_mix_pre, g_mix_post, w_in, g_q, g_k, g_heads, w_out, g_ffn_pre, g_ffn_post,
                  w_gate, w_up, conv_w, conv_b, w_down, tf):
    depth = w_in.shape[0]
    nf = D_FF // tf
    chunk_cols = lambda t_: t_.reshape(depth, t_.shape[1], nf, tf).transpose(0, 2, 1, 3)
    return dict(
        g_mix_pre=g_mix_pre, g_mix_post=g_mix_post, g_q=g_q, g_k=g_k, g_heads=g_heads,
        g_ffn_pre=g_ffn_pre, g_ffn_post=g_ffn_post,
        w_in=w_in.astype(BF16), w_out=w_out.astype(BF16),
        w_gate=chunk_cols(w_gate.astype(BF16)), w_up=chunk_cols(w_up.astype(BF16)),
        w_down=w_down.astype(BF16).reshape(depth, nf, tf, D_MODEL),
        conv_w=chunk_cols(conv_w), conv_b=conv_b.reshape(depth, nf, 1, tf))


def kernel(x_prompt, x_sample, g_mix_pre, g_mix_post, w_in, g_q, g_k, g_heads, w_out, g_ffn_pre, g_ffn_post,
           w_gate, w_up, conv_w, conv_b, w_down):
    w = _prep_weights(g_mix_pre, g_mix_post, w_in, g_q, g_k, g_heads, w_out, g_ffn_pre, g_ffn_post,
                      w_gate, w_up, conv_w, conv_b, w_down, tf=512)
    return (_trunk(x_prompt, w), _trunk(x_sample, w))
```

```python
import functools

import numpy as np
import jax
import jax.numpy as jnp
from jax import lax
from jax.experimental import pallas as pl
from jax.experimental.pallas import tpu as pltpu

F32 = jnp.float32
BF16 = jnp.bfloat16

D_MODEL = 1024
HEAD_DIM = 64
N_HEADS_A = 4
DILATED_PATTERNS = ((128, 1), (512, 4), (2048, 16))
N_HEADS_B = 8
N_KV_B = 2
GQA_GROUP = N_HEADS_B // N_KV_B
N_GROUPS_C = 4
D_A = N_HEADS_A * HEAD_DIM
D_B = N_HEADS_B * HEAD_DIM
D_KV_B = N_KV_B * HEAD_DIM
D_C = N_GROUPS_C * HEAD_DIM
D_MIX = D_A + D_B + D_C
D_IN = 3 * D_A + D_B + 2 * D_KV_B + D_C
D_FF = 4 * D_MODEL
ROPE_THETA_PARTIAL = 500000.0
ROT_DIM_PARTIAL = HEAD_DIM // 4
ROPE_THETA_AXIAL = 10000.0
GRID_W = 64
RMS_EPS = 1e-6
NEG_INF = -1e30
Q_SCALE = HEAD_DIM ** -0.5 * float(np.log2(np.e))

LANES = 128
BF16_ROWS = 16
DIL_HALO = max(w // 2 for w, _ in DILATED_PATTERNS)
DIL_TQ = 128
FFT_N2 = 128
VMEM_LIMIT = 56 * 1024 * 1024


def _params(*sem):
    return pltpu.CompilerParams(dimension_semantics=sem, vmem_limit_bytes=VMEM_LIMIT)


def _rms(x, g):
    return x * lax.rsqrt(jnp.mean(x * x, axis=-1, keepdims=True) + RMS_EPS) * g


def _head_mean_sq(x, m_ref):
    sq = x * x
    hi = sq.astype(BF16)
    lo = (sq - hi.astype(F32)).astype(BF16)
    m = m_ref[...]
    return (jnp.dot(hi, m, preferred_element_type=F32) + jnp.dot(lo, m, preferred_element_type=F32))


def _rotate_half(x, half):
    n = x.shape[-1]
    lane = lax.broadcasted_iota(jnp.int32, (1, n), 1)
    fwd = pltpu.roll(x, n - half, axis=1)
    bwd = pltpu.roll(x, half, axis=1)
    return jnp.where((lane % (2 * half)) < half, fwd, bwd)


def _inproj_body(x_ref, g_ref, w_ref, gq_ref, gk_ref, ca_ref, sa_ref, cb_ref, sb_ref, m_ref,
                 qa_ref, ka_ref, va_ref, qb_ref, kb_ref, vb_ref, uc_ref):
    h = _rms(x_ref[...], g_ref[...]).astype(BF16)

    def proj(lo, width):
        return jnp.dot(h, w_ref[:, lo:lo + width], preferred_element_type=F32)

    ca, sa, cb, sb = ca_ref[...], sa_ref[...], cb_ref[...], sb_ref[...]

    def rope_a(p):
        return p * ca + _rotate_half(p, ROT_DIM_PARTIAL // 2) * sa

    def norm_rope_b(p, gain):
        y = p * lax.rsqrt(_head_mean_sq(p, m_ref) + RMS_EPS) * gain
        return y * cb + _rotate_half(y, HEAD_DIM // 4) * sb

    pq = proj(0, D_A)
    pk = proj(D_A, D_A)
    for c in range(D_A // LANES):
        sl = slice(c * LANES, (c + 1) * LANES)
        qa_ref[:, sl] = (rope_a(pq[:, sl]) * Q_SCALE).astype(BF16)
        ka_ref[:, sl] = rope_a(pk[:, sl]).astype(BF16)
    va_ref[...] = proj(2 * D_A, D_A).astype(BF16)

    gq, gk = gq_ref[...], gk_ref[...]
    for c2 in range(D_B // (2 * LANES)):
        p = proj(3 * D_A + c2 * 2 * LANES, 2 * LANES)
        for c in range(2):
            sl = slice(c * LANES, (c + 1) * LANES)
            osl = slice((2 * c2 + c) * LANES, (2 * c2 + c + 1) * LANES)
            qb_ref[:, osl] = (norm_rope_b(p[:, sl], gq) * Q_SCALE).astype(BF16)
    pkv = proj(3 * D_A + D_B, 2 * D_KV_B)
    kb_ref[...] = norm_rope_b(pkv[:, :D_KV_B], gk).astype(BF16)
    vb_ref[...] = pkv[:, D_KV_B:].astype(BF16)
    uc_ref[...] = proj(3 * D_A + D_B + 2 * D_KV_B, D_C)


def _inproj(x2d, seq, g_pre, w_in, gq, gk, tabs, head_mask, tm):
    n = x2d.shape[0]
    blocks_per_seq = seq // tm
    row = lambda width: pl.BlockSpec((tm, width), lambda i: (i, 0))
    const = lambda shape: pl.BlockSpec(shape, lambda i: (0, 0))
    tab = pl.BlockSpec((tm, LANES), lambda i: (i % blocks_per_seq, 0))
    widths = (D_A, D_A, D_A, D_B, D_KV_B, D_KV_B, D_C)
    dtypes = (BF16,) * 6 + (F32,)
    return pl.pallas_call(
        _inproj_body,
        grid=(n // tm,),
        in_specs=[row(D_MODEL), const((1, D_MODEL)), const((D_MODEL, D_IN)),
                  const((1, LANES)), const((1, LANES)), tab, tab, tab, tab, const((LANES, LANES))],
        out_specs=[row(w) for w in widths],
        out_shape=[jax.ShapeDtypeStruct((n, w), d) for w, d in zip(widths, dtypes)],
        name="inproj",
        compiler_params=_params("parallel"),
    )(x2d, g_pre, w_in, gq, gk, *tabs, head_mask)


def _dilated_bias():
    win = DIL_TQ + 2 * DIL_HALO
    delta = np.arange(win)[None, :] - DIL_HALO - np.arange(DIL_TQ)[:, None]
    mult = np.zeros(delta.shape, np.int64)
    for window, dil in DILATED_PATTERNS:
        mult += (np.abs(delta) <= (window // (2 * dil)) * dil) & (delta % dil == 0)
    return np.where(mult > 0, np.log2(np.maximum(mult, 1)), NEG_INF).astype(np.float32)


def _dilated_body(q_ref, k_ref, v_ref, bias_ref, o_ref, *, seq):
    i = pl.program_id(1)
    win = DIL_TQ + 2 * DIL_HALO
    start = pl.multiple_of(i * DIL_TQ, DIL_TQ)
    kpos = i * DIL_TQ - DIL_HALO + lax.broadcasted_iota(jnp.int32, (1, win), 1)
    valid = (kpos >= 0) & (kpos < seq)
    bias = bias_ref[...]
    outs = []
    for h in range(N_HEADS_A):
        hs = slice(h * HEAD_DIM, (h + 1) * HEAD_DIM)
        q = q_ref[0, :, hs]
        kw = k_ref[0, pl.ds(start, win), hs]
        vw = v_ref[0, pl.ds(start, win), hs]
        s = lax.dot_general(q, kw, (((1,), (1,)), ((), ())), preferred_element_type=F32)
        s = jnp.where(valid, s + bias, NEG_INF)
        m = jnp.max(s, axis=-1, keepdims=True)
        p = jnp.exp2(s - m)
        l = jnp.sum(p, axis=-1, keepdims=True)
        o = jnp.dot(p.astype(BF16), vw, preferred_element_type=F32)
        outs.append(o / l)
    o_ref[0] = jnp.concatenate(outs, axis=-1)


def _dilated(qa, ka_pad, va_pad, bias):
    b, seq, _ = qa.shape
    win = DIL_TQ + 2 * DIL_HALO
    kv = pl.BlockSpec((1, seq + 2 * DIL_HALO, D_A), lambda bi, i: (bi, 0, 0))
    blk = pl.BlockSpec((1, DIL_TQ, D_A), lambda bi, i: (bi, i, 0))
    return pl.pallas_call(
        functools.partial(_dilated_body, seq=seq),
        grid=(b, seq // DIL_TQ),
        in_specs=[blk, kv, kv, pl.BlockSpec((DIL_TQ, win), lambda bi, i: (0, 0))],
        out_specs=blk,
        out_shape=jax.ShapeDtypeStruct((b, seq, D_A), F32),
        name="dilated",
        compiler_params=_params("parallel", "arbitrary"),
    )(qa, ka_pad, va_pad, bias)


def _grid_attn_body(qt_ref, k_ref, vt_ref, o_ref, s0_ref, s1_ref, acc_ref, *, tq, tk, nk):
    cols = acc_ref.shape[1]
    for g in range(N_KV_B):
        gs = slice(g * HEAD_DIM, (g + 1) * HEAD_DIM)
        qt = qt_ref[0, 0, g]

        def scores(kb):
            k0 = pl.multiple_of(kb * tk, tk)
            return jnp.dot(k_ref[0, g, pl.ds(k0, tk), :], qt, preferred_element_type=F32)

        def consume(s_ref, kb, m_prev, l_prev):
            s = s_ref[...]
            m_new = jnp.maximum(m_prev, jnp.max(s, axis=0, keepdims=True))
            alpha = jnp.exp2(m_prev - m_new)
            p = jnp.exp2(s - m_new)
            l_new = alpha * l_prev + jnp.sum(p, axis=0, keepdims=True)
            pv = jnp.dot(vt_ref[0, g, kb], p.astype(BF16), preferred_element_type=F32)
            acc_ref[gs, :] = alpha * acc_ref[gs, :] + pv
            return m_new, l_new

        s0_ref[...] = scores(0)
        acc_ref[gs, :] = jnp.zeros((HEAD_DIM, cols), F32)

        def pair(j, carry):
            m, l = carry
            kb = 2 * j
            s1_ref[...] = scores(kb + 1)
            m, l = consume(s0_ref, kb, m, l)
            s0_ref[...] = scores(jnp.minimum(kb + 2, nk - 1))
            m, l = consume(s1_ref, kb + 1, m, l)
            return m, l

        init = (jnp.full((1, cols), NEG_INF, F32), jnp.zeros((1, cols), F32))
        _, l_fin = lax.fori_loop(0, nk // 2, pair, init)
        acc_ref[gs, :] = acc_ref[gs, :] / l_fin
    ot = acc_ref[...].T
    for g in range(N_KV_B):
        for r in range(GQA_GROUP):
            hq = g * GQA_GROUP + r
            o_ref[0, :, hq * HEAD_DIM:(hq + 1) * HEAD_DIM] = ot[r * tq:(r + 1) * tq, g * HEAD_DIM:(g + 1) * HEAD_DIM]


def _grid_attn(qb, kb, vb, tq, tk):
    b, seq, _ = qb.shape
    cols = GQA_GROUP * tq
    nq, nk = seq // tq, seq // tk
    qt = qb.reshape(b, nq, tq, N_KV_B, GQA_GROUP, HEAD_DIM).transpose(0, 1, 3, 5, 4, 2)
    qt = qt.reshape(b, nq, N_KV_B, HEAD_DIM, cols)
    kh = kb.reshape(b, seq, N_KV_B, HEAD_DIM).transpose(0, 2, 1, 3)
    vt = vb.reshape(b, nk, tk, N_KV_B, HEAD_DIM).transpose(0, 3, 1, 4, 2)
    return pl.pallas_call(
        functools.partial(_grid_attn_body, tq=tq, tk=tk, nk=nk),
        grid=(b, nq),
        in_specs=[pl.BlockSpec((1, 1, N_KV_B, HEAD_DIM, cols), lambda bi, i: (bi, i, 0, 0, 0)),
                  pl.BlockSpec((1, N_KV_B, seq, HEAD_DIM), lambda bi, i: (bi, 0, 0, 0)),
                  pl.BlockSpec((1, N_KV_B, nk, HEAD_DIM, tk), lambda bi, i: (bi, 0, 0, 0, 0))],
        out_specs=pl.BlockSpec((1, tq, D_B), lambda bi, i: (bi, i, 0)),
        out_shape=jax.ShapeDtypeStruct((b, seq, D_B), F32),
        scratch_shapes=[pltpu.VMEM((tk, cols), F32), pltpu.VMEM((tk, cols), F32),
                        pltpu.VMEM((N_KV_B * HEAD_DIM, cols), F32)],
        name="grid_attn",
        compiler_params=_params("parallel", "arbitrary"),
    )(qt, kh, vt)


def _fft_tables(seq):
    n1, n2 = seq // FFT_N2, FFT_N2
    k1 = np.arange(n1)
    ang1 = 2.0 * np.pi * ((k1[:, None] * k1[None, :]) % n1) / n1
    f1 = np.concatenate([np.cos(ang1), -np.sin(ang1)], axis=0)
    k = k1[:, None, None] + n1 * np.arange(n2)[None, :, None]
    ang2 = 2.0 * np.pi * ((k * np.arange(n2)[None, None, :]) % seq) / seq
    g = np.concatenate([np.cos(ang2), np.sin(ang2)], axis=-1)
    c = np.arange(HEAD_DIM)
    ang3 = 2.0 * np.pi * ((c[:, None] * c[None, :]) % HEAD_DIM) / HEAD_DIM
    eye = np.eye(N_GROUPS_C)
    dc, ds = np.kron(eye, np.cos(ang3)), np.kron(eye, np.sin(ang3))
    d = np.concatenate([np.concatenate([dc, -ds], axis=1), np.concatenate([ds, dc], axis=1)], axis=0)
    return tuple(jnp.asarray(t, F32) for t in (f1, g, d))


def _fft1_body(u_ref, f_ref, a_ref):
    a_ref[0] = jnp.dot(f_ref[...], u_ref[0], precision=lax.Precision.HIGHEST, preferred_element_type=F32)


def _fft2_body(a_ref, g_ref, d_ref, o_ref, *, kb1):
    hp = lax.Precision.HIGHEST
    for j in range(kb1):
        ar, ai = a_ref[0, 0, j], a_ref[0, 1, j]
        p = (jnp.dot(ar, d_ref[:D_C, :], precision=hp, preferred_element_type=F32)
             + jnp.dot(ai, d_ref[D_C:, :], precision=hp, preferred_element_type=F32))
        gj = g_ref[j]
        z = (jnp.dot(gj[:, :FFT_N2], p[:, :D_C], precision=hp, preferred_element_type=F32)
             + jnp.dot(gj[:, FFT_N2:], p[:, D_C:], precision=hp, preferred_element_type=F32))
        o_ref[0, :, j * D_C:(j + 1) * D_C] = z


def _fourier(uc, tables):
    b, seq, _ = uc.shape
    f1, g, d = tables
    n1, n2 = seq // FFT_N2, FFT_N2
    cols = n2 * D_C
    tc = 4096
    a = pl.pallas_call(
        _fft1_body,
        grid=(b, cols // tc),
        in_specs=[pl.BlockSpec((1, n1, tc), lambda bi, j: (bi, 0, j)),
                  pl.BlockSpec((2 * n1, n1), lambda bi, j: (0, 0))],
        out_specs=pl.BlockSpec((1, 2 * n1, tc), lambda bi, j: (bi, 0, j)),
        out_shape=jax.ShapeDtypeStruct((b, 2 * n1, cols), F32),
        name="fft1",
        compiler_params=_params("parallel", "parallel"),
    )(uc.reshape(b, n1, cols), f1)
    kb1 = 8
    z = pl.pallas_call(
        functools.partial(_fft2_body, kb1=kb1),
        grid=(b, n1 // kb1),
        in_specs=[pl.BlockSpec((1, 2, kb1, n2, D_C), lambda bi, j: (bi, 0, j, 0, 0)),
                  pl.BlockSpec((kb1, n2, 2 * n2), lambda bi, j: (j, 0, 0)),
                  pl.BlockSpec((2 * D_C, 2 * D_C), lambda bi, j: (0, 0))],
        out_specs=pl.BlockSpec((1, n2, kb1 * D_C), lambda bi, j: (bi, 0, j)),
        out_shape=jax.ShapeDtypeStruct((b, n2, n1 * D_C), F32),
        name="fft2",
        compiler_params=_params("parallel", "parallel"),
    )(a.reshape(b, 2, n1, n2, D_C), g, d)
    return z.reshape(b * seq, D_C)


def _outproj_body(oa_ref, ob_ref, oc_ref, x_ref, gh_ref, w_ref, gpost_ref, gffn_ref, m_ref,
                  x1_ref, hn_ref):
    mix = None
    col = 0
    for o_ref in (oa_ref, ob_ref, oc_ref):
        for c in range(o_ref.shape[-1] // LANES):
            o = o_ref[:, c * LANES:(c + 1) * LANES]
            y = o * lax.rsqrt(_head_mean_sq(o, m_ref) + RMS_EPS) * gh_ref[:, col:col + LANES]
            part = jnp.dot(y.astype(BF16), w_ref[col:col + LANES, :], preferred_element_type=F32)
            mix = part if mix is None else mix + part
            col += LANES
    x1 = x_ref[...] + _rms(mix, gpost_ref[...])
    x1_ref[...] = x1
    hn_ref[...] = _rms(x1, gffn_ref[...]).astype(BF16)


def _outproj(oa, ob, oc, x2d, g_heads, w_out, g_post, g_ffn, head_mask, tm):
    n = x2d.shape[0]
    row = lambda width: pl.BlockSpec((tm, width), lambda i: (i, 0))
    const = lambda shape: pl.BlockSpec(shape, lambda i: (0, 0))
    return pl.pallas_call(
        _outproj_body,
        grid=(n // tm,),
        in_specs=[row(D_A), row(D_B), row(D_C), row(D_MODEL), const((1, D_MIX)),
                  const((D_MIX, D_MODEL)), const((1, D_MODEL)), const((1, D_MODEL)),
                  const((LANES, LANES))],
        out_specs=[row(D_MODEL), row(D_MODEL)],
        out_shape=[jax.ShapeDtypeStruct((n, D_MODEL), F32), jax.ShapeDtypeStruct((n, D_MODEL), BF16)],
        name="outproj",
        compiler_params=_params("parallel"),
    )(oa, ob, oc, x2d, g_heads, w_out, g_post, g_ffn, head_mask)


def _ffn_body(h_ref, hp_ref, hx_ref, x_ref, wg_ref, wu_ref, wd_ref, cw_ref, cb_ref, gpost_ref,
              o_ref, hext_ref, g_ref, acc_ref, *, tm, blocks_per_seq):
    i = pl.program_id(0)
    c = pl.program_id(1)
    pad = BF16_ROWS

    @pl.when(c == 0)
    def _():
        first = (i % blocks_per_seq) == 0
        last = (i % blocks_per_seq) == blocks_per_seq - 1
        hext_ref[:pad, :] = jnp.where(first, jnp.zeros_like(hp_ref[0]), hp_ref[0])
        hext_ref[pad:pad + tm, :] = h_ref[...]
        hext_ref[pad + tm:, :] = jnp.where(last, jnp.zeros_like(hx_ref[0]), hx_ref[0])
        acc_ref[...] = jnp.zeros(acc_ref.shape, F32)

    g_ref[...] = jnp.dot(hext_ref[...], wg_ref[0], preferred_element_type=F32)
    cw = cw_ref[0]
    g = (g_ref[pad - 1:pad - 1 + tm, :] * cw[0:1, :] + g_ref[pad:pad + tm, :] * cw[1:2, :]
         + g_ref[pad + 1:pad + 1 + tm, :] * cw[2:3, :] + cb_ref[0])
    up = jnp.dot(hext_ref[pad:pad + tm, :], wu_ref[0], preferred_element_type=F32)
    act = (jax.nn.gelu(g, approximate=True) * up).astype(BF16)
    acc_ref[...] += jnp.dot(act, wd_ref[0], preferred_element_type=F32)

    @pl.when(c == pl.num_programs(1) - 1)
    def _():
        o_ref[...] = x_ref[...] + _rms(acc_ref[...], gpost_ref[...])


def _ffn(hn, x1, seq, w_gate, w_up, w_down, conv_w, conv_b, g_post, tm, tf):
    n = x1.shape[0]
    nf = D_FF // tf
    blocks_per_seq = seq // tm
    halo_blocks = tm // BF16_ROWS
    n_halo = n // BF16_ROWS
    hn3 = hn.reshape(n_halo, BF16_ROWS, D_MODEL)
    row = lambda: pl.BlockSpec((tm, D_MODEL), lambda i, c: (i, 0))
    prev = pl.BlockSpec((1, BF16_ROWS, D_MODEL), lambda i, c: (jnp.maximum(i * halo_blocks - 1, 0), 0, 0))
    nxt = pl.BlockSpec((1, BF16_ROWS, D_MODEL),
                       lambda i, c: (jnp.minimum((i + 1) * halo_blocks, n_halo - 1), 0, 0))
    chunk = lambda shape: pl.BlockSpec((1,) + shape, lambda i, c: (c, 0, 0))
    return pl.pallas_call(
        functools.partial(_ffn_body, tm=tm, blocks_per_seq=blocks_per_seq),
        grid=(n // tm, nf),
        in_specs=[row(), prev, nxt, row(), chunk((D_MODEL, tf)), chunk((D_MODEL, tf)),
                  chunk((tf, D_MODEL)), chunk((3, tf)), chunk((1, tf)),
                  pl.BlockSpec((1, D_MODEL), lambda i, c: (0, 0))],
        out_specs=row(),
        out_shape=jax.ShapeDtypeStruct((n, D_MODEL), F32),
        scratch_shapes=[pltpu.VMEM((tm + 2 * BF16_ROWS, D_MODEL), BF16),
                        pltpu.VMEM((tm + 2 * BF16_ROWS, tf), F32),
                        pltpu.VMEM((tm, D_MODEL), F32)],
        name="ffn",
        compiler_params=_params("parallel", "arbitrary"),
    )(hn, hn3, hn3, x1, w_gate, w_up, w_down, conv_w, conv_b, g_post)


def _rope_tables(pos, rot_dim, theta):
    half = rot_dim // 2
    inv = jnp.power(jnp.float32(theta), -jnp.arange(half, dtype=jnp.float32) / half)
    ang = pos.astype(jnp.float32)[:, None] * inv[None, :]
    return jnp.cos(ang), jnp.sin(ang)


def _position_tables(seq):
    rows = seq // GRID_W
    t = jnp.arange(seq, dtype=jnp.int32)
    row = jnp.broadcast_to(jnp.arange(rows, dtype=jnp.int32)[:, None], (rows, GRID_W)).reshape(-1)
    col = jnp.broadcast_to(jnp.arange(GRID_W, dtype=jnp.int32)[None, :], (rows, GRID_W)).reshape(-1)
    cos_t, sin_t = _rope_tables(t, ROT_DIM_PARTIAL, ROPE_THETA_PARTIAL)
    cos_r, sin_r = _rope_tables(row, HEAD_DIM // 2, ROPE_THETA_AXIAL)
    cos_c, sin_c = _rope_tables(col, HEAD_DIM // 2, ROPE_THETA_AXIAL)
    rest = HEAD_DIM - ROT_DIM_PARTIAL
    ca = jnp.concatenate([cos_t, cos_t, jnp.ones((seq, rest), F32)], axis=-1)
    sa = jnp.concatenate([-sin_t, sin_t, jnp.zeros((seq, rest), F32)], axis=-1)
    cb = jnp.concatenate([cos_r, cos_r, cos_c, cos_c], axis=-1)
    sb = jnp.concatenate([-sin_r, sin_r, -sin_c, sin_c], axis=-1)
    reps = LANES // HEAD_DIM
    return tuple(jnp.tile(t_, (1, reps)) for t_ in (ca, sa, cb, sb))


def _head_mask():
    lane = np.arange(LANES)
    same = (lane[:, None] // HEAD_DIM) == (lane[None, :] // HEAD_DIM)
    return jnp.asarray(same / HEAD_DIM, BF16)


def _trunk(x, w):
    b, seq, _ = x.shape
    n = b * seq
    tm = 512
    tm_ffn = 1024
    tf = 512
    tabs = _position_tables(seq)
    fft_tabs = _fft_tables(seq)
    head_mask = _head_mask()
    bias = jnp.asarray(_dilated_bias())
    x2d = x.reshape(n, D_MODEL)
    depth = w["w_in"].shape[0]
    for l in range(depth):
        tile2 = lambda g_: jnp.tile(g_[l][None, :], (1, LANES // HEAD_DIM))
        qa, ka, va, qb, kb, vb, uc = _inproj(
            x2d, seq, w["g_mix_pre"][l][None], w["w_in"][l], tile2(w["g_q"]), tile2(w["g_k"]),
            tabs, head_mask, tm)
        pad = lambda t_: jnp.pad(t_.reshape(b, seq, D_A), ((0, 0), (DIL_HALO, DIL_HALO), (0, 0)))
        oa = _dilated(qa.reshape(b, seq, D_A), pad(ka), pad(va), bias).reshape(n, D_A)
        ob = _grid_attn(qb.reshape(b, seq, D_B), kb.reshape(b, seq, D_KV_B), vb.reshape(b, seq, D_KV_B),
                        tq=128, tk=512).reshape(n, D_B)
        oc = _fourier(uc.reshape(b, seq, D_C), fft_tabs)
        x1, hn = _outproj(oa, ob, oc, x2d, w["g_heads"][l][None], w["w_out"][l], w["g_mix_post"][l][None],
                          w["g_ffn_pre"][l][None], head_mask, tm)
        x2d = _ffn(hn, x1, seq, w["w_gate"][l], w["w_up"][l], w["w_down"][l], w["conv_w"][l],
                   w["conv_b"][l], w["g_ffn_post"][l][None], tm_ffn, tf)
    return x2d.reshape(b, seq, D_MODEL)


def _prep_weights(g_mix_pre, g_mix_post, w_in, g_q, g_k, g_heads, w_out, g_ffn_pre, g_ffn_post,
                  w_gate, w_up, conv_w, conv_b, w_down, tf):
    depth = w_in.shape[0]
    nf = D_FF // tf
    chunk_cols = lambda t_: t_.reshape(depth, t_.shape[1], nf, tf).transpose(0, 2, 1, 3)
    return dict(
        g_mix_pre=g_mix_pre, g_mix_post=g_mix_post, g_q=g_q, g_k=g_k, g_heads=g_heads,
        g_ffn_pre=g_ffn_pre, g_ffn_post=g_ffn_post,
        w_in=w_in.astype(BF16), w_out=w_out.astype(BF16),
        w_gate=chunk_cols(w_gate.astype(BF16)), w_up=chunk_cols(w_up.astype(BF16)),
        w_down=w_down.astype(BF16).reshape(depth, nf, tf, D_MODEL),
        conv_w=chunk_cols(conv_w), conv_b=conv_b.reshape(depth, nf, 1, tf))


def kernel(x_prompt, x_sample, g_mix_pre, g_mix_post, w_in, g_q, g_k, g_heads, w_out, g_ffn_pre, g_ffn_post,
           w_gate, w_up, conv_w, conv_b, w_down):
    w = _prep_weights(g_mix_pre, g_mix_post, w_in, g_q, g_k, g_heads, w_out, g_ffn_pre, g_ffn_post,
                      w_gate, w_up, conv_w, conv_b, w_down, tf=512)
    return (_trunk(x_prompt, w), _trunk(x_sample, w))
```

```python
import functools

import numpy as np
import jax
import jax.numpy as jnp
from jax import lax
from jax.experimental import pallas as pl
from jax.experimental.pallas import tpu as pltpu

F32 = jnp.float32
BF16 = jnp.bfloat16

D_MODEL = 1024
HEAD_DIM = 64
N_HEADS_A = 4
DILATED_PATTERNS = ((128, 1), (512, 4), (2048, 16))
N_HEADS_B = 8
N_KV_B = 2
GQA_GROUP = N_HEADS_B // N_KV_B
N_GROUPS_C = 4
D_A = N_HEADS_A * HEAD_DIM
D_B = N_HEADS_B * HEAD_DIM
D_KV_B = N_KV_B * HEAD_DIM
D_C = N_GROUPS_C * HEAD_DIM
D_MIX = D_A + D_B + D_C
D_IN = 3 * D_A + D_B + 2 * D_KV_B + D_C
D_FF = 4 * D_MODEL
ROPE_THETA_PARTIAL = 500000.0
ROT_DIM_PARTIAL = HEAD_DIM // 4
ROPE_THETA_AXIAL = 10000.0
GRID_W = 64
RMS_EPS = 1e-6
NEG_INF = -1e30
Q_SCALE = HEAD_DIM ** -0.5 * float(np.log2(np.e))

LANES = 128
BF16_ROWS = 16
DIL_HALO = max(w // 2 for w, _ in DILATED_PATTERNS)
DIL_TQ = 128
DIL_NEAR_TILES = 4
FFT_N2 = 128
VMEM_LIMIT = 56 * 1024 * 1024
GELU_C0 = float(np.sqrt(2.0 / np.pi))
GELU_C1 = 0.044715 * GELU_C0


def _params(*sem):
    return pltpu.CompilerParams(dimension_semantics=sem, vmem_limit_bytes=VMEM_LIMIT)


def _rms(x, g):
    return x * lax.rsqrt(jnp.mean(x * x, axis=-1, keepdims=True) + RMS_EPS) * g


def _head_mean_sq(x, m_ref):
    return jnp.dot((x * x).astype(BF16), m_ref[...], preferred_element_type=F32)


def _rotate_half(x, half):
    n = x.shape[-1]
    lane = lax.broadcasted_iota(jnp.int32, (1, n), 1)
    fwd = pltpu.roll(x, n - half, axis=1)
    bwd = pltpu.roll(x, half, axis=1)
    return jnp.where((lane % (2 * half)) < half, fwd, bwd)


def _inproj_body(x_ref, g_ref, w_ref, gq_ref, gk_ref, ca_ref, sa_ref, cb_ref, sb_ref, m_ref,
                 qa_ref, ka_ref, va_ref, qb_ref, kb_ref, vb_ref, uc_ref):
    h = _rms(x_ref[...], g_ref[...]).astype(BF16)

    def proj(lo, width):
        return jnp.dot(h, w_ref[:, lo:lo + width], preferred_element_type=F32)

    ca, sa, cb, sb = ca_ref[...], sa_ref[...], cb_ref[...], sb_ref[...]

    def rope_a(p):
        return p * ca + _rotate_half(p, ROT_DIM_PARTIAL // 2) * sa

    def norm_b(p, gain):
        return p * lax.rsqrt(_head_mean_sq(p, m_ref) + RMS_EPS) * gain

    def rope_b(y):
        return y * cb + _rotate_half(y, HEAD_DIM // 4) * sb

    pq = proj(0, D_A)
    pk = proj(D_A, D_A)
    for c in range(D_A // LANES):
        sl = slice(c * LANES, (c + 1) * LANES)
        qa_ref[:, sl] = (rope_a(pq[:, sl]) * Q_SCALE).astype(BF16)
        ka_ref[:, sl] = rope_a(pk[:, sl]).astype(BF16)
    va_ref[...] = proj(2 * D_A, D_A).astype(BF16)

    gq, gk = gq_ref[...], gk_ref[...]
    for c2 in range(D_B // (2 * LANES)):
        y = norm_b(proj(3 * D_A + c2 * 2 * LANES, 2 * LANES), gq)
        for c in range(2):
            sl = slice(c * LANES, (c + 1) * LANES)
            osl = slice((2 * c2 + c) * LANES, (2 * c2 + c + 1) * LANES)
            qb_ref[:, osl] = (rope_b(y[:, sl]) * Q_SCALE).astype(BF16)
    pkv = proj(3 * D_A + D_B, 2 * D_KV_B)
    kb_ref[...] = rope_b(norm_b(pkv, gk)[:, :D_KV_B]).astype(BF16)
    vb_ref[...] = pkv[:, D_KV_B:].astype(BF16)
    uc_ref[...] = proj(3 * D_A + D_B + 2 * D_KV_B, D_C)


def _inproj(x2d, seq, g_pre, w_in, gq, gk, tabs, head_mask, tm):
    n = x2d.shape[0]
    blocks_per_seq = seq // tm
    row = lambda width: pl.BlockSpec((tm, width), lambda i: (i, 0))
    const = lambda shape: pl.BlockSpec(shape, lambda i: (0, 0))
    tab = pl.BlockSpec((tm, LANES), lambda i: (i % blocks_per_seq, 0))
    widths = (D_A, D_A, D_A, D_B, D_KV_B, D_KV_B, D_C)
    dtypes = (BF16,) * 6 + (F32,)
    return pl.pallas_call(
        _inproj_body,
        grid=(n // tm,),
        in_specs=[row(D_MODEL), const((1, D_MODEL)), const((D_MODEL, D_IN)),
                  const((1, 2 * LANES)), const((1, 2 * LANES)), tab, tab, tab, tab,
                  const((2 * LANES, 2 * LANES))],
        out_specs=[row(w) for w in widths],
        out_shape=[jax.ShapeDtypeStruct((n, w), d) for w, d in zip(widths, dtypes)],
        name="inproj",
        compiler_params=_params("parallel"),
    )(x2d, g_pre, w_in, gq, gk, *tabs, head_mask)


DIL_FAR = DILATED_PATTERNS[-1]
DIL_NEAR = DILATED_PATTERNS[:-1]
NEAR_HALO = max(w // 2 for w, _ in DIL_NEAR)
FAR_HALO = DIL_FAR[0] // (2 * DIL_FAR[1])
STAT_LANES = LANES


def _band_bias(tq, halo, patterns):
    delta = np.arange(tq + 2 * halo)[None, :] - halo - np.arange(tq)[:, None]
    mult = np.zeros(delta.shape, np.int64)
    for window, dil in patterns:
        mult += (np.abs(delta) <= (window // (2 * dil)) * dil) & (delta % dil == 0)
    return np.where(mult > 0, np.log2(np.maximum(mult, 1)), NEG_INF).astype(np.float32)


def _band_attention_t(q, kw, vtw, bias_t, valid):
    tq = q.shape[0]
    qt = q.astype(F32).T.astype(BF16)
    rows = lax.broadcasted_iota(jnp.int32, (D_A, N_HEADS_A * tq), 0) // HEAD_DIM
    cols = lax.broadcasted_iota(jnp.int32, (D_A, N_HEADS_A * tq), 1) // tq
    qbd = jnp.where(rows == cols, jnp.concatenate([qt] * N_HEADS_A, axis=1), jnp.zeros((), BF16))
    st = jnp.dot(kw, qbd, preferred_element_type=F32)
    st = jnp.where(valid, st + bias_t, NEG_INF)
    m = jnp.max(st, axis=0, keepdims=True)
    p = jnp.exp2(st - m)
    l = jnp.sum(p, axis=0, keepdims=True)
    ot_all = jnp.dot(vtw, p.astype(BF16), preferred_element_type=F32)
    ot = jnp.concatenate([ot_all[h * HEAD_DIM:(h + 1) * HEAD_DIM, h * tq:(h + 1) * tq]
                          for h in range(N_HEADS_A)], axis=0)
    return ot, m, l


def _dilated_near_body(q_ref, k_ref, vt_ref, bias_ref, acc_ref, stat_ref, *, seq, tiles):
    tq = DIL_TQ
    win = tq + 2 * NEAR_HALO
    for t in range(tiles):
        q0 = (pl.program_id(1) * tiles + t) * tq
        start = pl.multiple_of(q0 + (DIL_HALO - NEAR_HALO), tq)
        kpos = q0 - NEAR_HALO + lax.broadcasted_iota(jnp.int32, (win, 1), 0)
        valid = (kpos >= 0) & (kpos < seq)
        ot, m, l = _band_attention_t(q_ref[0, t * tq:(t + 1) * tq, :], k_ref[0, pl.ds(start, win), :],
                                     vt_ref[0, :, pl.ds(start, win)], bias_ref[...], valid)
        acc_ref[0, t * tq:(t + 1) * tq, :] = ot.T
        stat_rows = ([m[:, h * tq:(h + 1) * tq] for h in range(N_HEADS_A)]
                     + [l[:, h * tq:(h + 1) * tq] for h in range(N_HEADS_A)]
                     + [jnp.zeros((STAT_LANES - 2 * N_HEADS_A, tq), F32)])
        stat_ref[0, t * tq:(t + 1) * tq, :] = jnp.concatenate(stat_rows, axis=0).T


def _dilated_far_body(q_ref, k_ref, vt_ref, acc_ref, stat_ref, bias_ref, o_ref, *, cls_len):
    tq = DIL_TQ
    win = tq + 2 * FAR_HALO
    for t in range(cls_len // tq):
        rs = slice(t * tq, (t + 1) * tq)
        kpos = t * tq - FAR_HALO + lax.broadcasted_iota(jnp.int32, (win, 1), 0)
        valid = (kpos >= 0) & (kpos < cls_len)
        ot2, m2, l2 = _band_attention_t(q_ref[0, rs, :], k_ref[0, t * tq:t * tq + win, :],
                                        vt_ref[0, 0, :, t * tq:t * tq + win], bias_ref[...], valid)
        ot1 = acc_ref[0, rs, :].T
        st1 = stat_ref[0, rs, :].T
        outs = []
        for h in range(N_HEADS_A):
            hs = slice(h * HEAD_DIM, (h + 1) * HEAD_DIM)
            cs = slice(h * tq, (h + 1) * tq)
            m1, l1 = st1[h:h + 1, :], st1[N_HEADS_A + h:N_HEADS_A + h + 1, :]
            m = jnp.maximum(m1, m2[:, cs])
            w1, w2 = jnp.exp2(m1 - m), jnp.exp2(m2[:, cs] - m)
            outs.append((w1 * ot1[hs, :] + w2 * ot2[hs, :]) / (w1 * l1 + w2 * l2[:, cs]))
        o_ref[0, rs, :] = jnp.concatenate(outs, axis=0).T


def _dilated(qa, ka_pad, va_pad):
    b, seq, _ = qa.shape
    dil = DIL_FAR[1]
    cls_len = seq // dil
    tq = DIL_TQ
    tiles = DIL_NEAR_TILES
    seq_pad = seq + 2 * DIL_HALO
    cls_pad = cls_len + 2 * FAR_HALO
    bias_t = lambda halo, patterns: jnp.asarray(np.tile(_band_bias(tq, halo, patterns).T, (1, N_HEADS_A)))
    blk = lambda width: pl.BlockSpec((1, tiles * tq, width), lambda bi, i: (bi, i, 0))
    acc, stats = pl.pallas_call(
        functools.partial(_dilated_near_body, seq=seq, tiles=tiles),
        grid=(b, seq // (tiles * tq)),
        in_specs=[blk(D_A), pl.BlockSpec((1, seq_pad, D_A), lambda bi, i: (bi, 0, 0)),
                  pl.BlockSpec((1, D_A, seq_pad), lambda bi, i: (bi, 0, 0)),
                  pl.BlockSpec((tq + 2 * NEAR_HALO, N_HEADS_A * tq), lambda bi, i: (0, 0))],
        out_specs=[blk(D_A), blk(STAT_LANES)],
        out_shape=[jax.ShapeDtypeStruct((b, seq, D_A), F32), jax.ShapeDtypeStruct((b, seq, STAT_LANES), F32)],
        name="dilated_near",
        compiler_params=_params("parallel", "arbitrary"),
    )(qa, ka_pad, va_pad.transpose(0, 2, 1), bias_t(NEAR_HALO, DIL_NEAR))
    view = lambda t_, width: t_.reshape(b, t_.shape[1] // dil, dil * width)
    cls_blk = lambda width: pl.BlockSpec((1, cls_len, width), lambda bi, r: (bi, 0, r))
    vt_cls = va_pad.reshape(b, cls_pad, dil, D_A).transpose(0, 2, 3, 1)
    out = pl.pallas_call(
        functools.partial(_dilated_far_body, cls_len=cls_len),
        grid=(b, dil),
        in_specs=[cls_blk(D_A), pl.BlockSpec((1, cls_pad, D_A), lambda bi, r: (bi, 0, r)),
                  pl.BlockSpec((1, 1, D_A, cls_pad), lambda bi, r: (bi, r, 0, 0)),
                  cls_blk(D_A), cls_blk(STAT_LANES),
                  pl.BlockSpec((tq + 2 * FAR_HALO, N_HEADS_A * tq), lambda bi, r: (0, 0))],
        out_specs=cls_blk(D_A),
        out_shape=jax.ShapeDtypeStruct((b, cls_len, dil * D_A), F32),
        name="dilated_far",
        compiler_params=_params("parallel", "parallel"),
    )(view(qa, D_A), view(ka_pad, D_A), vt_cls, view(acc, D_A), view(stats, STAT_LANES),
      bias_t(FAR_HALO, ((2 * FAR_HALO, 1),)))
    return out.reshape(b, seq, D_A)


def _grid_attn_body(qt_ref, k_ref, vt_ref, o_ref, s0_ref, s1_ref, acc_ref, *, tq, tk, nk):
    cols = acc_ref.shape[2]
    for g in range(N_KV_B):
        qt = qt_ref[0, 0, g]

        def scores(kb):
            k0 = pl.multiple_of(kb * tk, tk)
            return jnp.dot(k_ref[0, g, pl.ds(k0, tk), :], qt, preferred_element_type=F32)

        def consume(s_ref, kb, m_prev):
            s = s_ref[...]
            m_new = jnp.maximum(m_prev, jnp.max(s, axis=0, keepdims=True))
            alpha = jnp.exp2(m_prev - m_new)
            p = jnp.exp2(s - m_new).astype(BF16)
            pv = jnp.dot(vt_ref[0, g, kb], p, preferred_element_type=F32)
            acc_ref[g] = alpha * acc_ref[g] + pv
            return m_new

        s0_ref[...] = scores(0)
        acc_ref[g] = jnp.zeros(acc_ref.shape[1:], F32)

        def pair(j, m):
            kb = 2 * j
            s1_ref[...] = scores(kb + 1)
            m = consume(s0_ref, kb, m)
            s0_ref[...] = scores(jnp.minimum(kb + 2, nk - 1))
            return consume(s1_ref, kb + 1, m)

        lax.fori_loop(0, nk // 2, pair, jnp.full((1, cols), NEG_INF, F32))
    outs = [acc_ref[g, :HEAD_DIM, :] / acc_ref[g, HEAD_DIM:HEAD_DIM + 1, :] for g in range(N_KV_B)]
    ot = jnp.concatenate(outs, axis=0).T
    for g in range(N_KV_B):
        for r in range(GQA_GROUP):
            hq = g * GQA_GROUP + r
            o_ref[0, :, hq * HEAD_DIM:(hq + 1) * HEAD_DIM] = ot[r * tq:(r + 1) * tq, g * HEAD_DIM:(g + 1) * HEAD_DIM]


def _grid_attn(qb, kb, vb, tq, tk):
    b, seq, _ = qb.shape
    cols = GQA_GROUP * tq
    nq, nk = seq // tq, seq // tk
    qt = qb.reshape(b, nq, tq, N_KV_B, GQA_GROUP, HEAD_DIM).transpose(0, 1, 3, 5, 4, 2)
    qt = qt.reshape(b, nq, N_KV_B, HEAD_DIM, cols)
    kh = kb.reshape(b, seq, N_KV_B, HEAD_DIM).transpose(0, 2, 1, 3)
    vt = vb.reshape(b, nk, tk, N_KV_B, HEAD_DIM).transpose(0, 3, 1, 4, 2)
    vt = jnp.concatenate([vt, jnp.ones((b, N_KV_B, nk, BF16_ROWS, tk), BF16)], axis=3)
    return pl.pallas_call(
        functools.partial(_grid_attn_body, tq=tq, tk=tk, nk=nk),
        grid=(b, nq),
        in_specs=[pl.BlockSpec((1, 1, N_KV_B, HEAD_DIM, cols), lambda bi, i: (bi, i, 0, 0, 0)),
                  pl.BlockSpec((1, N_KV_B, seq, HEAD_DIM), lambda bi, i: (bi, 0, 0, 0)),
                  pl.BlockSpec((1, N_KV_B, nk, HEAD_DIM + BF16_ROWS, tk), lambda bi, i: (bi, 0, 0, 0, 0))],
        out_specs=pl.BlockSpec((1, tq, D_B), lambda bi, i: (bi, i, 0)),
        out_shape=jax.ShapeDtypeStruct((b, seq, D_B), F32),
        scratch_shapes=[pltpu.VMEM((tk, cols), F32), pltpu.VMEM((tk, cols), F32),
                        pltpu.VMEM((N_KV_B, HEAD_DIM + BF16_ROWS, cols), F32)],
        name="grid_attn",
        compiler_params=_params("parallel", "arbitrary"),
    )(qt, kh, vt)


def _fft_tables(seq):
    n1, n2 = seq // FFT_N2, FFT_N2
    k1 = np.arange(n1)
    ang1 = 2.0 * np.pi * ((k1[:, None] * k1[None, :]) % n1) / n1
    f1 = np.concatenate([np.cos(ang1), -np.sin(ang1)], axis=0)
    k = k1[:, None, None] + n1 * np.arange(n2)[None, :, None]
    ang2 = 2.0 * np.pi * ((k * np.arange(n2)[None, None, :]) % seq) / seq
    g = np.concatenate([np.cos(ang2), np.sin(ang2)], axis=-1)
    c = np.arange(HEAD_DIM)
    ang3 = 2.0 * np.pi * ((c[:, None] * c[None, :]) % HEAD_DIM) / HEAD_DIM
    eye = np.eye(N_GROUPS_C)
    dc, ds = np.kron(eye, np.cos(ang3)), np.kron(eye, np.sin(ang3))
    d = np.concatenate([np.concatenate([dc, -ds], axis=1), np.concatenate([ds, dc], axis=1)], axis=0)
    return tuple(jnp.asarray(t, BF16) for t in (f1, g, d))


def _fft1_body(u_ref, f_ref, a_ref):
    a_ref[0] = jnp.dot(f_ref[...], u_ref[0].astype(BF16), preferred_element_type=F32).astype(BF16)


def _fft2_body(a_ref, g_ref, d_ref, o_ref, *, kb1):
    for j in range(kb1):
        ar, ai = a_ref[0, 0, j], a_ref[0, 1, j]
        p = (jnp.dot(ar, d_ref[:D_C, :], preferred_element_type=F32)
             + jnp.dot(ai, d_ref[D_C:, :], preferred_element_type=F32)).astype(BF16)
        gj = g_ref[j]
        o_ref[0, :, j * D_C:(j + 1) * D_C] = (
            jnp.dot(gj[:, :FFT_N2], p[:, :D_C], preferred_element_type=F32)
            + jnp.dot(gj[:, FFT_N2:], p[:, D_C:], preferred_element_type=F32))


def _fourier(uc, tables):
    b, seq, _ = uc.shape
    f1, g, d = tables
    n1, n2 = seq // FFT_N2, FFT_N2
    cols = n2 * D_C
    tc = 4096
    a = pl.pallas_call(
        _fft1_body,
        grid=(b, cols // tc),
        in_specs=[pl.BlockSpec((1, n1, tc), lambda bi, j: (bi, 0, j)),
                  pl.BlockSpec((2 * n1, n1), lambda bi, j: (0, 0))],
        out_specs=pl.BlockSpec((1, 2 * n1, tc), lambda bi, j: (bi, 0, j)),
        out_shape=jax.ShapeDtypeStruct((b, 2 * n1, cols), BF16),
        name="fft1",
        compiler_params=_params("parallel", "parallel"),
    )(uc.reshape(b, n1, cols), f1)
    kb1 = 8
    z = pl.pallas_call(
        functools.partial(_fft2_body, kb1=kb1),
        grid=(b, n1 // kb1),
        in_specs=[pl.BlockSpec((1, 2, kb1, n2, D_C), lambda bi, j: (bi, 0, j, 0, 0)),
                  pl.BlockSpec((kb1, n2, 2 * n2), lambda bi, j: (j, 0, 0)),
                  pl.BlockSpec((2 * D_C, 2 * D_C), lambda bi, j: (0, 0))],
        out_specs=pl.BlockSpec((1, n2, kb1 * D_C), lambda bi, j: (bi, 0, j)),
        out_shape=jax.ShapeDtypeStruct((b, n2, n1 * D_C), F32),
        name="fft2",
        compiler_params=_params("parallel", "parallel"),
    )(a.reshape(b, 2, n1, n2, D_C), g, d)
    return z.reshape(b * seq, D_C)


def _outproj_body(oa_ref, ob_ref, oc_ref, x_ref, gh_ref, w_ref, gpost_ref, gffn_ref, m_ref,
                  x1_ref, hn_ref):
    mix = None
    col = 0
    width = 2 * LANES
    for o_ref in (oa_ref, ob_ref, oc_ref):
        for c in range(o_ref.shape[-1] // width):
            o = o_ref[:, c * width:(c + 1) * width]
            y = o * lax.rsqrt(_head_mean_sq(o, m_ref) + RMS_EPS) * gh_ref[:, col:col + width]
            part = jnp.dot(y.astype(BF16), w_ref[col:col + width, :], preferred_element_type=F32)
            mix = part if mix is None else mix + part
            col += width
    x1 = x_ref[...] + _rms(mix, gpost_ref[...])
    x1_ref[...] = x1
    hn_ref[...] = _rms(x1, gffn_ref[...]).astype(BF16)


def _outproj(oa, ob, oc, x2d, g_heads, w_out, g_post, g_ffn, head_mask, tm):
    n = x2d.shape[0]
    row = lambda width: pl.BlockSpec((tm, width), lambda i: (i, 0))
    const = lambda shape: pl.BlockSpec(shape, lambda i: (0, 0))
    return pl.pallas_call(
        _outproj_body,
        grid=(n // tm,),
        in_specs=[row(D_A), row(D_B), row(D_C), row(D_MODEL), const((1, D_MIX)),
                  const((D_MIX, D_MODEL)), const((1, D_MODEL)), const((1, D_MODEL)),
                  const((2 * LANES, 2 * LANES))],
        out_specs=[row(D_MODEL), row(D_MODEL)],
        out_shape=[jax.ShapeDtypeStruct((n, D_MODEL), F32), jax.ShapeDtypeStruct((n, D_MODEL), BF16)],
        name="outproj",
        compiler_params=_params("parallel"),
    )(oa, ob, oc, x2d, g_heads, w_out, g_post, g_ffn, head_mask)


def _ffn_body(h_ref, hp_ref, hx_ref, x_ref, wg_ref, wu_ref, wd_ref, cw_ref, cb_ref, gpost_ref,
              o_ref, hext_ref, acc_ref, *, tm, blocks_per_seq):
    i = pl.program_id(0)
    c = pl.program_id(1)
    pad = BF16_ROWS
    rows = tm + 2 * pad

    @pl.when(c == 0)
    def _():
        first = (i % blocks_per_seq) == 0
        last = (i % blocks_per_seq) == blocks_per_seq - 1
        hext_ref[:pad, :] = jnp.where(first, jnp.zeros_like(hp_ref[0]), hp_ref[0])
        hext_ref[pad:pad + tm, :] = h_ref[...]
        hext_ref[pad + tm:, :] = jnp.where(last, jnp.zeros_like(hx_ref[0]), hx_ref[0])
        acc_ref[...] = jnp.zeros(acc_ref.shape, F32)

    g = jnp.dot(hext_ref[...], wg_ref[0], preferred_element_type=F32)
    cw = cw_ref[0]
    g = (pltpu.roll(g, 1, axis=0)[pad:pad + tm] * cw[0:1, :] + g[pad:pad + tm] * cw[1:2, :]
         + pltpu.roll(g, rows - 1, axis=0)[pad:pad + tm] * cw[2:3, :] + cb_ref[0])
    up = jnp.dot(h_ref[...], wu_ref[0], preferred_element_type=F32)
    inner = g * (GELU_C0 + GELU_C1 * (g * g))
    act = (g * (1.0 + jnp.tanh(inner)) * up).astype(BF16)
    acc_ref[...] += jnp.dot(act, wd_ref[0], preferred_element_type=F32)

    @pl.when(c == pl.num_programs(1) - 1)
    def _():
        o_ref[...] = x_ref[...] + _rms(acc_ref[...], gpost_ref[...])


def _ffn(hn, x1, seq, w_gate, w_up, w_down, conv_w, conv_b, g_post, tm, tf):
    n = x1.shape[0]
    nf = D_FF // tf
    blocks_per_seq = seq // tm
    halo_blocks = tm // BF16_ROWS
    n_halo = n // BF16_ROWS
    hn3 = hn.reshape(n_halo, BF16_ROWS, D_MODEL)
    row = lambda: pl.BlockSpec((tm, D_MODEL), lambda i, c: (i, 0))
    prev = pl.BlockSpec((1, BF16_ROWS, D_MODEL), lambda i, c: (jnp.maximum(i * halo_blocks - 1, 0), 0, 0))
    nxt = pl.BlockSpec((1, BF16_ROWS, D_MODEL),
                       lambda i, c: (jnp.minimum((i + 1) * halo_blocks, n_halo - 1), 0, 0))
    chunk = lambda shape: pl.BlockSpec((1,) + shape, lambda i, c: (c, 0, 0))
    return pl.pallas_call(
        functools.partial(_ffn_body, tm=tm, blocks_per_seq=blocks_per_seq),
        grid=(n // tm, nf),
        in_specs=[row(), prev, nxt, row(), chunk((D_MODEL, tf)), chunk((D_MODEL, tf)),
                  chunk((tf, D_MODEL)), chunk((3, tf)), chunk((1, tf)),
                  pl.BlockSpec((1, D_MODEL), lambda i, c: (0, 0))],
        out_specs=row(),
        out_shape=jax.ShapeDtypeStruct((n, D_MODEL), F32),
        scratch_shapes=[pltpu.VMEM((tm + 2 * BF16_ROWS, D_MODEL), BF16), pltpu.VMEM((tm, D_MODEL), F32)],
        name="ffn",
        compiler_params=_params("parallel", "arbitrary"),
    )(hn, hn3, hn3, x1, w_gate, w_up, w_down, conv_w, conv_b, g_post)


def _rope_tables(pos, rot_dim, theta):
    half = rot_dim // 2
    inv = jnp.power(jnp.float32(theta), -jnp.arange(half, dtype=jnp.float32) / half)
    ang = pos.astype(jnp.float32)[:, None] * inv[None, :]
    return jnp.cos(ang), jnp.sin(ang)


def _position_tables(seq):
    rows = seq // GRID_W
    t = jnp.arange(seq, dtype=jnp.int32)
    row = jnp.broadcast_to(jnp.arange(rows, dtype=jnp.int32)[:, None], (rows, GRID_W)).reshape(-1)
    col = jnp.broadcast_to(jnp.arange(GRID_W, dtype=jnp.int32)[None, :], (rows, GRID_W)).reshape(-1)
    cos_t, sin_t = _rope_tables(t, ROT_DIM_PARTIAL, ROPE_THETA_PARTIAL)
    cos_r, sin_r = _rope_tables(row, HEAD_DIM // 2, ROPE_THETA_AXIAL)
    cos_c, sin_c = _rope_tables(col, HEAD_DIM // 2, ROPE_THETA_AXIAL)
    rest = HEAD_DIM - ROT_DIM_PARTIAL
    ca = jnp.concatenate([cos_t, cos_t, jnp.ones((seq, rest), F32)], axis=-1)
    sa = jnp.concatenate([-sin_t, sin_t, jnp.zeros((seq, rest), F32)], axis=-1)
    cb = jnp.concatenate([cos_r, cos_r, cos_c, cos_c], axis=-1)
    sb = jnp.concatenate([-sin_r, sin_r, -sin_c, sin_c], axis=-1)
    reps = LANES // HEAD_DIM
    return tuple(jnp.tile(t_, (1, reps)) for t_ in (ca, sa, cb, sb))


def _head_mask():
    lane = np.arange(2 * LANES)
    same = (lane[:, None] // HEAD_DIM) == (lane[None, :] // HEAD_DIM)
    return jnp.asarray(same / HEAD_DIM, BF16)


def _trunk(x, w):
    b, seq, _ = x.shape
    n = b * seq
    tm = 512
    tm_ffn = 1024
    tf = 512
    tabs = _position_tables(seq)
    fft_tabs = _fft_tables(seq)
    head_mask = _head_mask()
    x2d = x.reshape(n, D_MODEL)
    depth = w["w_in"].shape[0]
    for l in range(depth):
        tile2 = lambda g_: jnp.tile(g_[l][None, :], (1, 2 * LANES // HEAD_DIM))
        qa, ka, va, qb, kb, vb, uc = _inproj(
            x2d, seq, w["g_mix_pre"][l][None], w["w_in"][l], tile2(w["g_q"]), tile2(w["g_k"]),
            tabs, head_mask, tm)
        pad = lambda t_: jnp.pad(t_.reshape(b, seq, D_A), ((0, 0), (DIL_HALO, DIL_HALO), (0, 0)))
        oa = _dilated(qa.reshape(b, seq, D_A), pad(ka), pad(va)).reshape(n, D_A)
        ob = _grid_attn(qb.reshape(b, seq, D_B), kb.reshape(b, seq, D_KV_B), vb.reshape(b, seq, D_KV_B),
                        tq=128, tk=512).reshape(n, D_B)
        oc = _fourier(uc.reshape(b, seq, D_C), fft_tabs)
        x1, hn = _outproj(oa, ob, oc, x2d, w["g_heads"][l][None], w["w_out"][l], w["g_mix_post"][l][None],
                          w["g_ffn_pre"][l][None], head_mask, tm)
        x2d = _ffn(hn, x1, seq, w["w_gate"][l], w["w_up"][l], w["w_down"][l], w["conv_w"][l],
                   w["conv_b"][l], w["g_ffn_post"][l][None], tm_ffn, tf)
    return x2d.reshape(b, seq, D_MODEL)


def _prep_weights(g_mix_pre, g_mix_post, w_in, g_q, g_k, g_heads, w_out, g_ffn_pre, g_ffn_post,
                  w_gate, w_up, conv_w, conv_b, w_down, tf):
    depth = w_in.shape[0]
    nf = D_FF // tf
    chunk_cols = lambda t_: t_.reshape(depth, t_.shape[1], nf, tf).transpose(0, 2, 1, 3)
    return dict(
        g_mix_pre=g_mix_pre, g_mix_post=g_mix_post, g_q=g_q, g_k=g_k, g_heads=g_heads,
        g_ffn_pre=g_ffn_pre, g_ffn_post=g_ffn_post,
        w_in=w_in.astype(BF16), w_out=w_out.astype(BF16),
        w_gate=chunk_cols(w_gate.astype(BF16)), w_up=chunk_cols((0.5 * w_up).astype(BF16)),
        w_down=w_down.astype(BF16).reshape(depth, nf, tf, D_MODEL),
        conv_w=chunk_cols(conv_w), conv_b=conv_b.reshape(depth, nf, 1, tf))


def kernel(x_prompt, x_sample, g_mix_pre, g_mix_post, w_in, g_q, g_k, g_heads, w_out, g_ffn_pre, g_ffn_post,
           w_gate, w_up, conv_w, conv_b, w_down):
    w = _prep_weights(g_mix_pre, g_mix_post, w_in, g_q, g_k, g_heads, w_out, g_ffn_pre, g_ffn_post,
                      w_gate, w_up, conv_w, conv_b, w_down, tf=512)
    return (_trunk(x_prompt, w), _trunk(x_sample, w))
```

```python
import functools

import numpy as np
import jax
import jax.numpy as jnp
from jax import lax
from jax.experimental import pallas as pl
from jax.experimental.pallas import tpu as pltpu

F32 = jnp.float32
BF16 = jnp.bfloat16

D_MODEL = 1024
HEAD_DIM = 64
N_HEADS_A = 4
DILATED_PATTERNS = ((128, 1), (512, 4), (2048, 16))
N_HEADS_B = 8
N_KV_B = 2
GQA_GROUP = N_HEADS_B // N_KV_B
N_GROUPS_C = 4
D_A = N_HEADS_A * HEAD_DIM
D_B = N_HEADS_B * HEAD_DIM
D_KV_B = N_KV_B * HEAD_DIM
D_C = N_GROUPS_C * HEAD_DIM
D_MIX = D_A + D_B + D_C
D_IN = 3 * D_A + D_B + 2 * D_KV_B + D_C
D_FF = 4 * D_MODEL
ROPE_THETA_PARTIAL = 500000.0
ROT_DIM_PARTIAL = HEAD_DIM // 4
ROPE_THETA_AXIAL = 10000.0
GRID_W = 64
RMS_EPS = 1e-6
NEG_INF = -1e30
Q_SCALE = HEAD_DIM ** -0.5 * float(np.log2(np.e))

LANES = 128
BF16_ROWS = 16
DIL_HALO = max(w // 2 for w, _ in DILATED_PATTERNS)
DIL_TQ = 128
DIL_NEAR_TILES = 4
FFT_N2 = 128
VMEM_LIMIT = 56 * 1024 * 1024
TOKEN_TILE = 512
GRID_TQ = 128
FFN_TILE = 1024
FFN_CHUNK = 512
GELU_C0 = float(np.sqrt(2.0 / np.pi))
GELU_C1 = 0.044715 * GELU_C0


def _params(*sem):
    return pltpu.CompilerParams(dimension_semantics=sem, vmem_limit_bytes=VMEM_LIMIT)


def _rms(x, g):
    return x * lax.rsqrt(jnp.mean(x * x, axis=-1, keepdims=True) + RMS_EPS) * g


def _head_mean_sq(x, m_ref):
    return jnp.dot((x * x).astype(BF16), m_ref[...], preferred_element_type=F32)


def _rotate_half(x, half):
    n = x.shape[-1]
    lane = lax.broadcasted_iota(jnp.int32, (1, n), 1)
    fwd = pltpu.roll(x, n - half, axis=1)
    bwd = pltpu.roll(x, half, axis=1)
    return jnp.where((lane % (2 * half)) < half, fwd, bwd)


def _inproj_body(x_ref, g_ref, w_ref, gq_ref, gk_ref, ca_ref, sa_ref, cb_ref, sb_ref, m_ref,
                 ka_zero, va_zero, vat_zero,
                 qa_ref, ka_ref, va_ref, vat_ref, qt_ref, kb_ref, vt_ref, uc_ref, *, tq):
    del ka_zero, va_zero, vat_zero
    tm = x_ref.shape[0]
    h = _rms(x_ref[...], g_ref[...]).astype(BF16)

    def proj(lo, width):
        return jnp.dot(h, w_ref[:, lo:lo + width], preferred_element_type=F32)

    ca, sa, cb, sb = ca_ref[...], sa_ref[...], cb_ref[...], sb_ref[...]

    def rope_a(p):
        return p * ca + _rotate_half(p, ROT_DIM_PARTIAL // 2) * sa

    def norm_b(p, gain):
        return p * lax.rsqrt(_head_mean_sq(p, m_ref) + RMS_EPS) * gain

    def rope_b(y):
        return y * cb + _rotate_half(y, HEAD_DIM // 4) * sb

    pq = proj(0, D_A)
    pk = proj(D_A, D_A)
    for c in range(D_A // LANES):
        sl = slice(c * LANES, (c + 1) * LANES)
        qa_ref[:, sl] = (rope_a(pq[:, sl]) * Q_SCALE).astype(BF16)
        ka_ref[0, :, sl] = rope_a(pk[:, sl]).astype(BF16)
    pv = proj(2 * D_A, D_A)
    va_ref[0] = pv.astype(BF16)
    vat_ref[0] = pv.T.astype(BF16)

    gq, gk = gq_ref[...], gk_ref[...]
    zeros = jnp.zeros((HEAD_DIM, GQA_GROUP * tq), BF16)
    for g in range(N_KV_B):
        y = norm_b(proj(3 * D_A + g * 2 * LANES, 2 * LANES), gq)
        for c in range(2):
            yt = (rope_b(y[:, c * LANES:(c + 1) * LANES]) * Q_SCALE).T.astype(BF16)
            for hh in range(LANES // HEAD_DIM):
                r = c * (LANES // HEAD_DIM) + hh
                for j in range(tm // tq):
                    qt_ref[j, g, g * HEAD_DIM:(g + 1) * HEAD_DIM, r * tq:(r + 1) * tq] = (
                        yt[hh * HEAD_DIM:(hh + 1) * HEAD_DIM, j * tq:(j + 1) * tq])
        for j in range(tm // tq):
            qt_ref[j, g, (1 - g) * HEAD_DIM:(2 - g) * HEAD_DIM, :] = zeros
    pkv = proj(3 * D_A + D_B, 2 * D_KV_B)
    kb_ref[...] = rope_b(norm_b(pkv, gk)[:, :D_KV_B]).astype(BF16)
    vbt = pkv[:, D_KV_B:].T.astype(BF16)
    for g in range(N_KV_B):
        vt_ref[0, g, :HEAD_DIM, :] = vbt[g * HEAD_DIM:(g + 1) * HEAD_DIM, :]
        vt_ref[0, g, HEAD_DIM:, :] = jnp.ones((BF16_ROWS, tm), BF16)
    uc_ref[...] = proj(3 * D_A + D_B + 2 * D_KV_B, D_C)


def _inproj(x2d, batch, seq, g_pre, w_in, gq, gk, tabs, head_mask, tm, tq):
    n = x2d.shape[0]
    bps = seq // tm
    halo_blocks = DIL_HALO // tm
    seq_pad = seq + 2 * DIL_HALO
    cols = GQA_GROUP * tq
    row = lambda width: pl.BlockSpec((tm, width), lambda i: (i, 0))
    const = lambda shape: pl.BlockSpec(shape, lambda i: (0, 0))
    tab = pl.BlockSpec((tm, LANES), lambda i: (i % bps, 0))
    pad_rows = pl.BlockSpec((1, tm, D_A), lambda i: (i // bps, i % bps + halo_blocks, 0))
    pad_cols = pl.BlockSpec((1, D_A, tm), lambda i: (i // bps, 0, i % bps + halo_blocks))
    out_specs = [row(D_A), pad_rows, pad_rows, pad_cols,
                 pl.BlockSpec((tm // tq, N_KV_B, 2 * HEAD_DIM, cols), lambda i: (i, 0, 0, 0)),
                 row(D_KV_B),
                 pl.BlockSpec((1, N_KV_B, HEAD_DIM + BF16_ROWS, tm), lambda i: (i, 0, 0, 0)),
                 row(D_C)]
    out_shape = [jax.ShapeDtypeStruct((n, D_A), BF16),
                 jax.ShapeDtypeStruct((batch, seq_pad, D_A), BF16),
                 jax.ShapeDtypeStruct((batch, seq_pad, D_A), BF16),
                 jax.ShapeDtypeStruct((batch, D_A, seq_pad), BF16),
                 jax.ShapeDtypeStruct((n // tq, N_KV_B, 2 * HEAD_DIM, cols), BF16),
                 jax.ShapeDtypeStruct((n, D_KV_B), BF16),
                 jax.ShapeDtypeStruct((n // tm, N_KV_B, HEAD_DIM + BF16_ROWS, tm), BF16),
                 jax.ShapeDtypeStruct((n, D_C), F32)]
    padded = (1, 2, 3)
    zero_bufs = [jnp.zeros(out_shape[k].shape, BF16) for k in padded]
    n_in = 10
    return pl.pallas_call(
        functools.partial(_inproj_body, tq=tq),
        grid=(n // tm,),
        in_specs=[row(D_MODEL), const((1, D_MODEL)), const((D_MODEL, D_IN)),
                  const((1, 2 * LANES)), const((1, 2 * LANES)), tab, tab, tab, tab,
                  const((2 * LANES, 2 * LANES))] + [pl.BlockSpec(memory_space=pl.ANY)] * len(padded),
        out_specs=out_specs,
        out_shape=out_shape,
        input_output_aliases={n_in + k: o for k, o in enumerate(padded)},
        name="inproj",
        compiler_params=_params("parallel"),
    )(x2d, g_pre, w_in, gq, gk, *tabs, head_mask, *zero_bufs)


DIL_FAR = DILATED_PATTERNS[-1]
DIL_NEAR = DILATED_PATTERNS[:-1]
NEAR_HALO = max(w // 2 for w, _ in DIL_NEAR)
FAR_HALO = DIL_FAR[0] // (2 * DIL_FAR[1])
STAT_LANES = LANES


def _band_bias(tq, halo, patterns):
    delta = np.arange(tq + 2 * halo)[None, :] - halo - np.arange(tq)[:, None]
    mult = np.zeros(delta.shape, np.int64)
    for window, dil in patterns:
        mult += (np.abs(delta) <= (window // (2 * dil)) * dil) & (delta % dil == 0)
    return np.where(mult > 0, np.log2(np.maximum(mult, 1)), NEG_INF).astype(np.float32)


def _band_attention_t(q, kw, vtw, bias_t, valid):
    tq = q.shape[0]
    qt = q.astype(F32).T.astype(BF16)
    rows = lax.broadcasted_iota(jnp.int32, (D_A, N_HEADS_A * tq), 0) // HEAD_DIM
    cols = lax.broadcasted_iota(jnp.int32, (D_A, N_HEADS_A * tq), 1) // tq
    qbd = jnp.where(rows == cols, jnp.concatenate([qt] * N_HEADS_A, axis=1), jnp.zeros((), BF16))
    st = jnp.dot(kw, qbd, preferred_element_type=F32)
    st = jnp.where(valid, st + bias_t, NEG_INF)
    m = jnp.max(st, axis=0, keepdims=True)
    p = jnp.exp2(st - m)
    l = jnp.sum(p, axis=0, keepdims=True)
    ot_all = jnp.dot(vtw, p.astype(BF16), preferred_element_type=F32)
    ot = jnp.concatenate([ot_all[h * HEAD_DIM:(h + 1) * HEAD_DIM, h * tq:(h + 1) * tq]
                          for h in range(N_HEADS_A)], axis=0)
    return ot, m, l


def _dilated_near_body(q_ref, k_ref, vt_ref, bias_ref, acc_ref, stat_ref, *, seq, tiles):
    tq = DIL_TQ
    win = tq + 2 * NEAR_HALO
    for t in range(tiles):
        q0 = (pl.program_id(1) * tiles + t) * tq
        start = pl.multiple_of(q0 + (DIL_HALO - NEAR_HALO), tq)
        kpos = q0 - NEAR_HALO + lax.broadcasted_iota(jnp.int32, (win, 1), 0)
        valid = (kpos >= 0) & (kpos < seq)
        ot, m, l = _band_attention_t(q_ref[0, t * tq:(t + 1) * tq, :], k_ref[0, pl.ds(start, win), :],
                                     vt_ref[0, :, pl.ds(start, win)], bias_ref[...], valid)
        acc_ref[0, t * tq:(t + 1) * tq, :] = ot.T
        stat_rows = ([m[:, h * tq:(h + 1) * tq] for h in range(N_HEADS_A)]
                     + [l[:, h * tq:(h + 1) * tq] for h in range(N_HEADS_A)]
                     + [jnp.zeros((STAT_LANES - 2 * N_HEADS_A, tq), F32)])
        stat_ref[0, t * tq:(t + 1) * tq, :] = jnp.concatenate(stat_rows, axis=0).T


def _dilated_far_body(q_ref, k_ref, v_ref, acc_ref, stat_ref, bias_ref, o_ref, *, cls_len):
    tq = DIL_TQ
    win = tq + 2 * FAR_HALO
    vt = v_ref[0].astype(F32).T.astype(BF16)
    for t in range(cls_len // tq):
        rs = slice(t * tq, (t + 1) * tq)
        kpos = t * tq - FAR_HALO + lax.broadcasted_iota(jnp.int32, (win, 1), 0)
        valid = (kpos >= 0) & (kpos < cls_len)
        ot2, m2, l2 = _band_attention_t(q_ref[0, rs, :], k_ref[0, t * tq:t * tq + win, :],
                                        vt[:, t * tq:t * tq + win], bias_ref[...], valid)
        ot1 = acc_ref[0, rs, :].T
        st1 = stat_ref[0, rs, :].T
        outs = []
        for h in range(N_HEADS_A):
            hs = slice(h * HEAD_DIM, (h + 1) * HEAD_DIM)
            cs = slice(h * tq, (h + 1) * tq)
            m1, l1 = st1[h:h + 1, :], st1[N_HEADS_A + h:N_HEADS_A + h + 1, :]
            m = jnp.maximum(m1, m2[:, cs])
            w1, w2 = jnp.exp2(m1 - m), jnp.exp2(m2[:, cs] - m)
            outs.append((w1 * ot1[hs, :] + w2 * ot2[hs, :]) / (w1 * l1 + w2 * l2[:, cs]))
        o_ref[0, rs, :] = jnp.concatenate(outs, axis=0).T


def _dilated(qa, ka_pad, va_pad, vat_pad):
    b, seq, _ = qa.shape
    dil = DIL_FAR[1]
    cls_len = seq // dil
    tq = DIL_TQ
    tiles = DIL_NEAR_TILES
    seq_pad = seq + 2 * DIL_HALO
    cls_pad = cls_len + 2 * FAR_HALO
    bias_t = lambda halo, patterns: jnp.asarray(np.tile(_band_bias(tq, halo, patterns).T, (1, N_HEADS_A)))
    blk = lambda width: pl.BlockSpec((1, tiles * tq, width), lambda bi, i: (bi, i, 0))
    acc, stats = pl.pallas_call(
        functools.partial(_dilated_near_body, seq=seq, tiles=tiles),
        grid=(b, seq // (tiles * tq)),
        in_specs=[blk(D_A), pl.BlockSpec((1, seq_pad, D_A), lambda bi, i: (bi, 0, 0)),
                  pl.BlockSpec((1, D_A, seq_pad), lambda bi, i: (bi, 0, 0)),
                  pl.BlockSpec((tq + 2 * NEAR_HALO, N_HEADS_A * tq), lambda bi, i: (0, 0))],
        out_specs=[blk(D_A), blk(STAT_LANES)],
        out_shape=[jax.ShapeDtypeStruct((b, seq, D_A), F32), jax.ShapeDtypeStruct((b, seq, STAT_LANES), F32)],
        name="dilated_near",
        compiler_params=_params("parallel", "arbitrary"),
    )(qa, ka_pad, vat_pad, bias_t(NEAR_HALO, DIL_NEAR))
    view = lambda t_, width: t_.reshape(b, t_.shape[1] // dil, dil * width)
    cls_blk = lambda width: pl.BlockSpec((1, cls_len, width), lambda bi, r: (bi, 0, r))
    out = pl.pallas_call(
        functools.partial(_dilated_far_body, cls_len=cls_len),
        grid=(b, dil),
        in_specs=[cls_blk(D_A), pl.BlockSpec((1, cls_pad, D_A), lambda bi, r: (bi, 0, r)),
                  pl.BlockSpec((1, cls_pad, D_A), lambda bi, r: (bi, 0, r)),
                  cls_blk(D_A), cls_blk(STAT_LANES),
                  pl.BlockSpec((tq + 2 * FAR_HALO, N_HEADS_A * tq), lambda bi, r: (0, 0))],
        out_specs=cls_blk(D_A),
        out_shape=jax.ShapeDtypeStruct((b, cls_len, dil * D_A), F32),
        name="dilated_far",
        compiler_params=_params("parallel", "parallel"),
    )(view(qa, D_A), view(ka_pad, D_A), view(va_pad, D_A), view(acc, D_A), view(stats, STAT_LANES),
      bias_t(FAR_HALO, ((2 * FAR_HALO, 1),)))
    return out.reshape(b, seq, D_A)


def _grid_attn_body(qt_ref, k_ref, vt_ref, o_ref, s0_ref, s1_ref, acc_ref, *, tq, tk, nk):
    cols = acc_ref.shape[2]
    for g in range(N_KV_B):
        qt = qt_ref[0, g]

        def scores(kb):
            k0 = pl.multiple_of(kb * tk, tk)
            return jnp.dot(k_ref[0, pl.ds(k0, tk), :], qt, preferred_element_type=F32)

        def consume(s_ref, kb, m_prev):
            s = s_ref[...]
            m_new = jnp.maximum(m_prev, jnp.max(s, axis=0, keepdims=True))
            alpha = jnp.exp2(m_prev - m_new)
            p = jnp.exp2(s - m_new).astype(BF16)
            pv = jnp.dot(vt_ref[kb, g], p, preferred_element_type=F32)
            acc_ref[g] = alpha * acc_ref[g] + pv
            return m_new

        s0_ref[...] = scores(0)
        acc_ref[g] = jnp.zeros(acc_ref.shape[1:], F32)

        def pair(j, m):
            kb = 2 * j
            s1_ref[...] = scores(kb + 1)
            m = consume(s0_ref, kb, m)
            s0_ref[...] = scores(jnp.minimum(kb + 2, nk - 1))
            return consume(s1_ref, kb + 1, m)

        lax.fori_loop(0, nk // 2, pair, jnp.full((1, cols), NEG_INF, F32))
    outs = [acc_ref[g, :HEAD_DIM, :] / acc_ref[g, HEAD_DIM:HEAD_DIM + 1, :] for g in range(N_KV_B)]
    ot = jnp.concatenate(outs, axis=0).T
    for g in range(N_KV_B):
        for r in range(GQA_GROUP):
            hq = g * GQA_GROUP + r
            o_ref[0, :, hq * HEAD_DIM:(hq + 1) * HEAD_DIM] = ot[r * tq:(r + 1) * tq, g * HEAD_DIM:(g + 1) * HEAD_DIM]


def _grid_attn(qt, kb, vt, tq, tk):
    b, seq, _ = kb.shape
    cols = GQA_GROUP * tq
    nq, nk = seq // tq, seq // tk
    return pl.pallas_call(
        functools.partial(_grid_attn_body, tq=tq, tk=tk, nk=nk),
        grid=(b, nq),
        in_specs=[pl.BlockSpec((1, N_KV_B, 2 * HEAD_DIM, cols), lambda bi, i: (bi * nq + i, 0, 0, 0)),
                  pl.BlockSpec((1, seq, D_KV_B), lambda bi, i: (bi, 0, 0)),
                  pl.BlockSpec((nk, N_KV_B, HEAD_DIM + BF16_ROWS, tk), lambda bi, i: (bi, 0, 0, 0))],
        out_specs=pl.BlockSpec((1, tq, D_B), lambda bi, i: (bi, i, 0)),
        out_shape=jax.ShapeDtypeStruct((b, seq, D_B), F32),
        scratch_shapes=[pltpu.VMEM((tk, cols), F32), pltpu.VMEM((tk, cols), F32),
                        pltpu.VMEM((N_KV_B, HEAD_DIM + BF16_ROWS, cols), F32)],
        name="grid_attn",
        compiler_params=_params("parallel", "arbitrary"),
    )(qt, kb, vt)


def _fft_tables(seq):
    n1, n2 = seq // FFT_N2, FFT_N2
    k1 = np.arange(n1)
    ang1 = 2.0 * np.pi * ((k1[:, None] * k1[None, :]) % n1) / n1
    f1 = np.concatenate([np.cos(ang1), -np.sin(ang1)], axis=0)
    k = k1[:, None, None] + n1 * np.arange(n2)[None, :, None]
    ang2 = 2.0 * np.pi * ((k * np.arange(n2)[None, None, :]) % seq) / seq
    g = np.concatenate([np.cos(ang2), np.sin(ang2)], axis=-1)
    c = np.arange(HEAD_DIM)
    ang3 = 2.0 * np.pi * ((c[:, None] * c[None, :]) % HEAD_DIM) / HEAD_DIM
    eye = np.eye(N_GROUPS_C)
    dc, ds = np.kron(eye, np.cos(ang3)), np.kron(eye, np.sin(ang3))
    d = np.concatenate([np.concatenate([dc, -ds], axis=1), np.concatenate([ds, dc], axis=1)], axis=0)
    return tuple(jnp.asarray(t, BF16) for t in (f1, g, d))


def _fft1_body(u_ref, f_ref, a_ref):
    a_ref[0] = jnp.dot(f_ref[...], u_ref[0].astype(BF16), preferred_element_type=F32).astype(BF16)


def _fft2_body(a_ref, g_ref, d_ref, o_ref, *, kb1):
    for j in range(kb1):
        ar, ai = a_ref[0, 0, j], a_ref[0, 1, j]
        p = (jnp.dot(ar, d_ref[:D_C, :], preferred_element_type=F32)
             + jnp.dot(ai, d_ref[D_C:, :], preferred_element_type=F32)).astype(BF16)
        gj = g_ref[j]
        o_ref[0, :, j * D_C:(j + 1) * D_C] = (
            jnp.dot(gj[:, :FFT_N2], p[:, :D_C], preferred_element_type=F32)
            + jnp.dot(gj[:, FFT_N2:], p[:, D_C:], preferred_element_type=F32))


def _fourier(uc, tables):
    b, seq, _ = uc.shape
    f1, g, d = tables
    n1, n2 = seq // FFT_N2, FFT_N2
    cols = n2 * D_C
    tc = 4096
    a = pl.pallas_call(
        _fft1_body,
        grid=(b, cols // tc),
        in_specs=[pl.BlockSpec((1, n1, tc), lambda bi, j: (bi, 0, j)),
                  pl.BlockSpec((2 * n1, n1), lambda bi, j: (0, 0))],
        out_specs=pl.BlockSpec((1, 2 * n1, tc), lambda bi, j: (bi, 0, j)),
        out_shape=jax.ShapeDtypeStruct((b, 2 * n1, cols), BF16),
        name="fft1",
        compiler_params=_params("parallel", "parallel"),
    )(uc.reshape(b, n1, cols), f1)
    kb1 = 8
    z = pl.pallas_call(
        functools.partial(_fft2_body, kb1=kb1),
        grid=(b, n1 // kb1),
        in_specs=[pl.BlockSpec((1, 2, kb1, n2, D_C), lambda bi, j: (bi, 0, j, 0, 0)),
                  pl.BlockSpec((kb1, n2, 2 * n2), lambda bi, j: (j, 0, 0)),
                  pl.BlockSpec((2 * D_C, 2 * D_C), lambda bi, j: (0, 0))],
        out_specs=pl.BlockSpec((1, n2, kb1 * D_C), lambda bi, j: (bi, 0, j)),
        out_shape=jax.ShapeDtypeStruct((b, n2, n1 * D_C), F32),
        name="fft2",
        compiler_params=_params("parallel", "parallel"),
    )(a.reshape(b, 2, n1, n2, D_C), g, d)
    return z.reshape(b * seq, D_C)


def _outproj_body(oa_ref, ob_ref, oc_ref, x_ref, gh_ref, w_ref, gpost_ref, gffn_ref, m_ref,
                  x1_ref, hn_ref):
    mix = None
    col = 0
    width = 2 * LANES
    for o_ref in (oa_ref, ob_ref, oc_ref):
        for c in range(o_ref.shape[-1] // width):
            o = o_ref[:, c * width:(c + 1) * width]
            y = o * lax.rsqrt(_head_mean_sq(o, m_ref) + RMS_EPS) * gh_ref[:, col:col + width]
            part = jnp.dot(y.astype(BF16), w_ref[col:col + width, :], preferred_element_type=F32)
            mix = part if mix is None else mix + part
            col += width
    x1 = x_ref[...] + _rms(mix, gpost_ref[...])
    x1_ref[...] = x1
    hn_ref[...] = _rms(x1, gffn_ref[...]).astype(BF16)


def _outproj(oa, ob, oc, x2d, g_heads, w_out, g_post, g_ffn, head_mask, tm):
    n = x2d.shape[0]
    row = lambda width: pl.BlockSpec((tm, width), lambda i: (i, 0))
    const = lambda shape: pl.BlockSpec(shape, lambda i: (0, 0))
    return pl.pallas_call(
        _outproj_body,
        grid=(n // tm,),
        in_specs=[row(D_A), row(D_B), row(D_C), row(D_MODEL), const((1, D_MIX)),
                  const((D_MIX, D_MODEL)), const((1, D_MODEL)), const((1, D_MODEL)),
                  const((2 * LANES, 2 * LANES))],
        out_specs=[row(D_MODEL), row(D_MODEL)],
        out_shape=[jax.ShapeDtypeStruct((n, D_MODEL), F32), jax.ShapeDtypeStruct((n, D_MODEL), BF16)],
        name="outproj",
        compiler_params=_params("parallel"),
    )(oa, ob, oc, x2d, g_heads, w_out, g_post, g_ffn, head_mask)


def _ffn_body(h_ref, hp_ref, hx_ref, x_ref, wg_ref, wu_ref, wd_ref, cw_ref, cb_ref, gpost_ref,
              o_ref, hext_ref, acc_ref, *, tm, blocks_per_seq):
    i = pl.program_id(0)
    c = pl.program_id(1)
    pad = BF16_ROWS
    rows = tm + 2 * pad

    @pl.when(c == 0)
    def _():
        first = (i % blocks_per_seq) == 0
        last = (i % blocks_per_seq) == blocks_per_seq - 1
        hext_ref[:pad, :] = jnp.where(first, jnp.zeros_like(hp_ref[0]), hp_ref[0])
        hext_ref[pad:pad + tm, :] = h_ref[...]
        hext_ref[pad + tm:, :] = jnp.where(last, jnp.zeros_like(hx_ref[0]), hx_ref[0])
        acc_ref[...] = jnp.zeros(acc_ref.shape, F32)

    g = jnp.dot(hext_ref[...], wg_ref[...], preferred_element_type=F32)
    cw = cw_ref[...]
    g = (pltpu.roll(g, 1, axis=0)[pad:pad + tm] * cw[0:1, :] + g[pad:pad + tm] * cw[1:2, :]
         + pltpu.roll(g, rows - 1, axis=0)[pad:pad + tm] * cw[2:3, :] + cb_ref[...])
    up = jnp.dot(h_ref[...], wu_ref[...], preferred_element_type=F32)
    inner = g * (GELU_C0 + GELU_C1 * (g * g))
    act = (g * (1.0 + jnp.tanh(inner)) * up).astype(BF16)
    acc_ref[...] += jnp.dot(act, wd_ref[...], preferred_element_type=F32)

    @pl.when(c == pl.num_programs(1) - 1)
    def _():
        o_ref[...] = x_ref[...] + _rms(acc_ref[...], gpost_ref[...])


def _ffn(hn, x1, seq, w_gate, w_up, w_down, conv_w, conv_b, g_post, tm, tf):
    n = x1.shape[0]
    nf = D_FF // tf
    blocks_per_seq = seq // tm
    halo_blocks = tm // BF16_ROWS
    n_halo = n // BF16_ROWS
    hn3 = hn.reshape(n_halo, BF16_ROWS, D_MODEL)
    row = lambda: pl.BlockSpec((tm, D_MODEL), lambda i, c: (i, 0))
    prev = pl.BlockSpec((1, BF16_ROWS, D_MODEL), lambda i, c: (jnp.maximum(i * halo_blocks - 1, 0), 0, 0))
    nxt = pl.BlockSpec((1, BF16_ROWS, D_MODEL),
                       lambda i, c: (jnp.minimum((i + 1) * halo_blocks, n_halo - 1), 0, 0))
    cols = lambda rows: pl.BlockSpec((rows, tf), lambda i, c: (0, c))
    return pl.pallas_call(
        functools.partial(_ffn_body, tm=tm, blocks_per_seq=blocks_per_seq),
        grid=(n // tm, nf),
        in_specs=[row(), prev, nxt, row(), cols(D_MODEL), cols(D_MODEL),
                  pl.BlockSpec((tf, D_MODEL), lambda i, c: (c, 0)), cols(3), cols(1),
                  pl.BlockSpec((1, D_MODEL), lambda i, c: (0, 0))],
        out_specs=row(),
        out_shape=jax.ShapeDtypeStruct((n, D_MODEL), F32),
        scratch_shapes=[pltpu.VMEM((tm + 2 * BF16_ROWS, D_MODEL), BF16), pltpu.VMEM((tm, D_MODEL), F32)],
        name="ffn",
        compiler_params=_params("parallel", "arbitrary"),
    )(hn, hn3, hn3, x1, w_gate, w_up, w_down, conv_w, conv_b, g_post)


def _rope_tables(pos, rot_dim, theta):
    half = rot_dim // 2
    inv = jnp.power(jnp.float32(theta), -jnp.arange(half, dtype=jnp.float32) / half)
    ang = pos.astype(jnp.float32)[:, None] * inv[None, :]
    return jnp.cos(ang), jnp.sin(ang)


def _position_tables(seq):
    rows = seq // GRID_W
    t = jnp.arange(seq, dtype=jnp.int32)
    row = jnp.broadcast_to(jnp.arange(rows, dtype=jnp.int32)[:, None], (rows, GRID_W)).reshape(-1)
    col = jnp.broadcast_to(jnp.arange(GRID_W, dtype=jnp.int32)[None, :], (rows, GRID_W)).reshape(-1)
    cos_t, sin_t = _rope_tables(t, ROT_DIM_PARTIAL, ROPE_THETA_PARTIAL)
    cos_r, sin_r = _rope_tables(row, HEAD_DIM // 2, ROPE_THETA_AXIAL)
    cos_c, sin_c = _rope_tables(col, HEAD_DIM // 2, ROPE_THETA_AXIAL)
    rest = HEAD_DIM - ROT_DIM_PARTIAL
    ca = jnp.concatenate([cos_t, cos_t, jnp.ones((seq, rest), F32)], axis=-1)
    sa = jnp.concatenate([-sin_t, sin_t, jnp.zeros((seq, rest), F32)], axis=-1)
    cb = jnp.concatenate([cos_r, cos_r, cos_c, cos_c], axis=-1)
    sb = jnp.concatenate([-sin_r, sin_r, -sin_c, sin_c], axis=-1)
    reps = LANES // HEAD_DIM
    return tuple(jnp.tile(t_, (1, reps)) for t_ in (ca, sa, cb, sb))


def _head_mask():
    lane = np.arange(2 * LANES)
    same = (lane[:, None] // HEAD_DIM) == (lane[None, :] // HEAD_DIM)
    return jnp.asarray(same / HEAD_DIM, BF16)


def _trunk(x, w):
    b, seq, _ = x.shape
    n = b * seq
    tm, tq, tk = TOKEN_TILE, GRID_TQ, TOKEN_TILE
    tabs = _position_tables(seq)
    fft_tabs = _fft_tables(seq)
    head_mask = _head_mask()
    x2d = x.reshape(n, D_MODEL)
    depth = w["w_in"].shape[0]
    for l in range(depth):
        tile4 = lambda g_: jnp.tile(g_[l][None, :], (1, 2 * LANES // HEAD_DIM))
        qa, ka_pad, va_pad, vat_pad, qt, kb, vt, uc = _inproj(
            x2d, b, seq, w["g_mix_pre"][l][None], w["w_in"][l], tile4(w["g_q"]), tile4(w["g_k"]),
            tabs, head_mask, tm, tq)
        oa = _dilated(qa.reshape(b, seq, D_A), ka_pad, va_pad, vat_pad).reshape(n, D_A)
        ob = _grid_attn(qt, kb.reshape(b, seq, D_KV_B), vt, tq, tk).reshape(n, D_B)
        oc = _fourier(uc.reshape(b, seq, D_C), fft_tabs)
        x1, hn = _outproj(oa, ob, oc, x2d, w["g_heads"][l][None], w["w_out"][l], w["g_mix_post"][l][None],
                          w["g_ffn_pre"][l][None], head_mask, tm)
        x2d = _ffn(hn, x1, seq, w["w_gate"][l], w["w_up"][l], w["w_down"][l], w["conv_w"][l],
                   w["conv_b"][l][None], w["g_ffn_post"][l][None], FFN_TILE, FFN_CHUNK)
    return x2d.reshape(b, seq, D_MODEL)


def _prep_weights(g_mix_pre, g_mix_post, w_in, g_q, g_k, g_heads, w_out, g_ffn_pre, g_ffn_post,
                  w_gate, w_up, conv_w, conv_b, w_down):
    return dict(
        g_mix_pre=g_mix_pre, g_mix_post=g_mix_post, g_q=g_q, g_k=g_k, g_heads=g_heads,
        g_ffn_pre=g_ffn_pre, g_ffn_post=g_ffn_post,
        w_in=w_in.astype(BF16), w_out=w_out.astype(BF16),
        w_gate=w_gate.astype(BF16), w_up=(0.5 * w_up).astype(BF16), w_down=w_down.astype(BF16),
        conv_w=conv_w, conv_b=conv_b)


def kernel(x_prompt, x_sample, g_mix_pre, g_mix_post, w_in, g_q, g_k, g_heads, w_out, g_ffn_pre, g_ffn_post,
           w_gate, w_up, conv_w, conv_b, w_down):
    w = _prep_weights(g_mix_pre, g_mix_post, w_in, g_q, g_k, g_heads, w_out, g_ffn_pre, g_ffn_post,
                      w_gate, w_up, conv_w, conv_b, w_down)
    return (_trunk(x_prompt, w), _trunk(x_sample, w))
```

```python
import functools

import numpy as np
import jax
import jax.numpy as jnp
from jax import lax
from jax.experimental import pallas as pl
from jax.experimental.pallas import tpu as pltpu

F32 = jnp.float32
BF16 = jnp.bfloat16

D_MODEL = 1024
HEAD_DIM = 64
N_HEADS_A = 4
DILATED_PATTERNS = ((128, 1), (512, 4), (2048, 16))
N_HEADS_B = 8
N_KV_B = 2
GQA_GROUP = N_HEADS_B // N_KV_B
N_GROUPS_C = 4
D_A = N_HEADS_A * HEAD_DIM
D_B = N_HEADS_B * HEAD_DIM
D_KV_B = N_KV_B * HEAD_DIM
D_C = N_GROUPS_C * HEAD_DIM
D_MIX = D_A + D_B + D_C
D_IN = 3 * D_A + D_B + 2 * D_KV_B + D_C
D_FF = 4 * D_MODEL
ROPE_THETA_PARTIAL = 500000.0
ROT_DIM_PARTIAL = HEAD_DIM // 4
ROPE_THETA_AXIAL = 10000.0
GRID_W = 64
RMS_EPS = 1e-6
NEG_INF = -1e30
Q_SCALE = HEAD_DIM ** -0.5 * float(np.log2(np.e))

LANES = 128
BF16_ROWS = 16
DIL_HALO = max(w // 2 for w, _ in DILATED_PATTERNS)
DIL_TQ = 128
DIL_FAR = DILATED_PATTERNS[-1]
DIL_CLASSES = DIL_FAR[1]
DIL_NEAR_TILES = 4
FFT_N2 = 128
VMEM_LIMIT = 56 * 1024 * 1024
TOKEN_TILE = 512
GRID_TQ = 512
FFN_TILE = 1024
FFN_CHUNK = 512
GELU_C0 = float(np.sqrt(2.0 / np.pi))
GELU_C1 = 0.044715 * GELU_C0


def _params(*sem):
    return pltpu.CompilerParams(dimension_semantics=sem, vmem_limit_bytes=VMEM_LIMIT)


def _rms(x, g):
    return x * lax.rsqrt(jnp.mean(x * x, axis=-1, keepdims=True) + RMS_EPS) * g


def _head_mean_sq(x, m_ref):
    return jnp.dot((x * x).astype(BF16), m_ref[...], preferred_element_type=F32)


def _rotate_half(x, half):
    n = x.shape[-1]
    lane = lax.broadcasted_iota(jnp.int32, (1, n), 1)
    fwd = pltpu.roll(x, n - half, axis=1)
    bwd = pltpu.roll(x, half, axis=1)
    return jnp.where((lane % (2 * half)) < half, fwd, bwd)


def _inproj_body(x_ref, g_ref, w_ref, gq_ref, gk_ref, ca_ref, sa_ref, cb_ref, sb_ref, m_ref,
                 ka_zero, kac_zero, vac_zero, vat_zero,
                 qa_ref, qac_ref, ka_ref, kac_ref, vac_ref, vat_ref, qt_ref, kb_ref, vt_ref, uc_ref,
                 rel_ref, *, tq):
    del ka_zero, kac_zero, vac_zero, vat_zero
    tm = x_ref.shape[0]
    h = _rms(x_ref[...], g_ref[...]).astype(BF16)

    def proj(lo, width):
        return jnp.dot(h, w_ref[:, lo:lo + width], preferred_element_type=F32)

    ca, sa, cb, sb = ca_ref[...], sa_ref[...], cb_ref[...], sb_ref[...]

    def rope_a(p):
        return p * ca + _rotate_half(p, ROT_DIM_PARTIAL // 2) * sa

    def norm_b(p, gain):
        return p * lax.rsqrt(_head_mean_sq(p, m_ref) + RMS_EPS) * gain

    def rope_b(y):
        return y * cb + _rotate_half(y, HEAD_DIM // 4) * sb

    def to_classes(val, out_ref):
        for c in range(D_A // LANES):
            rel_ref[c] = val[:, c * LANES:(c + 1) * LANES]
        for r in range(DIL_CLASSES):
            for c in range(D_A // LANES):
                lo = r * D_A + c * LANES
                out_ref[0, :, lo:lo + LANES] = rel_ref[c, pl.ds(r, tm // DIL_CLASSES, stride=DIL_CLASSES), :].astype(BF16)

    pq = proj(0, D_A)
    qa = jnp.concatenate([rope_a(pq[:, c * LANES:(c + 1) * LANES]) for c in range(D_A // LANES)], axis=1) * Q_SCALE
    qa_ref[...] = qa.astype(BF16)
    to_classes(qa, qac_ref)
    pk = proj(D_A, D_A)
    ka = jnp.concatenate([rope_a(pk[:, c * LANES:(c + 1) * LANES]) for c in range(D_A // LANES)], axis=1)
    ka_ref[0] = ka.astype(BF16)
    to_classes(ka, kac_ref)
    pv = proj(2 * D_A, D_A)
    to_classes(pv, vac_ref)
    vat_ref[0] = pv.T.astype(BF16)

    gq, gk = gq_ref[...], gk_ref[...]
    zeros = jnp.zeros((HEAD_DIM, GQA_GROUP * tq), BF16)
    for g in range(N_KV_B):
        y = norm_b(proj(3 * D_A + g * 2 * LANES, 2 * LANES), gq)
        for c in range(2):
            yt = (rope_b(y[:, c * LANES:(c + 1) * LANES]) * Q_SCALE).T.astype(BF16)
            for hh in range(LANES // HEAD_DIM):
                r = c * (LANES // HEAD_DIM) + hh
                for j in range(tm // tq):
                    qt_ref[j, g, g * HEAD_DIM:(g + 1) * HEAD_DIM, r * tq:(r + 1) * tq] = (
                        yt[hh * HEAD_DIM:(hh + 1) * HEAD_DIM, j * tq:(j + 1) * tq])
        for j in range(tm // tq):
            qt_ref[j, g, (1 - g) * HEAD_DIM:(2 - g) * HEAD_DIM, :] = zeros
    pkv = proj(3 * D_A + D_B, 2 * D_KV_B)
    kb_ref[...] = rope_b(norm_b(pkv, gk)[:, :D_KV_B]).astype(BF16)
    vbt = pkv[:, D_KV_B:].T.astype(BF16)
    for g in range(N_KV_B):
        vt_ref[0, g, :HEAD_DIM, :] = vbt[g * HEAD_DIM:(g + 1) * HEAD_DIM, :]
        vt_ref[0, g, HEAD_DIM:, :] = jnp.ones((BF16_ROWS, tm), BF16)
    uc_ref[...] = proj(3 * D_A + D_B + 2 * D_KV_B, D_C)


def _inproj(x2d, batch, seq, g_pre, w_in, gq, gk, tabs, head_mask, tm, tq):
    n = x2d.shape[0]
    bps = seq // tm
    halo_blocks = DIL_HALO // tm
    seq_pad = seq + 2 * DIL_HALO
    cols = GQA_GROUP * tq
    row = lambda width: pl.BlockSpec((tm, width), lambda i: (i, 0))
    const = lambda shape: pl.BlockSpec(shape, lambda i: (0, 0))
    tab = pl.BlockSpec((tm, LANES), lambda i: (i % bps, 0))
    pad_rows = pl.BlockSpec((1, tm, D_A), lambda i: (i // bps, i % bps + halo_blocks, 0))
    pad_cols = pl.BlockSpec((1, D_A, tm), lambda i: (i // bps, 0, i % bps + halo_blocks))
    cls_rows, cls_width = tm // DIL_CLASSES, DIL_CLASSES * D_A
    cls = pl.BlockSpec((1, cls_rows, cls_width), lambda i: (i // bps, i % bps, 0))
    cls_pad = pl.BlockSpec((1, cls_rows, cls_width), lambda i: (i // bps, i % bps + halo_blocks, 0))
    out_specs = [row(D_A), cls, pad_rows, cls_pad, cls_pad, pad_cols,
                 pl.BlockSpec((tm // tq, N_KV_B, 2 * HEAD_DIM, cols), lambda i: (i, 0, 0, 0)),
                 row(D_KV_B),
                 pl.BlockSpec((1, N_KV_B, HEAD_DIM + BF16_ROWS, tm), lambda i: (i, 0, 0, 0)),
                 row(D_C)]
    out_shape = [jax.ShapeDtypeStruct((n, D_A), BF16),
                 jax.ShapeDtypeStruct((batch, seq // DIL_CLASSES, cls_width), BF16),
                 jax.ShapeDtypeStruct((batch, seq_pad, D_A), BF16),
                 jax.ShapeDtypeStruct((batch, seq_pad // DIL_CLASSES, cls_width), BF16),
                 jax.ShapeDtypeStruct((batch, seq_pad // DIL_CLASSES, cls_width), BF16),
                 jax.ShapeDtypeStruct((batch, D_A, seq_pad), BF16),
                 jax.ShapeDtypeStruct((n // tq, N_KV_B, 2 * HEAD_DIM, cols), BF16),
                 jax.ShapeDtypeStruct((n, D_KV_B), BF16),
                 jax.ShapeDtypeStruct((n // tm, N_KV_B, HEAD_DIM + BF16_ROWS, tm), BF16),
                 jax.ShapeDtypeStruct((n, D_C), F32)]
    padded = (2, 3, 4, 5)
    zero_bufs = [jnp.zeros(out_shape[k].shape, BF16) for k in padded]
    n_in = 10
    return pl.pallas_call(
        functools.partial(_inproj_body, tq=tq),
        grid=(n // tm,),
        in_specs=[row(D_MODEL), const((1, D_MODEL)), const((D_MODEL, D_IN)),
                  const((1, 2 * LANES)), const((1, 2 * LANES)), tab, tab, tab, tab,
                  const((2 * LANES, 2 * LANES))] + [pl.BlockSpec(memory_space=pl.ANY)] * len(padded),
        out_specs=out_specs,
        out_shape=out_shape,
        input_output_aliases={n_in + k: o for k, o in enumerate(padded)},
        scratch_shapes=[pltpu.VMEM((D_A // LANES, tm, LANES), F32)],
        name="inproj",
        compiler_params=_params("parallel"),
    )(x2d, g_pre, w_in, gq, gk, *tabs, head_mask, *zero_bufs)


DIL_NEAR = DILATED_PATTERNS[:-1]
NEAR_HALO = max(w // 2 for w, _ in DIL_NEAR)
FAR_HALO = DIL_FAR[0] // (2 * DIL_FAR[1])
STAT_LANES = LANES


def _band_bias(tq, halo, patterns):
    delta = np.arange(tq + 2 * halo)[None, :] - halo - np.arange(tq)[:, None]
    mult = np.zeros(delta.shape, np.int64)
    for window, dil in patterns:
        mult += (np.abs(delta) <= (window // (2 * dil)) * dil) & (delta % dil == 0)
    return np.where(mult > 0, np.log2(np.maximum(mult, 1)), NEG_INF).astype(np.float32)


def _band_attention_t(q, kw, vtw, bias_t, valid):
    tq = q.shape[0]
    qt = q.astype(F32).T.astype(BF16)
    rows = lax.broadcasted_iota(jnp.int32, (D_A, N_HEADS_A * tq), 0) // HEAD_DIM
    cols = lax.broadcasted_iota(jnp.int32, (D_A, N_HEADS_A * tq), 1) // tq
    qbd = jnp.where(rows == cols, jnp.concatenate([qt] * N_HEADS_A, axis=1), jnp.zeros((), BF16))
    st = jnp.dot(kw, qbd, preferred_element_type=F32)
    st = jnp.where(valid, st + bias_t, NEG_INF)
    m = jnp.max(st, axis=0, keepdims=True)
    p = jnp.exp2(st - m)
    l = jnp.sum(p, axis=0, keepdims=True)
    ot_all = jnp.dot(vtw, p.astype(BF16), preferred_element_type=F32)
    ot = jnp.concatenate([ot_all[h * HEAD_DIM:(h + 1) * HEAD_DIM, h * tq:(h + 1) * tq]
                          for h in range(N_HEADS_A)], axis=0)
    return ot, m, l


def _dilated_near_body(q_ref, k_ref, vt_ref, bias_ref, acc_ref, stat_ref, acc_nat, stat_nat, *, seq, tiles):
    tq = DIL_TQ
    win = tq + 2 * NEAR_HALO
    for t in range(tiles):
        q0 = (pl.program_id(1) * tiles + t) * tq
        start = pl.multiple_of(q0 + (DIL_HALO - NEAR_HALO), tq)
        kpos = q0 - NEAR_HALO + lax.broadcasted_iota(jnp.int32, (win, 1), 0)
        valid = (kpos >= 0) & (kpos < seq)
        ot, m, l = _band_attention_t(q_ref[0, t * tq:(t + 1) * tq, :], k_ref[0, pl.ds(start, win), :],
                                     vt_ref[0, :, pl.ds(start, win)], bias_ref[...], valid)
        o = ot.T
        for c in range(D_A // LANES):
            acc_nat[c, t * tq:(t + 1) * tq, :] = o[:, c * LANES:(c + 1) * LANES]
        stat_rows = ([m[:, h * tq:(h + 1) * tq] for h in range(N_HEADS_A)]
                     + [l[:, h * tq:(h + 1) * tq] for h in range(N_HEADS_A)]
                     + [jnp.zeros((STAT_LANES - 2 * N_HEADS_A, tq), F32)])
        stat_nat[t * tq:(t + 1) * tq, :] = jnp.concatenate(stat_rows, axis=0).T
    cls_rows = tiles * tq // DIL_CLASSES
    for r in range(DIL_CLASSES):
        for c in range(D_A // LANES):
            lo = r * D_A + c * LANES
            acc_ref[0, :, lo:lo + LANES] = acc_nat[c, pl.ds(r, cls_rows, stride=DIL_CLASSES), :]
        stat_ref[0, :, r * STAT_LANES:(r + 1) * STAT_LANES] = stat_nat[pl.ds(r, cls_rows, stride=DIL_CLASSES), :]


def _dilated_far_body(q_ref, k_ref, v_ref, acc_ref, stat_ref, bias_ref, o_ref, *, cls_len):
    tq = DIL_TQ
    win = tq + 2 * FAR_HALO
    vt = v_ref[0].astype(F32).T.astype(BF16)
    for t in range(cls_len // tq):
        rs = slice(t * tq, (t + 1) * tq)
        kpos = t * tq - FAR_HALO + lax.broadcasted_iota(jnp.int32, (win, 1), 0)
        valid = (kpos >= 0) & (kpos < cls_len)
        ot2, m2, l2 = _band_attention_t(q_ref[0, rs, :], k_ref[0, t * tq:t * tq + win, :],
                                        vt[:, t * tq:t * tq + win], bias_ref[...], valid)
        ot1 = acc_ref[0, rs, :].T
        st1 = stat_ref[0, rs, :].T
        outs = []
        for h in range(N_HEADS_A):
            hs = slice(h * HEAD_DIM, (h + 1) * HEAD_DIM)
            cs = slice(h * tq, (h + 1) * tq)
            m1, l1 = st1[h:h + 1, :], st1[N_HEADS_A + h:N_HEADS_A + h + 1, :]
            m = jnp.maximum(m1, m2[:, cs])
            w1, w2 = jnp.exp2(m1 - m), jnp.exp2(m2[:, cs] - m)
            outs.append((w1 * ot1[hs, :] + w2 * ot2[hs, :]) / (w1 * l1 + w2 * l2[:, cs]))
        o_ref[0, rs, :] = jnp.concatenate(outs, axis=0).T


def _dilated(qa, qa_cls, ka_pad, ka_cls, va_cls, vat_pad):
    b, seq, _ = qa.shape
    dil = DIL_CLASSES
    cls_len = seq // dil
    tq = DIL_TQ
    tiles = DIL_NEAR_TILES
    seq_pad = seq + 2 * DIL_HALO
    cls_pad = cls_len + 2 * FAR_HALO
    step_cls = tiles * tq // dil
    bias_t = lambda halo, patterns: jnp.asarray(np.tile(_band_bias(tq, halo, patterns).T, (1, N_HEADS_A)))
    step_blk = lambda width: pl.BlockSpec((1, step_cls, dil * width), lambda bi, i: (bi, i, 0))
    acc, stats = pl.pallas_call(
        functools.partial(_dilated_near_body, seq=seq, tiles=tiles),
        grid=(b, seq // (tiles * tq)),
        in_specs=[pl.BlockSpec((1, tiles * tq, D_A), lambda bi, i: (bi, i, 0)),
                  pl.BlockSpec((1, seq_pad, D_A), lambda bi, i: (bi, 0, 0)),
                  pl.BlockSpec((1, D_A, seq_pad), lambda bi, i: (bi, 0, 0)),
                  pl.BlockSpec((tq + 2 * NEAR_HALO, N_HEADS_A * tq), lambda bi, i: (0, 0))],
        out_specs=[step_blk(D_A), step_blk(STAT_LANES)],
        out_shape=[jax.ShapeDtypeStruct((b, cls_len, dil * D_A), F32),
                   jax.ShapeDtypeStruct((b, cls_len, dil * STAT_LANES), F32)],
        scratch_shapes=[pltpu.VMEM((D_A // LANES, tiles * tq, LANES), F32),
                        pltpu.VMEM((tiles * tq, STAT_LANES), F32)],
        name="dilated_near",
        compiler_params=_params("parallel", "arbitrary"),
    )(qa, ka_pad, vat_pad, bias_t(NEAR_HALO, DIL_NEAR))
    cls_blk = lambda width: pl.BlockSpec((1, cls_len, width), lambda bi, r: (bi, 0, r))
    cls_kv = pl.BlockSpec((1, cls_pad, D_A), lambda bi, r: (bi, 0, r))
    return pl.pallas_call(
        functools.partial(_dilated_far_body, cls_len=cls_len),
        grid=(b, dil),
        in_specs=[cls_blk(D_A), cls_kv, cls_kv, cls_blk(D_A), cls_blk(STAT_LANES),
                  pl.BlockSpec((tq + 2 * FAR_HALO, N_HEADS_A * tq), lambda bi, r: (0, 0))],
        out_specs=cls_blk(D_A),
        out_shape=jax.ShapeDtypeStruct((b, cls_len, dil * D_A), F32),
        name="dilated_far",
        compiler_params=_params("parallel", "parallel"),
    )(qa_cls, ka_cls, va_cls, acc, stats, bias_t(FAR_HALO, ((2 * FAR_HALO, 1),)))


def _grid_attn_body(qt_ref, k_ref, vt_ref, o_ref, s0_ref, s1_ref, acc_ref, *, tq, tk, nk):
    cols = acc_ref.shape[2]
    for g in range(N_KV_B):
        qt = qt_ref[0, g]

        def scores(kb):
            k0 = pl.multiple_of(kb * tk, tk)
            return jnp.dot(k_ref[0, pl.ds(k0, tk), :], qt, preferred_element_type=F32)

        def consume(s_ref, kb, m_prev):
            s = s_ref[...]
            m_new = jnp.maximum(m_prev, jnp.max(s, axis=0, keepdims=True))
            alpha = jnp.exp2(m_prev - m_new)
            p = jnp.exp2(s - m_new).astype(BF16)
            pv = jnp.dot(vt_ref[kb, g], p, preferred_element_type=F32)
            acc_ref[g] = alpha * acc_ref[g] + pv
            return m_new

        s0_ref[...] = scores(0)
        acc_ref[g] = jnp.zeros(acc_ref.shape[1:], F32)

        def pair(j, m):
            kb = 2 * j
            s1_ref[...] = scores(kb + 1)
            m = consume(s0_ref, kb, m)
            s0_ref[...] = scores(jnp.minimum(kb + 2, nk - 1))
            return consume(s1_ref, kb + 1, m)

        lax.fori_loop(0, nk // 2, pair, jnp.full((1, cols), NEG_INF, F32))
    outs = [acc_ref[g, :HEAD_DIM, :] / acc_ref[g, HEAD_DIM:HEAD_DIM + 1, :] for g in range(N_KV_B)]
    ot = jnp.concatenate(outs, axis=0).T
    for g in range(N_KV_B):
        for r in range(GQA_GROUP):
            hq = g * GQA_GROUP + r
            o_ref[0, :, hq * HEAD_DIM:(hq + 1) * HEAD_DIM] = ot[r * tq:(r + 1) * tq, g * HEAD_DIM:(g + 1) * HEAD_DIM]


def _grid_attn(qt, kb, vt, tq, tk):
    b, seq, _ = kb.shape
    cols = GQA_GROUP * tq
    nq, nk = seq // tq, seq // tk
    return pl.pallas_call(
        functools.partial(_grid_attn_body, tq=tq, tk=tk, nk=nk),
        grid=(b, nq),
        in_specs=[pl.BlockSpec((1, N_KV_B, 2 * HEAD_DIM, cols), lambda bi, i: (bi * nq + i, 0, 0, 0)),
                  pl.BlockSpec((1, seq, D_KV_B), lambda bi, i: (bi, 0, 0)),
                  pl.BlockSpec((nk, N_KV_B, HEAD_DIM + BF16_ROWS, tk), lambda bi, i: (bi, 0, 0, 0))],
        out_specs=pl.BlockSpec((1, tq, D_B), lambda bi, i: (bi, i, 0)),
        out_shape=jax.ShapeDtypeStruct((b, seq, D_B), F32),
        scratch_shapes=[pltpu.VMEM((tk, cols), F32), pltpu.VMEM((tk, cols), F32),
                        pltpu.VMEM((N_KV_B, HEAD_DIM + BF16_ROWS, cols), F32)],
        name="grid_attn",
        compiler_params=_params("parallel", "arbitrary"),
    )(qt, kb, vt)


def _fft_tables(seq):
    n1, n2 = seq // FFT_N2, FFT_N2
    k1 = np.arange(n1)
    ang1 = 2.0 * np.pi * ((k1[:, None] * k1[None, :]) % n1) / n1
    f1 = np.concatenate([np.cos(ang1), -np.sin(ang1)], axis=0)
    k = k1[:, None, None] + n1 * np.arange(n2)[None, :, None]
    ang2 = 2.0 * np.pi * ((k * np.arange(n2)[None, None, :]) % seq) / seq
    g = np.concatenate([np.cos(ang2), np.sin(ang2)], axis=-1)
    c = np.arange(HEAD_DIM)
    ang3 = 2.0 * np.pi * ((c[:, None] * c[None, :]) % HEAD_DIM) / HEAD_DIM
    eye = np.eye(N_GROUPS_C)
    dc, ds = np.kron(eye, np.cos(ang3)), np.kron(eye, np.sin(ang3))
    d = np.concatenate([np.concatenate([dc, -ds], axis=1), np.concatenate([ds, dc], axis=1)], axis=0)
    return tuple(jnp.asarray(t, BF16) for t in (f1, g, d))


def _fft1_body(u_ref, f_ref, a_ref):
    a_ref[0] = jnp.dot(f_ref[...], u_ref[0].astype(BF16), preferred_element_type=F32).astype(BF16)


def _fft2_body(a_ref, g_ref, d_ref, o_ref, *, kb1):
    for j in range(kb1):
        ar, ai = a_ref[0, 0, j], a_ref[0, 1, j]
        p = (jnp.dot(ar, d_ref[:D_C, :], preferred_element_type=F32)
             + jnp.dot(ai, d_ref[D_C:, :], preferred_element_type=F32)).astype(BF16)
        gj = g_ref[j]
        o_ref[0, :, j * D_C:(j + 1) * D_C] = (
            jnp.dot(gj[:, :FFT_N2], p[:, :D_C], preferred_element_type=F32)
            + jnp.dot(gj[:, FFT_N2:], p[:, D_C:], preferred_element_type=F32))


def _fourier(uc, tables):
    b, seq, _ = uc.shape
    f1, g, d = tables
    n1, n2 = seq // FFT_N2, FFT_N2
    cols = n2 * D_C
    tc = 4096
    a = pl.pallas_call(
        _fft1_body,
        grid=(b, cols // tc),
        in_specs=[pl.BlockSpec((1, n1, tc), lambda bi, j: (bi, 0, j)),
                  pl.BlockSpec((2 * n1, n1), lambda bi, j: (0, 0))],
        out_specs=pl.BlockSpec((1, 2 * n1, tc), lambda bi, j: (bi, 0, j)),
        out_shape=jax.ShapeDtypeStruct((b, 2 * n1, cols), BF16),
        name="fft1",
        compiler_params=_params("parallel", "parallel"),
    )(uc.reshape(b, n1, cols), f1)
    kb1 = 8
    z = pl.pallas_call(
        functools.partial(_fft2_body, kb1=kb1),
        grid=(b, n1 // kb1),
        in_specs=[pl.BlockSpec((1, 2, kb1, n2, D_C), lambda bi, j: (bi, 0, j, 0, 0)),
                  pl.BlockSpec((kb1, n2, 2 * n2), lambda bi, j: (j, 0, 0)),
                  pl.BlockSpec((2 * D_C, 2 * D_C), lambda bi, j: (0, 0))],
        out_specs=pl.BlockSpec((1, n2, kb1 * D_C), lambda bi, j: (bi, 0, j)),
        out_shape=jax.ShapeDtypeStruct((b, n2, n1 * D_C), F32),
        name="fft2",
        compiler_params=_params("parallel", "parallel"),
    )(a.reshape(b, 2, n1, n2, D_C), g, d)
    return z.reshape(b * seq, D_C)


def _outproj_body(oac_ref, ob_ref, oc_ref, x_ref, gh_ref, w_ref, gpost_ref, gffn_ref, m_ref,
                  x1_ref, hn_ref, oa_ref):
    cls_rows = oa_ref.shape[1] // DIL_CLASSES
    for r in range(DIL_CLASSES):
        for c in range(D_A // LANES):
            lo = r * D_A + c * LANES
            oa_ref[c, pl.ds(r, cls_rows, stride=DIL_CLASSES), :] = oac_ref[0, :, lo:lo + LANES]
    oa = jnp.concatenate([oa_ref[c] for c in range(D_A // LANES)], axis=1)
    mix = None
    col = 0
    width = 2 * LANES
    for o_src in (oa, ob_ref, oc_ref):
        for c in range(o_src.shape[-1] // width):
            o = o_src[:, c * width:(c + 1) * width]
            y = o * lax.rsqrt(_head_mean_sq(o, m_ref) + RMS_EPS) * gh_ref[:, col:col + width]
            part = jnp.dot(y.astype(BF16), w_ref[col:col + width, :], preferred_element_type=F32)
            mix = part if mix is None else mix + part
            col += width
    x1 = x_ref[...] + _rms(mix, gpost_ref[...])
    x1_ref[...] = x1
    hn_ref[...] = _rms(x1, gffn_ref[...]).astype(BF16)


def _outproj(oa_cls, ob, oc, x2d, seq, g_heads, w_out, g_post, g_ffn, head_mask, tm):
    n = x2d.shape[0]
    bps = seq // tm
    row = lambda width: pl.BlockSpec((tm, width), lambda i: (i, 0))
    const = lambda shape: pl.BlockSpec(shape, lambda i: (0, 0))
    return pl.pallas_call(
        _outproj_body,
        grid=(n // tm,),
        in_specs=[pl.BlockSpec((1, tm // DIL_CLASSES, DIL_CLASSES * D_A), lambda i: (i // bps, i % bps, 0)),
                  row(D_B), row(D_C), row(D_MODEL), const((1, D_MIX)),
                  const((D_MIX, D_MODEL)), const((1, D_MODEL)), const((1, D_MODEL)),
                  const((2 * LANES, 2 * LANES))],
        out_specs=[row(D_MODEL), row(D_MODEL)],
        out_shape=[jax.ShapeDtypeStruct((n, D_MODEL), F32), jax.ShapeDtypeStruct((n, D_MODEL), BF16)],
        scratch_shapes=[pltpu.VMEM((D_A // LANES, tm, LANES), F32)],
        name="outproj",
        compiler_params=_params("parallel"),
    )(oa_cls, ob, oc, x2d, g_heads, w_out, g_post, g_ffn, head_mask)


def _ffn_body(h_ref, hp_ref, hx_ref, x_ref, wg_ref, wu_ref, wd_ref, cw_ref, cb_ref, gpost_ref,
              o_ref, hext_ref, acc_ref, *, tm, blocks_per_seq):
    i = pl.program_id(0)
    c = pl.program_id(1)
    pad = BF16_ROWS
    rows = tm + 2 * pad

    @pl.when(c == 0)
    def _():
        first = (i % blocks_per_seq) == 0
        last = (i % blocks_per_seq) == blocks_per_seq - 1
        hext_ref[:pad, :] = jnp.where(first, jnp.zeros_like(hp_ref[0]), hp_ref[0])
        hext_ref[pad:pad + tm, :] = h_ref[...]
        hext_ref[pad + tm:, :] = jnp.where(last, jnp.zeros_like(hx_ref[0]), hx_ref[0])
        acc_ref[...] = jnp.zeros(acc_ref.shape, F32)

    g = jnp.dot(hext_ref[...], wg_ref[...], preferred_element_type=F32)
    cw = cw_ref[...]
    g = (pltpu.roll(g, 1, axis=0)[pad:pad + tm] * cw[0:1, :] + g[pad:pad + tm] * cw[1:2, :]
         + pltpu.roll(g, rows - 1, axis=0)[pad:pad + tm] * cw[2:3, :] + cb_ref[...])
    up = jnp.dot(h_ref[...], wu_ref[...], preferred_element_type=F32)
    inner = g * (GELU_C0 + GELU_C1 * (g * g))
    act = (g * (1.0 + jnp.tanh(inner)) * up).astype(BF16)
    acc_ref[...] += jnp.dot(act, wd_ref[...], preferred_element_type=F32)

    @pl.when(c == pl.num_programs(1) - 1)
    def _():
        o_ref[...] = x_ref[...] + _rms(acc_ref[...], gpost_ref[...])


def _ffn(hn, x1, seq, w_gate, w_up, w_down, conv_w, conv_b, g_post, tm, tf):
    n = x1.shape[0]
    nf = D_FF // tf
    blocks_per_seq = seq // tm
    halo_blocks = tm // BF16_ROWS
    n_halo = n // BF16_ROWS
    hn3 = hn.reshape(n_halo, BF16_ROWS, D_MODEL)
    row = lambda: pl.BlockSpec((tm, D_MODEL), lambda i, c: (i, 0))
    prev = pl.BlockSpec((1, BF16_ROWS, D_MODEL), lambda i, c: (jnp.maximum(i * halo_blocks - 1, 0), 0, 0))
    nxt = pl.BlockSpec((1, BF16_ROWS, D_MODEL),
                       lambda i, c: (jnp.minimum((i + 1) * halo_blocks, n_halo - 1), 0, 0))
    cols = lambda rows: pl.BlockSpec((rows, tf), lambda i, c: (0, c))
    return pl.pallas_call(
        functools.partial(_ffn_body, tm=tm, blocks_per_seq=blocks_per_seq),
        grid=(n // tm, nf),
        in_specs=[row(), prev, nxt, row(), cols(D_MODEL), cols(D_MODEL),
                  pl.BlockSpec((tf, D_MODEL), lambda i, c: (c, 0)), cols(3), cols(1),
                  pl.BlockSpec((1, D_MODEL), lambda i, c: (0, 0))],
        out_specs=row(),
        out_shape=jax.ShapeDtypeStruct((n, D_MODEL), F32),
        scratch_shapes=[pltpu.VMEM((tm + 2 * BF16_ROWS, D_MODEL), BF16), pltpu.VMEM((tm, D_MODEL), F32)],
        name="ffn",
        compiler_params=_params("parallel", "arbitrary"),
    )(hn, hn3, hn3, x1, w_gate, w_up, w_down, conv_w, conv_b, g_post)


def _rope_tables(pos, rot_dim, theta):
    half = rot_dim // 2
    inv = jnp.power(jnp.float32(theta), -jnp.arange(half, dtype=jnp.float32) / half)
    ang = pos.astype(jnp.float32)[:, None] * inv[None, :]
    return jnp.cos(ang), jnp.sin(ang)


def _position_tables(seq):
    rows = seq // GRID_W
    t = jnp.arange(seq, dtype=jnp.int32)
    row = jnp.broadcast_to(jnp.arange(rows, dtype=jnp.int32)[:, None], (rows, GRID_W)).reshape(-1)
    col = jnp.broadcast_to(jnp.arange(GRID_W, dtype=jnp.int32)[None, :], (rows, GRID_W)).reshape(-1)
    cos_t, sin_t = _rope_tables(t, ROT_DIM_PARTIAL, ROPE_THETA_PARTIAL)
    cos_r, sin_r = _rope_tables(row, HEAD_DIM // 2, ROPE_THETA_AXIAL)
    cos_c, sin_c = _rope_tables(col, HEAD_DIM // 2, ROPE_THETA_AXIAL)
    rest = HEAD_DIM - ROT_DIM_PARTIAL
    ca = jnp.concatenate([cos_t, cos_t, jnp.ones((seq, rest), F32)], axis=-1)
    sa = jnp.concatenate([-sin_t, sin_t, jnp.zeros((seq, rest), F32)], axis=-1)
    cb = jnp.concatenate([cos_r, cos_r, cos_c, cos_c], axis=-1)
    sb = jnp.concatenate([-sin_r, sin_r, -sin_c, sin_c], axis=-1)
    reps = LANES // HEAD_DIM
    return tuple(jnp.tile(t_, (1, reps)) for t_ in (ca, sa, cb, sb))


def _head_mask():
    lane = np.arange(2 * LANES)
    same = (lane[:, None] // HEAD_DIM) == (lane[None, :] // HEAD_DIM)
    return jnp.asarray(same / HEAD_DIM, BF16)


def _trunk(x, w):
    b, seq, _ = x.shape
    n = b * seq
    tm, tq, tk = TOKEN_TILE, GRID_TQ, TOKEN_TILE
    tabs = _position_tables(seq)
    fft_tabs = _fft_tables(seq)
    head_mask = _head_mask()
    x2d = x.reshape(n, D_MODEL)
    depth = w["w_in"].shape[0]
    for l in range(depth):
        tile4 = lambda g_: jnp.tile(g_[l][None, :], (1, 2 * LANES // HEAD_DIM))
        qa, qa_cls, ka_pad, ka_cls, va_cls, vat_pad, qt, kb, vt, uc = _inproj(
            x2d, b, seq, w["g_mix_pre"][l][None], w["w_in"][l], tile4(w["g_q"]), tile4(w["g_k"]),
            tabs, head_mask, tm, tq)
        oa_cls = _dilated(qa.reshape(b, seq, D_A), qa_cls, ka_pad, ka_cls, va_cls, vat_pad)
        ob = _grid_attn(qt, kb.reshape(b, seq, D_KV_B), vt, tq, tk).reshape(n, D_B)
        oc = _fourier(uc.reshape(b, seq, D_C), fft_tabs)
        x1, hn = _outproj(oa_cls, ob, oc, x2d, seq, w["g_heads"][l][None], w["w_out"][l], w["g_mix_post"][l][None],
                          w["g_ffn_pre"][l][None], head_mask, tm)
        x2d = _ffn(hn, x1, seq, w["w_gate"][l], w["w_up"][l], w["w_down"][l], w["conv_w"][l],
                   w["conv_b"][l][None], w["g_ffn_post"][l][None], FFN_TILE, FFN_CHUNK)
    return x2d.reshape(b, seq, D_MODEL)


def _prep_weights(g_mix_pre, g_mix_post, w_in, g_q, g_k, g_heads, w_out, g_ffn_pre, g_ffn_post,
                  w_gate, w_up, conv_w, conv_b, w_down):
    return dict(
        g_mix_pre=g_mix_pre, g_mix_post=g_mix_post, g_q=g_q, g_k=g_k, g_heads=g_heads,
        g_ffn_pre=g_ffn_pre, g_ffn_post=g_ffn_post,
        w_in=w_in.astype(BF16), w_out=w_out.astype(BF16),
        w_gate=w_gate.astype(BF16), w_up=(0.5 * w_up).astype(BF16), w_down=w_down.astype(BF16),
        conv_w=conv_w, conv_b=conv_b)


def kernel(x_prompt, x_sample, g_mix_pre, g_mix_post, w_in, g_q, g_k, g_heads, w_out, g_ffn_pre, g_ffn_post,
           w_gate, w_up, conv_w, conv_b, w_down):
    w = _prep_weights(g_mix_pre, g_mix_post, w_in, g_q, g_k, g_heads, w_out, g_ffn_pre, g_ffn_post,
                      w_gate, w_up, conv_w, conv_b, w_down)
    return (_trunk(x_prompt, w), _trunk(x_sample, w))
```

```python
import functools

import numpy as np
import jax
import jax.numpy as jnp
from jax import lax
from jax.experimental import pallas as pl
from jax.experimental.pallas import tpu as pltpu

F32 = jnp.float32
BF16 = jnp.bfloat16

D_MODEL = 1024
HEAD_DIM = 64
N_HEADS_A = 4
DILATED_PATTERNS = ((128, 1), (512, 4), (2048, 16))
N_HEADS_B = 8
N_KV_B = 2
GQA_GROUP = N_HEADS_B // N_KV_B
N_GROUPS_C = 4
D_A = N_HEADS_A * HEAD_DIM
D_B = N_HEADS_B * HEAD_DIM
D_KV_B = N_KV_B * HEAD_DIM
D_C = N_GROUPS_C * HEAD_DIM
D_MIX = D_A + D_B + D_C
D_IN = 3 * D_A + D_B + 2 * D_KV_B + D_C
D_FF = 4 * D_MODEL
ROPE_THETA_PARTIAL = 500000.0
ROT_DIM_PARTIAL = HEAD_DIM // 4
ROPE_THETA_AXIAL = 10000.0
GRID_W = 64
RMS_EPS = 1e-6
NEG_INF = -1e30
Q_SCALE = HEAD_DIM ** -0.5 * float(np.log2(np.e))

LANES = 128
BF16_ROWS = 16
DIL_HALO = max(w // 2 for w, _ in DILATED_PATTERNS)
DIL_TQ = 128
DIL_FAR = DILATED_PATTERNS[-1]
DIL_CLASSES = DIL_FAR[1]
DIL_NEAR_TILES = 4
FFT_N2 = 128
VMEM_LIMIT = 56 * 1024 * 1024
TOKEN_TILE = 512
GRID_TQ = 512
FFN_TILE = 1024
FFN_CHUNK = 1024
GELU_C0 = float(np.sqrt(2.0 / np.pi))
GELU_C1 = 0.044715 * GELU_C0


def _params(*sem):
    return pltpu.CompilerParams(dimension_semantics=sem, vmem_limit_bytes=VMEM_LIMIT)


def _rms(x, g):
    return x * lax.rsqrt(jnp.mean(x * x, axis=-1, keepdims=True) + RMS_EPS) * g


def _head_mean_sq(x, m_ref):
    return jnp.dot((x * x).astype(BF16), m_ref[...], preferred_element_type=F32)


def _rotate_half(x, half):
    n = x.shape[-1]
    lane = lax.broadcasted_iota(jnp.int32, (1, n), 1)
    fwd = pltpu.roll(x, n - half, axis=1)
    bwd = pltpu.roll(x, half, axis=1)
    return jnp.where((lane % (2 * half)) < half, fwd, bwd)


def _inproj_body(x_ref, g_ref, w_ref, gq_ref, gk_ref, ca_ref, sa_ref, cb_ref, sb_ref, m_ref,
                 ka_zero, kac_zero, vac_zero, vat_zero,
                 qa_ref, qac_ref, ka_ref, kac_ref, vac_ref, vat_ref, qt_ref, kb_ref, vt_ref, uc_ref,
                 rel_ref, *, tq):
    del ka_zero, kac_zero, vac_zero, vat_zero
    tm = x_ref.shape[0]
    h = _rms(x_ref[...], g_ref[...]).astype(BF16)

    def proj(lo, width):
        return jnp.dot(h, w_ref[:, lo:lo + width], preferred_element_type=F32)

    ca, sa, cb, sb = ca_ref[...], sa_ref[...], cb_ref[...], sb_ref[...]

    def rope_a(p):
        return p * ca + _rotate_half(p, ROT_DIM_PARTIAL // 2) * sa

    def norm_b(p, gain):
        return p * lax.rsqrt(_head_mean_sq(p, m_ref) + RMS_EPS) * gain

    def rope_b(y):
        return y * cb + _rotate_half(y, HEAD_DIM // 4) * sb

    def to_classes(val, out_ref):
        for c in range(D_A // LANES):
            rel_ref[c] = val[:, c * LANES:(c + 1) * LANES]
        for r in range(DIL_CLASSES):
            for c in range(D_A // LANES):
                lo = r * D_A + c * LANES
                out_ref[0, :, lo:lo + LANES] = rel_ref[c, pl.ds(r, tm // DIL_CLASSES, stride=DIL_CLASSES), :].astype(BF16)

    pq = proj(0, D_A)
    qa = jnp.concatenate([rope_a(pq[:, c * LANES:(c + 1) * LANES]) for c in range(D_A // LANES)], axis=1) * Q_SCALE
    qa_ref[...] = qa.astype(BF16)
    to_classes(qa, qac_ref)
    pk = proj(D_A, D_A)
    ka = jnp.concatenate([rope_a(pk[:, c * LANES:(c + 1) * LANES]) for c in range(D_A // LANES)], axis=1)
    ka_ref[0] = ka.astype(BF16)
    to_classes(ka, kac_ref)
    pv = proj(2 * D_A, D_A)
    to_classes(pv, vac_ref)
    vat_ref[0] = pv.T.astype(BF16)

    gq, gk = gq_ref[...], gk_ref[...]
    zeros = jnp.zeros((HEAD_DIM, GQA_GROUP * tq), BF16)
    for g in range(N_KV_B):
        y = norm_b(proj(3 * D_A + g * 2 * LANES, 2 * LANES), gq)
        for c in range(2):
            yt = (rope_b(y[:, c * LANES:(c + 1) * LANES]) * Q_SCALE).T.astype(BF16)
            for hh in range(LANES // HEAD_DIM):
                r = c * (LANES // HEAD_DIM) + hh
                for j in range(tm // tq):
                    qt_ref[j, g, g * HEAD_DIM:(g + 1) * HEAD_DIM, r * tq:(r + 1) * tq] = (
                        yt[hh * HEAD_DIM:(hh + 1) * HEAD_DIM, j * tq:(j + 1) * tq])
        for j in range(tm // tq):
            qt_ref[j, g, (1 - g) * HEAD_DIM:(2 - g) * HEAD_DIM, :] = zeros
    pkv = proj(3 * D_A + D_B, 2 * D_KV_B)
    kb_ref[...] = rope_b(norm_b(pkv, gk)[:, :D_KV_B]).astype(BF16)
    vbt = pkv[:, D_KV_B:].T.astype(BF16)
    for g in range(N_KV_B):
        vt_ref[0, g] = vbt[g * HEAD_DIM:(g + 1) * HEAD_DIM, :]
    uc_ref[...] = proj(3 * D_A + D_B + 2 * D_KV_B, D_C)


def _inproj(x2d, batch, seq, g_pre, w_in, gq, gk, tabs, head_mask, tm, tq):
    n = x2d.shape[0]
    bps = seq // tm
    halo_blocks = DIL_HALO // tm
    seq_pad = seq + 2 * DIL_HALO
    cols = GQA_GROUP * tq
    row = lambda width: pl.BlockSpec((tm, width), lambda i: (i, 0))
    const = lambda shape: pl.BlockSpec(shape, lambda i: (0, 0))
    tab = pl.BlockSpec((tm, LANES), lambda i: (i % bps, 0))
    pad_rows = pl.BlockSpec((1, tm, D_A), lambda i: (i // bps, i % bps + halo_blocks, 0))
    pad_cols = pl.BlockSpec((1, D_A, tm), lambda i: (i // bps, 0, i % bps + halo_blocks))
    cls_rows, cls_width = tm // DIL_CLASSES, DIL_CLASSES * D_A
    cls = pl.BlockSpec((1, cls_rows, cls_width), lambda i: (i // bps, i % bps, 0))
    cls_pad = pl.BlockSpec((1, cls_rows, cls_width), lambda i: (i // bps, i % bps + halo_blocks, 0))
    out_specs = [row(D_A), cls, pad_rows, cls_pad, cls_pad, pad_cols,
                 pl.BlockSpec((tm // tq, N_KV_B, 2 * HEAD_DIM, cols), lambda i: (i, 0, 0, 0)),
                 row(D_KV_B),
                 pl.BlockSpec((1, N_KV_B, HEAD_DIM, tm), lambda i: (i, 0, 0, 0)),
                 row(D_C)]
    out_shape = [jax.ShapeDtypeStruct((n, D_A), BF16),
                 jax.ShapeDtypeStruct((batch, seq // DIL_CLASSES, cls_width), BF16),
                 jax.ShapeDtypeStruct((batch, seq_pad, D_A), BF16),
                 jax.ShapeDtypeStruct((batch, seq_pad // DIL_CLASSES, cls_width), BF16),
                 jax.ShapeDtypeStruct((batch, seq_pad // DIL_CLASSES, cls_width), BF16),
                 jax.ShapeDtypeStruct((batch, D_A, seq_pad), BF16),
                 jax.ShapeDtypeStruct((n // tq, N_KV_B, 2 * HEAD_DIM, cols), BF16),
                 jax.ShapeDtypeStruct((n, D_KV_B), BF16),
                 jax.ShapeDtypeStruct((n // tm, N_KV_B, HEAD_DIM, tm), BF16),
                 jax.ShapeDtypeStruct((n, D_C), F32)]
    padded = (2, 3, 4, 5)
    zero_bufs = [jnp.zeros(out_shape[k].shape, BF16) for k in padded]
    n_in = 10
    return pl.pallas_call(
        functools.partial(_inproj_body, tq=tq),
        grid=(n // tm,),
        in_specs=[row(D_MODEL), const((1, D_MODEL)), const((D_MODEL, D_IN)),
                  const((1, 2 * LANES)), const((1, 2 * LANES)), tab, tab, tab, tab,
                  const((2 * LANES, 2 * LANES))] + [pl.BlockSpec(memory_space=pl.ANY)] * len(padded),
        out_specs=out_specs,
        out_shape=out_shape,
        input_output_aliases={n_in + k: o for k, o in enumerate(padded)},
        scratch_shapes=[pltpu.VMEM((D_A // LANES, tm, LANES), F32)],
        name="inproj",
        compiler_params=_params("parallel"),
    )(x2d, g_pre, w_in, gq, gk, *tabs, head_mask, *zero_bufs)


DIL_NEAR = DILATED_PATTERNS[:-1]
NEAR_HALO = max(w // 2 for w, _ in DIL_NEAR)
FAR_HALO = DIL_FAR[0] // (2 * DIL_FAR[1])
STAT_LANES = LANES


def _band_bias(tq, halo, patterns):
    delta = np.arange(tq + 2 * halo)[None, :] - halo - np.arange(tq)[:, None]
    mult = np.zeros(delta.shape, np.int64)
    for window, dil in patterns:
        mult += (np.abs(delta) <= (window // (2 * dil)) * dil) & (delta % dil == 0)
    return np.where(mult > 0, np.log2(np.maximum(mult, 1)), NEG_INF).astype(np.float32)


def _band_attention_t(q, kw, vtw, bias_t, valid):
    tq = q.shape[0]
    qt = q.astype(F32).T.astype(BF16)
    rows = lax.broadcasted_iota(jnp.int32, (D_A, N_HEADS_A * tq), 0) // HEAD_DIM
    cols = lax.broadcasted_iota(jnp.int32, (D_A, N_HEADS_A * tq), 1) // tq
    qbd = jnp.where(rows == cols, jnp.concatenate([qt] * N_HEADS_A, axis=1), jnp.zeros((), BF16))
    st = jnp.dot(kw, qbd, preferred_element_type=F32)
    st = jnp.where(valid, st + bias_t, NEG_INF)
    m = jnp.max(st, axis=0, keepdims=True)
    p = jnp.exp2(st - m)
    l = jnp.sum(p, axis=0, keepdims=True)
    ot_all = jnp.dot(vtw, p.astype(BF16), preferred_element_type=F32)
    ot = jnp.concatenate([ot_all[h * HEAD_DIM:(h + 1) * HEAD_DIM, h * tq:(h + 1) * tq]
                          for h in range(N_HEADS_A)], axis=0)
    return ot, m, l


def _dilated_near_body(q_ref, k_ref, vt_ref, bias_ref, acc_ref, stat_ref, acc_nat, stat_nat, *, seq, tiles):
    tq = DIL_TQ
    win = tq + 2 * NEAR_HALO
    for t in range(tiles):
        q0 = (pl.program_id(1) * tiles + t) * tq
        start = pl.multiple_of(q0 + (DIL_HALO - NEAR_HALO), tq)
        kpos = q0 - NEAR_HALO + lax.broadcasted_iota(jnp.int32, (win, 1), 0)
        valid = (kpos >= 0) & (kpos < seq)
        ot, m, l = _band_attention_t(q_ref[0, t * tq:(t + 1) * tq, :], k_ref[0, pl.ds(start, win), :],
                                     vt_ref[0, :, pl.ds(start, win)], bias_ref[...], valid)
        o = ot.T
        for c in range(D_A // LANES):
            acc_nat[c, t * tq:(t + 1) * tq, :] = o[:, c * LANES:(c + 1) * LANES]
        stat_rows = ([m[:, h * tq:(h + 1) * tq] for h in range(N_HEADS_A)]
                     + [l[:, h * tq:(h + 1) * tq] for h in range(N_HEADS_A)]
                     + [jnp.zeros((STAT_LANES - 2 * N_HEADS_A, tq), F32)])
        stat_nat[t * tq:(t + 1) * tq, :] = jnp.concatenate(stat_rows, axis=0).T
    cls_rows = tiles * tq // DIL_CLASSES
    for r in range(DIL_CLASSES):
        for c in range(D_A // LANES):
            lo = r * D_A + c * LANES
            acc_ref[0, :, lo:lo + LANES] = acc_nat[c, pl.ds(r, cls_rows, stride=DIL_CLASSES), :]
        stat_ref[0, :, r * STAT_LANES:(r + 1) * STAT_LANES] = stat_nat[pl.ds(r, cls_rows, stride=DIL_CLASSES), :]


def _dilated_far_body(q_ref, k_ref, v_ref, acc_ref, stat_ref, bias_ref, o_ref, *, cls_len):
    tq = DIL_TQ
    win = tq + 2 * FAR_HALO
    vt = v_ref[0].astype(F32).T.astype(BF16)
    for t in range(cls_len // tq):
        rs = slice(t * tq, (t + 1) * tq)
        kpos = t * tq - FAR_HALO + lax.broadcasted_iota(jnp.int32, (win, 1), 0)
        valid = (kpos >= 0) & (kpos < cls_len)
        ot2, m2, l2 = _band_attention_t(q_ref[0, rs, :], k_ref[0, t * tq:t * tq + win, :],
                                        vt[:, t * tq:t * tq + win], bias_ref[...], valid)
        ot1 = acc_ref[0, rs, :].T
        st1 = stat_ref[0, rs, :].T
        outs = []
        for h in range(N_HEADS_A):
            hs = slice(h * HEAD_DIM, (h + 1) * HEAD_DIM)
            cs = slice(h * tq, (h + 1) * tq)
            m1, l1 = st1[h:h + 1, :], st1[N_HEADS_A + h:N_HEADS_A + h + 1, :]
            m = jnp.maximum(m1, m2[:, cs])
            w1, w2 = jnp.exp2(m1 - m), jnp.exp2(m2[:, cs] - m)
            outs.append((w1 * ot1[hs, :] + w2 * ot2[hs, :]) / (w1 * l1 + w2 * l2[:, cs]))
        o_ref[0, rs, :] = jnp.concatenate(outs, axis=0).T


def _dilated(qa, qa_cls, ka_pad, ka_cls, va_cls, vat_pad):
    b, seq, _ = qa.shape
    dil = DIL_CLASSES
    cls_len = seq // dil
    tq = DIL_TQ
    tiles = DIL_NEAR_TILES
    seq_pad = seq + 2 * DIL_HALO
    cls_pad = cls_len + 2 * FAR_HALO
    step_cls = tiles * tq // dil
    bias_t = lambda halo, patterns: jnp.asarray(np.tile(_band_bias(tq, halo, patterns).T, (1, N_HEADS_A)))
    step_blk = lambda width: pl.BlockSpec((1, step_cls, dil * width), lambda bi, i: (bi, i, 0))
    acc, stats = pl.pallas_call(
        functools.partial(_dilated_near_body, seq=seq, tiles=tiles),
        grid=(b, seq // (tiles * tq)),
        in_specs=[pl.BlockSpec((1, tiles * tq, D_A), lambda bi, i: (bi, i, 0)),
                  pl.BlockSpec((1, seq_pad, D_A), lambda bi, i: (bi, 0, 0)),
                  pl.BlockSpec((1, D_A, seq_pad), lambda bi, i: (bi, 0, 0)),
                  pl.BlockSpec((tq + 2 * NEAR_HALO, N_HEADS_A * tq), lambda bi, i: (0, 0))],
        out_specs=[step_blk(D_A), step_blk(STAT_LANES)],
        out_shape=[jax.ShapeDtypeStruct((b, cls_len, dil * D_A), F32),
                   jax.ShapeDtypeStruct((b, cls_len, dil * STAT_LANES), F32)],
        scratch_shapes=[pltpu.VMEM((D_A // LANES, tiles * tq, LANES), F32),
                        pltpu.VMEM((tiles * tq, STAT_LANES), F32)],
        name="dilated_near",
        compiler_params=_params("parallel", "arbitrary"),
    )(qa, ka_pad, vat_pad, bias_t(NEAR_HALO, DIL_NEAR))
    cls_blk = lambda width: pl.BlockSpec((1, cls_len, width), lambda bi, r: (bi, 0, r))
    cls_kv = pl.BlockSpec((1, cls_pad, D_A), lambda bi, r: (bi, 0, r))
    return pl.pallas_call(
        functools.partial(_dilated_far_body, cls_len=cls_len),
        grid=(b, dil),
        in_specs=[cls_blk(D_A), cls_kv, cls_kv, cls_blk(D_A), cls_blk(STAT_LANES),
                  pl.BlockSpec((tq + 2 * FAR_HALO, N_HEADS_A * tq), lambda bi, r: (0, 0))],
        out_specs=cls_blk(D_A),
        out_shape=jax.ShapeDtypeStruct((b, cls_len, dil * D_A), F32),
        name="dilated_far",
        compiler_params=_params("parallel", "parallel"),
    )(qa_cls, ka_cls, va_cls, acc, stats, bias_t(FAR_HALO, ((2 * FAR_HALO, 1),)))


def _grid_attn_body(qt_ref, k_ref, vt_ref, o_ref, s0_ref, s1_ref, acc_ref, *, tq, tk, nk):
    cols = acc_ref.shape[2]
    for g in range(N_KV_B):
        qt = qt_ref[0, g]

        def scores(kb):
            k0 = pl.multiple_of(kb * tk, tk)
            return jnp.dot(k_ref[0, pl.ds(k0, tk), :], qt, preferred_element_type=F32)

        def consume(s_ref, kb, carry):
            m_prev, l_prev = carry
            s = s_ref[...]
            m_new = jnp.maximum(m_prev, jnp.max(s, axis=0, keepdims=True))
            alpha = jnp.exp2(m_prev - m_new)
            p = jnp.exp2(s - m_new)
            l_new = alpha * l_prev + jnp.sum(p, axis=0, keepdims=True)
            pv = jnp.dot(vt_ref[kb, g], p.astype(BF16), preferred_element_type=F32)
            acc_ref[g] = alpha * acc_ref[g] + pv
            return m_new, l_new

        s0_ref[...] = scores(0)
        acc_ref[g] = jnp.zeros(acc_ref.shape[1:], F32)

        def pair(j, m):
            kb = 2 * j
            s1_ref[...] = scores(kb + 1)
            m = consume(s0_ref, kb, m)
            s0_ref[...] = scores(jnp.minimum(kb + 2, nk - 1))
            return consume(s1_ref, kb + 1, m)

        _, l_fin = lax.fori_loop(0, nk // 2, pair, (jnp.full((1, cols), NEG_INF, F32), jnp.zeros((1, cols), F32)))
        acc_ref[g] = acc_ref[g] / l_fin
    ot = acc_ref[...].reshape(N_KV_B * HEAD_DIM, cols).T
    for g in range(N_KV_B):
        for r in range(GQA_GROUP):
            hq = g * GQA_GROUP + r
            o_ref[0, :, hq * HEAD_DIM:(hq + 1) * HEAD_DIM] = ot[r * tq:(r + 1) * tq, g * HEAD_DIM:(g + 1) * HEAD_DIM]


def _grid_attn(qt, kb, vt, tq, tk):
    b, seq, _ = kb.shape
    cols = GQA_GROUP * tq
    nq, nk = seq // tq, seq // tk
    return pl.pallas_call(
        functools.partial(_grid_attn_body, tq=tq, tk=tk, nk=nk),
        grid=(b, nq),
        in_specs=[pl.BlockSpec((1, N_KV_B, 2 * HEAD_DIM, cols), lambda bi, i: (bi * nq + i, 0, 0, 0)),
                  pl.BlockSpec((1, seq, D_KV_B), lambda bi, i: (bi, 0, 0)),
                  pl.BlockSpec((nk, N_KV_B, HEAD_DIM, tk), lambda bi, i: (bi, 0, 0, 0))],
        out_specs=pl.BlockSpec((1, tq, D_B), lambda bi, i: (bi, i, 0)),
        out_shape=jax.ShapeDtypeStruct((b, seq, D_B), F32),
        scratch_shapes=[pltpu.VMEM((tk, cols), F32), pltpu.VMEM((tk, cols), F32),
                        pltpu.VMEM((N_KV_B, HEAD_DIM, cols), F32)],
        name="grid_attn",
        compiler_params=_params("parallel", "arbitrary"),
    )(qt, kb, vt)


def _fft_tables(seq):
    n1, n2 = seq // FFT_N2, FFT_N2
    k1 = np.arange(n1)
    ang1 = 2.0 * np.pi * ((k1[:, None] * k1[None, :]) % n1) / n1
    f1 = np.concatenate([np.cos(ang1), -np.sin(ang1)], axis=0)
    k = k1[:, None, None] + n1 * np.arange(n2)[None, :, None]
    ang2 = 2.0 * np.pi * ((k * np.arange(n2)[None, None, :]) % seq) / seq
    g = np.concatenate([np.cos(ang2), np.sin(ang2)], axis=-1)
    c = np.arange(HEAD_DIM)
    ang3 = 2.0 * np.pi * ((c[:, None] * c[None, :]) % HEAD_DIM) / HEAD_DIM
    eye = np.eye(N_GROUPS_C)
    dc, ds = np.kron(eye, np.cos(ang3)), np.kron(eye, np.sin(ang3))
    d = np.concatenate([np.concatenate([dc, -ds], axis=1), np.concatenate([ds, dc], axis=1)], axis=0)
    return tuple(jnp.asarray(t, BF16) for t in (f1, g, d))


def _fft1_body(u_ref, f_ref, a_ref):
    a_ref[0] = jnp.dot(f_ref[...], u_ref[0].astype(BF16), preferred_element_type=F32).astype(BF16)


def _fft2_body(a_ref, g_ref, d_ref, o_ref, *, kb1):
    for j in range(kb1):
        ar, ai = a_ref[0, 0, j], a_ref[0, 1, j]
        p = (jnp.dot(ar, d_ref[:D_C, :], preferred_element_type=F32)
             + jnp.dot(ai, d_ref[D_C:, :], preferred_element_type=F32)).astype(BF16)
        gj = g_ref[j]
        o_ref[0, :, j * D_C:(j + 1) * D_C] = (
            jnp.dot(gj[:, :FFT_N2], p[:, :D_C], preferred_element_type=F32)
            + jnp.dot(gj[:, FFT_N2:], p[:, D_C:], preferred_element_type=F32))


def _fourier(uc, tables):
    b, seq, _ = uc.shape
    f1, g, d = tables
    n1, n2 = seq // FFT_N2, FFT_N2
    cols = n2 * D_C
    tc = 4096
    a = pl.pallas_call(
        _fft1_body,
        grid=(b, cols // tc),
        in_specs=[pl.BlockSpec((1, n1, tc), lambda bi, j: (bi, 0, j)),
                  pl.BlockSpec((2 * n1, n1), lambda bi, j: (0, 0))],
        out_specs=pl.BlockSpec((1, 2 * n1, tc), lambda bi, j: (bi, 0, j)),
        out_shape=jax.ShapeDtypeStruct((b, 2 * n1, cols), BF16),
        name="fft1",
        compiler_params=_params("parallel", "parallel"),
    )(uc.reshape(b, n1, cols), f1)
    kb1 = 8
    z = pl.pallas_call(
        functools.partial(_fft2_body, kb1=kb1),
        grid=(b, n1 // kb1),
        in_specs=[pl.BlockSpec((1, 2, kb1, n2, D_C), lambda bi, j: (bi, 0, j, 0, 0)),
                  pl.BlockSpec((kb1, n2, 2 * n2), lambda bi, j: (j, 0, 0)),
                  pl.BlockSpec((2 * D_C, 2 * D_C), lambda bi, j: (0, 0))],
        out_specs=pl.BlockSpec((1, n2, kb1 * D_C), lambda bi, j: (bi, 0, j)),
        out_shape=jax.ShapeDtypeStruct((b, n2, n1 * D_C), F32),
        name="fft2",
        compiler_params=_params("parallel", "parallel"),
    )(a.reshape(b, 2, n1, n2, D_C), g, d)
    return z.reshape(b * seq, D_C)


def _outproj_body(oac_ref, ob_ref, oc_ref, x_ref, gh_ref, w_ref, gpost_ref, gffn_ref, m_ref,
                  x1_ref, hn_ref, oa_ref):
    cls_rows = oa_ref.shape[1] // DIL_CLASSES
    for r in range(DIL_CLASSES):
        for c in range(D_A // LANES):
            lo = r * D_A + c * LANES
            oa_ref[c, pl.ds(r, cls_rows, stride=DIL_CLASSES), :] = oac_ref[0, :, lo:lo + LANES]
    oa = jnp.concatenate([oa_ref[c] for c in range(D_A // LANES)], axis=1)
    mix = None
    col = 0
    width = 2 * LANES
    for o_src in (oa, ob_ref, oc_ref):
        for c in range(o_src.shape[-1] // width):
            o = o_src[:, c * width:(c + 1) * width]
            y = o * lax.rsqrt(_head_mean_sq(o, m_ref) + RMS_EPS) * gh_ref[:, col:col + width]
            part = jnp.dot(y.astype(BF16), w_ref[col:col + width, :], preferred_element_type=F32)
            mix = part if mix is None else mix + part
            col += width
    x1 = x_ref[...] + _rms(mix, gpost_ref[...])
    x1_ref[...] = x1
    hn_ref[...] = _rms(x1, gffn_ref[...]).astype(BF16)


def _outproj(oa_cls, ob, oc, x2d, seq, g_heads, w_out, g_post, g_ffn, head_mask, tm):
    n = x2d.shape[0]
    bps = seq // tm
    row = lambda width: pl.BlockSpec((tm, width), lambda i: (i, 0))
    const = lambda shape: pl.BlockSpec(shape, lambda i: (0, 0))
    return pl.pallas_call(
        _outproj_body,
        grid=(n // tm,),
        in_specs=[pl.BlockSpec((1, tm // DIL_CLASSES, DIL_CLASSES * D_A), lambda i: (i // bps, i % bps, 0)),
                  row(D_B), row(D_C), row(D_MODEL), const((1, D_MIX)),
                  const((D_MIX, D_MODEL)), const((1, D_MODEL)), const((1, D_MODEL)),
                  const((2 * LANES, 2 * LANES))],
        out_specs=[row(D_MODEL), row(D_MODEL)],
        out_shape=[jax.ShapeDtypeStruct((n, D_MODEL), F32), jax.ShapeDtypeStruct((n, D_MODEL), BF16)],
        scratch_shapes=[pltpu.VMEM((D_A // LANES, tm, LANES), F32)],
        name="outproj",
        compiler_params=_params("parallel"),
    )(oa_cls, ob, oc, x2d, g_heads, w_out, g_post, g_ffn, head_mask)


def _ffn_body(h_ref, hp_ref, hx_ref, x_ref, wg_ref, wu_ref, wd_ref, cw_ref, cb_ref, gpost_ref,
              o_ref, hext_ref, acc_ref, *, tm, blocks_per_seq):
    i = pl.program_id(0)
    c = pl.program_id(1)
    pad = BF16_ROWS
    rows = tm + 2 * pad

    @pl.when(c == 0)
    def _():
        first = (i % blocks_per_seq) == 0
        last = (i % blocks_per_seq) == blocks_per_seq - 1
        hext_ref[:pad, :] = jnp.where(first, jnp.zeros_like(hp_ref[0]), hp_ref[0])
        hext_ref[pad:pad + tm, :] = h_ref[...]
        hext_ref[pad + tm:, :] = jnp.where(last, jnp.zeros_like(hx_ref[0]), hx_ref[0])
        acc_ref[...] = jnp.zeros(acc_ref.shape, F32)

    g = jnp.dot(hext_ref[...], wg_ref[...], preferred_element_type=F32)
    cw = cw_ref[...]
    g = (pltpu.roll(g, 1, axis=0)[pad:pad + tm] * cw[0:1, :] + g[pad:pad + tm] * cw[1:2, :]
         + pltpu.roll(g, rows - 1, axis=0)[pad:pad + tm] * cw[2:3, :] + cb_ref[...])
    up = jnp.dot(h_ref[...], wu_ref[...], preferred_element_type=F32)
    inner = g * (GELU_C0 + GELU_C1 * (g * g))
    act = (g * (1.0 + jnp.tanh(inner)) * up).astype(BF16)
    acc_ref[...] += jnp.dot(act, wd_ref[...], preferred_element_type=F32)

    @pl.when(c == pl.num_programs(1) - 1)
    def _():
        o_ref[...] = x_ref[...] + _rms(acc_ref[...], gpost_ref[...])


def _ffn(hn, x1, seq, w_gate, w_up, w_down, conv_w, conv_b, g_post, tm, tf):
    n = x1.shape[0]
    nf = D_FF // tf
    blocks_per_seq = seq // tm
    halo_blocks = tm // BF16_ROWS
    n_halo = n // BF16_ROWS
    hn3 = hn.reshape(n_halo, BF16_ROWS, D_MODEL)
    row = lambda: pl.BlockSpec((tm, D_MODEL), lambda i, c: (i, 0))
    prev = pl.BlockSpec((1, BF16_ROWS, D_MODEL), lambda i, c: (jnp.maximum(i * halo_blocks - 1, 0), 0, 0))
    nxt = pl.BlockSpec((1, BF16_ROWS, D_MODEL),
                       lambda i, c: (jnp.minimum((i + 1) * halo_blocks, n_halo - 1), 0, 0))
    cols = lambda rows: pl.BlockSpec((rows, tf), lambda i, c: (0, c))
    return pl.pallas_call(
        functools.partial(_ffn_body, tm=tm, blocks_per_seq=blocks_per_seq),
        grid=(n // tm, nf),
        in_specs=[row(), prev, nxt, row(), cols(D_MODEL), cols(D_MODEL),
                  pl.BlockSpec((tf, D_MODEL), lambda i, c: (c, 0)), cols(3), cols(1),
                  pl.BlockSpec((1, D_MODEL), lambda i, c: (0, 0))],
        out_specs=row(),
        out_shape=jax.ShapeDtypeStruct((n, D_MODEL), F32),
        scratch_shapes=[pltpu.VMEM((tm + 2 * BF16_ROWS, D_MODEL), BF16), pltpu.VMEM((tm, D_MODEL), F32)],
        name="ffn",
        compiler_params=_params("parallel", "arbitrary"),
    )(hn, hn3, hn3, x1, w_gate, w_up, w_down, conv_w, conv_b, g_post)


def _rope_tables(pos, rot_dim, theta):
    half = rot_dim // 2
    inv = jnp.power(jnp.float32(theta), -jnp.arange(half, dtype=jnp.float32) / half)
    ang = pos.astype(jnp.float32)[:, None] * inv[None, :]
    return jnp.cos(ang), jnp.sin(ang)


def _position_tables(seq):
    rows = seq // GRID_W
    t = jnp.arange(seq, dtype=jnp.int32)
    row = jnp.broadcast_to(jnp.arange(rows, dtype=jnp.int32)[:, None], (rows, GRID_W)).reshape(-1)
    col = jnp.broadcast_to(jnp.arange(GRID_W, dtype=jnp.int32)[None, :], (rows, GRID_W)).reshape(-1)
    cos_t, sin_t = _rope_tables(t, ROT_DIM_PARTIAL, ROPE_THETA_PARTIAL)
    cos_r, sin_r = _rope_tables(row, HEAD_DIM // 2, ROPE_THETA_AXIAL)
    cos_c, sin_c = _rope_tables(col, HEAD_DIM // 2, ROPE_THETA_AXIAL)
    rest = HEAD_DIM - ROT_DIM_PARTIAL
    ca = jnp.concatenate([cos_t, cos_t, jnp.ones((seq, rest), F32)], axis=-1)
    sa = jnp.concatenate([-sin_t, sin_t, jnp.zeros((seq, rest), F32)], axis=-1)
    cb = jnp.concatenate([cos_r, cos_r, cos_c, cos_c], axis=-1)
    sb = jnp.concatenate([-sin_r, sin_r, -sin_c, sin_c], axis=-1)
    reps = LANES // HEAD_DIM
    return tuple(jnp.tile(t_, (1, reps)) for t_ in (ca, sa, cb, sb))


def _head_mask():
    lane = np.arange(2 * LANES)
    same = (lane[:, None] // HEAD_DIM) == (lane[None, :] // HEAD_DIM)
    return jnp.asarray(same / HEAD_DIM, BF16)


def _trunk(x, w):
    b, seq, _ = x.shape
    n = b * seq
    tm, tq, tk = TOKEN_TILE, GRID_TQ, TOKEN_TILE
    tabs = _position_tables(seq)
    fft_tabs = _fft_tables(seq)
    head_mask = _head_mask()
    x2d = x.reshape(n, D_MODEL)
    depth = w["w_in"].shape[0]
    for l in range(depth):
        tile4 = lambda g_: jnp.tile(g_[l][None, :], (1, 2 * LANES // HEAD_DIM))
        qa, qa_cls, ka_pad, ka_cls, va_cls, vat_pad, qt, kb, vt, uc = _inproj(
            x2d, b, seq, w["g_mix_pre"][l][None], w["w_in"][l], tile4(w["g_q"]), tile4(w["g_k"]),
            tabs, head_mask, tm, tq)
        oa_cls = _dilated(qa.reshape(b, seq, D_A), qa_cls, ka_pad, ka_cls, va_cls, vat_pad)
        ob = _grid_attn(qt, kb.reshape(b, seq, D_KV_B), vt, tq, tk).reshape(n, D_B)
        oc = _fourier(uc.reshape(b, seq, D_C), fft_tabs)
        x1, hn = _outproj(oa_cls, ob, oc, x2d, seq, w["g_heads"][l][None], w["w_out"][l], w["g_mix_post"][l][None],
                          w["g_ffn_pre"][l][None], head_mask, tm)
        x2d = _ffn(hn, x1, seq, w["w_gate"][l], w["w_up"][l], w["w_down"][l], w["conv_w"][l],
                   w["conv_b"][l][None], w["g_ffn_post"][l][None], FFN_TILE, FFN_CHUNK)
    return x2d.reshape(b, seq, D_MODEL)


def _prep_weights(g_mix_pre, g_mix_post, w_in, g_q, g_k, g_heads, w_out, g_ffn_pre, g_ffn_post,
                  w_gate, w_up, conv_w, conv_b, w_down):
    return dict(
        g_mix_pre=g_mix_pre, g_mix_post=g_mix_post, g_q=g_q, g_k=g_k, g_heads=g_heads,
        g_ffn_pre=g_ffn_pre, g_ffn_post=g_ffn_post,
        w_in=w_in.astype(BF16), w_out=w_out.astype(BF16),
        w_gate=w_gate.astype(BF16), w_up=(0.5 * w_up).astype(BF16), w_down=w_down.astype(BF16),
        conv_w=conv_w, conv_b=conv_b)


def kernel(x_prompt, x_sample, g_mix_pre, g_mix_post, w_in, g_q, g_k, g_heads, w_out, g_ffn_pre, g_ffn_post,
           w_gate, w_up, conv_w, conv_b, w_down):
    w = _prep_weights(g_mix_pre, g_mix_post, w_in, g_q, g_k, g_heads, w_out, g_ffn_pre, g_ffn_post,
                      w_gate, w_up, conv_w, conv_b, w_down)
    return (_trunk(x_prompt, w), _trunk(x_sample, w))
```

```python
import functools

import numpy as np
import jax
import jax.numpy as jnp
from jax import lax
from jax.experimental import pallas as pl
from jax.experimental.pallas import tpu as pltpu

F32 = jnp.float32
BF16 = jnp.bfloat16

D_MODEL = 1024
HEAD_DIM = 64
N_HEADS_A = 4
DILATED_PATTERNS = ((128, 1), (512, 4), (2048, 16))
N_HEADS_B = 8
N_KV_B = 2
GQA_GROUP = N_HEADS_B // N_KV_B
N_GROUPS_C = 4
D_A = N_HEADS_A * HEAD_DIM
D_B = N_HEADS_B * HEAD_DIM
D_KV_B = N_KV_B * HEAD_DIM
D_C = N_GROUPS_C * HEAD_DIM
D_MIX = D_A + D_B + D_C
D_IN = 3 * D_A + D_B + 2 * D_KV_B + D_C
D_FF = 4 * D_MODEL
ROPE_THETA_PARTIAL = 500000.0
ROT_DIM_PARTIAL = HEAD_DIM // 4
ROPE_THETA_AXIAL = 10000.0
GRID_W = 64
RMS_EPS = 1e-6
NEG_INF = -1e30
Q_SCALE = HEAD_DIM ** -0.5 * float(np.log2(np.e))

LANES = 128
BF16_ROWS = 16
DIL_HALO = max(w // 2 for w, _ in DILATED_PATTERNS)
DIL_TQ = 128
DIL_FAR = DILATED_PATTERNS[-1]
DIL_CLASSES = DIL_FAR[1]
DIL_NEAR_TILES = 4
FFT_N2 = 128
VMEM_LIMIT = 56 * 1024 * 1024
TOKEN_TILE = 512
GRID_TQ = 512
GRID_TK = 512
FFN_TILE = 1024
FFN_CHUNK = 1024
GELU_C0 = float(np.sqrt(2.0 / np.pi))
GELU_C1 = 0.044715 * GELU_C0


def _params(*sem):
    return pltpu.CompilerParams(dimension_semantics=sem, vmem_limit_bytes=VMEM_LIMIT)


def _rms(x, g):
    return x * lax.rsqrt(jnp.mean(x * x, axis=-1, keepdims=True) + RMS_EPS) * g


def _head_mean_sq(x, m_ref):
    return jnp.dot((x * x).astype(BF16), m_ref[...], preferred_element_type=F32)


def _rotate_half(x, half):
    n = x.shape[-1]
    lane = lax.broadcasted_iota(jnp.int32, (1, n), 1)
    fwd = pltpu.roll(x, n - half, axis=1)
    bwd = pltpu.roll(x, half, axis=1)
    return jnp.where((lane % (2 * half)) < half, fwd, bwd)


def _inproj_body(x_ref, g_ref, w_ref, gq_ref, gk_ref, ca_ref, sa_ref, cb_ref, sb_ref, m_ref,
                 ka_zero, kac_zero, vac_zero, vat_zero,
                 qa_ref, qac_ref, ka_ref, kac_ref, vac_ref, vat_ref, qt_ref, kb_ref, vt_ref, uc_ref,
                 rel_ref, *, tq):
    del ka_zero, kac_zero, vac_zero, vat_zero
    tm = x_ref.shape[0]
    h = _rms(x_ref[...], g_ref[...]).astype(BF16)

    def proj(lo, width):
        return jnp.dot(h, w_ref[:, lo:lo + width], preferred_element_type=F32)

    ca, sa, cb, sb = ca_ref[...], sa_ref[...], cb_ref[...], sb_ref[...]

    def rope_a(p):
        return p * ca + _rotate_half(p, ROT_DIM_PARTIAL // 2) * sa

    def norm_b(p, gain):
        return p * lax.rsqrt(_head_mean_sq(p, m_ref) + RMS_EPS) * gain

    def rope_b(y):
        return y * cb + _rotate_half(y, HEAD_DIM // 4) * sb

    def to_classes(val, out_ref):
        for c in range(D_A // LANES):
            rel_ref[c] = val[:, c * LANES:(c + 1) * LANES]
        for r in range(DIL_CLASSES):
            for c in range(D_A // LANES):
                lo = r * D_A + c * LANES
                out_ref[0, :, lo:lo + LANES] = rel_ref[c, pl.ds(r, tm // DIL_CLASSES, stride=DIL_CLASSES), :].astype(BF16)

    pq = proj(0, D_A)
    qa = jnp.concatenate([rope_a(pq[:, c * LANES:(c + 1) * LANES]) for c in range(D_A // LANES)], axis=1) * Q_SCALE
    qa_ref[...] = qa.astype(BF16)
    to_classes(qa, qac_ref)
    pk = proj(D_A, D_A)
    ka = jnp.concatenate([rope_a(pk[:, c * LANES:(c + 1) * LANES]) for c in range(D_A // LANES)], axis=1)
    ka_ref[0] = ka.astype(BF16)
    to_classes(ka, kac_ref)
    pv = proj(2 * D_A, D_A)
    to_classes(pv, vac_ref)
    vat_ref[0] = pv.T.astype(BF16)

    gq, gk = gq_ref[...], gk_ref[...]
    zeros = jnp.zeros((HEAD_DIM, GQA_GROUP * tq), BF16)
    for g in range(N_KV_B):
        y = norm_b(proj(3 * D_A + g * 2 * LANES, 2 * LANES), gq)
        for c in range(2):
            yt = (rope_b(y[:, c * LANES:(c + 1) * LANES]) * Q_SCALE).T.astype(BF16)
            for hh in range(LANES // HEAD_DIM):
                r = c * (LANES // HEAD_DIM) + hh
                for j in range(tm // tq):
                    qt_ref[j, g, g * HEAD_DIM:(g + 1) * HEAD_DIM, r * tq:(r + 1) * tq] = (
                        yt[hh * HEAD_DIM:(hh + 1) * HEAD_DIM, j * tq:(j + 1) * tq])
        for j in range(tm // tq):
            qt_ref[j, g, (1 - g) * HEAD_DIM:(2 - g) * HEAD_DIM, :] = zeros
    pkv = proj(3 * D_A + D_B, 2 * D_KV_B)
    kb_ref[...] = rope_b(norm_b(pkv, gk)[:, :D_KV_B]).astype(BF16)
    vbt = pkv[:, D_KV_B:].T.astype(BF16)
    for g in range(N_KV_B):
        vt_ref[0, g] = vbt[g * HEAD_DIM:(g + 1) * HEAD_DIM, :]
    uc_ref[...] = proj(3 * D_A + D_B + 2 * D_KV_B, D_C)


def _inproj(x2d, batch, seq, g_pre, w_in, gq, gk, tabs, head_mask, tm, tq):
    n = x2d.shape[0]
    bps = seq // tm
    halo_blocks = DIL_HALO // tm
    seq_pad = seq + 2 * DIL_HALO
    cols = GQA_GROUP * tq
    row = lambda width: pl.BlockSpec((tm, width), lambda i: (i, 0))
    const = lambda shape: pl.BlockSpec(shape, lambda i: (0, 0))
    tab = pl.BlockSpec((tm, LANES), lambda i: (i % bps, 0))
    pad_rows = pl.BlockSpec((1, tm, D_A), lambda i: (i // bps, i % bps + halo_blocks, 0))
    pad_cols = pl.BlockSpec((1, D_A, tm), lambda i: (i // bps, 0, i % bps + halo_blocks))
    cls_rows, cls_width = tm // DIL_CLASSES, DIL_CLASSES * D_A
    cls = pl.BlockSpec((1, cls_rows, cls_width), lambda i: (i // bps, i % bps, 0))
    cls_pad = pl.BlockSpec((1, cls_rows, cls_width), lambda i: (i // bps, i % bps + halo_blocks, 0))
    out_specs = [row(D_A), cls, pad_rows, cls_pad, cls_pad, pad_cols,
                 pl.BlockSpec((tm // tq, N_KV_B, 2 * HEAD_DIM, cols), lambda i: (i, 0, 0, 0)),
                 row(D_KV_B),
                 pl.BlockSpec((1, N_KV_B, HEAD_DIM, tm), lambda i: (i, 0, 0, 0)),
                 row(D_C)]
    out_shape = [jax.ShapeDtypeStruct((n, D_A), BF16),
                 jax.ShapeDtypeStruct((batch, seq // DIL_CLASSES, cls_width), BF16),
                 jax.ShapeDtypeStruct((batch, seq_pad, D_A), BF16),
                 jax.ShapeDtypeStruct((batch, seq_pad // DIL_CLASSES, cls_width), BF16),
                 jax.ShapeDtypeStruct((batch, seq_pad // DIL_CLASSES, cls_width), BF16),
                 jax.ShapeDtypeStruct((batch, D_A, seq_pad), BF16),
                 jax.ShapeDtypeStruct((n // tq, N_KV_B, 2 * HEAD_DIM, cols), BF16),
                 jax.ShapeDtypeStruct((n, D_KV_B), BF16),
                 jax.ShapeDtypeStruct((n // tm, N_KV_B, HEAD_DIM, tm), BF16),
                 jax.ShapeDtypeStruct((n, D_C), F32)]
    padded = (2, 3, 4, 5)
    zero_bufs = [jnp.zeros(out_shape[k].shape, BF16) for k in padded]
    n_in = 10
    return pl.pallas_call(
        functools.partial(_inproj_body, tq=tq),
        grid=(n // tm,),
        in_specs=[row(D_MODEL), const((1, D_MODEL)), const((D_MODEL, D_IN)),
                  const((1, 2 * LANES)), const((1, 2 * LANES)), tab, tab, tab, tab,
                  const((2 * LANES, 2 * LANES))] + [pl.BlockSpec(memory_space=pl.ANY)] * len(padded),
        out_specs=out_specs,
        out_shape=out_shape,
        input_output_aliases={n_in + k: o for k, o in enumerate(padded)},
        scratch_shapes=[pltpu.VMEM((D_A // LANES, tm, LANES), F32)],
        name="inproj",
        compiler_params=_params("parallel"),
    )(x2d, g_pre, w_in, gq, gk, *tabs, head_mask, *zero_bufs)


DIL_NEAR = DILATED_PATTERNS[:-1]
NEAR_HALO = max(w // 2 for w, _ in DIL_NEAR)
FAR_HALO = DIL_FAR[0] // (2 * DIL_FAR[1])
STAT_LANES = LANES


def _band_bias(tq, halo, patterns):
    delta = np.arange(tq + 2 * halo)[None, :] - halo - np.arange(tq)[:, None]
    mult = np.zeros(delta.shape, np.int64)
    for window, dil in patterns:
        mult += (np.abs(delta) <= (window // (2 * dil)) * dil) & (delta % dil == 0)
    return np.where(mult > 0, np.log2(np.maximum(mult, 1)), NEG_INF).astype(np.float32)


def _tile_bias_tables(tq, halo, patterns, length):
    base = _band_bias(tq, halo, patterns).T
    n_tiles, edge = length // tq, -(-halo // tq)
    compress = n_tiles > 2 * edge + 1
    tiles = list(range(edge + 1)) + list(range(n_tiles - edge, n_tiles)) if compress else list(range(n_tiles))
    tabs = []
    for t in tiles:
        kpos = t * tq - halo + np.arange(tq + 2 * halo)
        inside = (kpos >= 0) & (kpos < length)
        tabs.append(np.where(inside[:, None], base, NEG_INF))
    tables = jnp.asarray(np.tile(np.stack(tabs), (1, 1, N_HEADS_A)), F32)

    def index(t):
        if not compress:
            return t
        if isinstance(t, int):
            return t if t < edge else (t - (n_tiles - 2 * edge - 1) if t >= n_tiles - edge else edge)
        return jnp.where(t < edge, t, jnp.where(t >= n_tiles - edge, t - (n_tiles - 2 * edge - 1), edge))

    return tables, index


def _band_attention_t(q, kw, vtw, bias_t):
    tq = q.shape[0]
    qt = q.astype(F32).T.astype(BF16)
    rows = lax.broadcasted_iota(jnp.int32, (D_A, N_HEADS_A * tq), 0) // HEAD_DIM
    cols = lax.broadcasted_iota(jnp.int32, (D_A, N_HEADS_A * tq), 1) // tq
    qbd = jnp.where(rows == cols, jnp.concatenate([qt] * N_HEADS_A, axis=1), jnp.zeros((), BF16))
    st = jnp.dot(kw, qbd, preferred_element_type=F32)
    st = st + bias_t
    m = jnp.max(st, axis=0, keepdims=True)
    p = jnp.exp2(st - m)
    l = jnp.sum(p, axis=0, keepdims=True)
    ot_all = jnp.dot(vtw, p.astype(BF16), preferred_element_type=F32)
    ot = jnp.concatenate([ot_all[h * HEAD_DIM:(h + 1) * HEAD_DIM, h * tq:(h + 1) * tq]
                          for h in range(N_HEADS_A)], axis=0)
    return ot, m, l


def _dilated_near_body(q_ref, k_ref, vt_ref, bias_ref, acc_ref, stat_ref, acc_nat, stat_nat, *, tiles, bias_index):
    tq = DIL_TQ
    win = tq + 2 * NEAR_HALO
    for t in range(tiles):
        tile = pl.program_id(1) * tiles + t
        start = pl.multiple_of(tile * tq + (DIL_HALO - NEAR_HALO), tq)
        ot, m, l = _band_attention_t(q_ref[0, t * tq:(t + 1) * tq, :], k_ref[0, pl.ds(start, win), :],
                                     vt_ref[0, :, pl.ds(start, win)], bias_ref[bias_index(tile)])
        o = ot.T
        for c in range(D_A // LANES):
            acc_nat[c, t * tq:(t + 1) * tq, :] = o[:, c * LANES:(c + 1) * LANES]
        stat_rows = ([m[:, h * tq:(h + 1) * tq] for h in range(N_HEADS_A)]
                     + [l[:, h * tq:(h + 1) * tq] for h in range(N_HEADS_A)]
                     + [jnp.zeros((STAT_LANES - 2 * N_HEADS_A, tq), F32)])
        stat_nat[t * tq:(t + 1) * tq, :] = jnp.concatenate(stat_rows, axis=0).T
    cls_rows = tiles * tq // DIL_CLASSES
    for r in range(DIL_CLASSES):
        for c in range(D_A // LANES):
            lo = r * D_A + c * LANES
            acc_ref[0, :, lo:lo + LANES] = acc_nat[c, pl.ds(r, cls_rows, stride=DIL_CLASSES), :]
        stat_ref[0, :, r * STAT_LANES:(r + 1) * STAT_LANES] = stat_nat[pl.ds(r, cls_rows, stride=DIL_CLASSES), :]


def _dilated_far_body(q_ref, k_ref, v_ref, acc_ref, stat_ref, bias_ref, o_ref, *, cls_len, bias_index):
    tq = DIL_TQ
    win = tq + 2 * FAR_HALO
    vt = v_ref[0].astype(F32).T.astype(BF16)
    for t in range(cls_len // tq):
        rs = slice(t * tq, (t + 1) * tq)
        ot2, m2, l2 = _band_attention_t(q_ref[0, rs, :], k_ref[0, t * tq:t * tq + win, :],
                                        vt[:, t * tq:t * tq + win], bias_ref[bias_index(t)])
        ot1 = acc_ref[0, rs, :].T
        st1 = stat_ref[0, rs, :].T
        outs = []
        for h in range(N_HEADS_A):
            hs = slice(h * HEAD_DIM, (h + 1) * HEAD_DIM)
            cs = slice(h * tq, (h + 1) * tq)
            m1, l1 = st1[h:h + 1, :], st1[N_HEADS_A + h:N_HEADS_A + h + 1, :]
            m = jnp.maximum(m1, m2[:, cs])
            w1, w2 = jnp.exp2(m1 - m), jnp.exp2(m2[:, cs] - m)
            outs.append((w1 * ot1[hs, :] + w2 * ot2[hs, :]) / (w1 * l1 + w2 * l2[:, cs]))
        o_ref[0, rs, :] = jnp.concatenate(outs, axis=0).T


def _dilated(qa, qa_cls, ka_pad, ka_cls, va_cls, vat_pad):
    b, seq, _ = qa.shape
    dil = DIL_CLASSES
    cls_len = seq // dil
    tq = DIL_TQ
    tiles = DIL_NEAR_TILES
    seq_pad = seq + 2 * DIL_HALO
    cls_pad = cls_len + 2 * FAR_HALO
    step_cls = tiles * tq // dil
    near_bias, near_index = _tile_bias_tables(tq, NEAR_HALO, DIL_NEAR, seq)
    far_bias, far_index = _tile_bias_tables(tq, FAR_HALO, ((2 * FAR_HALO, 1),), cls_len)
    step_blk = lambda width: pl.BlockSpec((1, step_cls, dil * width), lambda bi, i: (bi, i, 0))
    acc, stats = pl.pallas_call(
        functools.partial(_dilated_near_body, tiles=tiles, bias_index=near_index),
        grid=(b, seq // (tiles * tq)),
        in_specs=[pl.BlockSpec((1, tiles * tq, D_A), lambda bi, i: (bi, i, 0)),
                  pl.BlockSpec((1, seq_pad, D_A), lambda bi, i: (bi, 0, 0)),
                  pl.BlockSpec((1, D_A, seq_pad), lambda bi, i: (bi, 0, 0)),
                  pl.BlockSpec(near_bias.shape, lambda bi, i: (0, 0, 0))],
        out_specs=[step_blk(D_A), step_blk(STAT_LANES)],
        out_shape=[jax.ShapeDtypeStruct((b, cls_len, dil * D_A), F32),
                   jax.ShapeDtypeStruct((b, cls_len, dil * STAT_LANES), F32)],
        scratch_shapes=[pltpu.VMEM((D_A // LANES, tiles * tq, LANES), F32),
                        pltpu.VMEM((tiles * tq, STAT_LANES), F32)],
        name="dilated_near",
        compiler_params=_params("parallel", "arbitrary"),
    )(qa, ka_pad, vat_pad, near_bias)
    cls_blk = lambda width: pl.BlockSpec((1, cls_len, width), lambda bi, r: (bi, 0, r))
    cls_kv = pl.BlockSpec((1, cls_pad, D_A), lambda bi, r: (bi, 0, r))
    return pl.pallas_call(
        functools.partial(_dilated_far_body, cls_len=cls_len, bias_index=far_index),
        grid=(b, dil),
        in_specs=[cls_blk(D_A), cls_kv, cls_kv, cls_blk(D_A), cls_blk(STAT_LANES),
                  pl.BlockSpec(far_bias.shape, lambda bi, r: (0, 0, 0))],
        out_specs=cls_blk(D_A),
        out_shape=jax.ShapeDtypeStruct((b, cls_len, dil * D_A), F32),
        name="dilated_far",
        compiler_params=_params("parallel", "parallel"),
    )(qa_cls, ka_cls, va_cls, acc, stats, far_bias)


def _grid_attn_body(qt_ref, k_ref, vt_ref, o_ref, s0_ref, s1_ref, acc_ref, *, tq, tk, nk):
    cols = acc_ref.shape[2]
    vparts = tk // vt_ref.shape[-1]
    for g in range(N_KV_B):
        qt = qt_ref[0, g]

        def scores(kb):
            k0 = pl.multiple_of(kb * tk, tk)
            return jnp.dot(k_ref[0, pl.ds(k0, tk), :], qt, preferred_element_type=F32)

        def consume(s_ref, kb, carry):
            m_prev, l_prev = carry
            s = s_ref[...]
            m_new = jnp.maximum(m_prev, jnp.max(s, axis=0, keepdims=True))
            alpha = jnp.exp2(m_prev - m_new)
            p = jnp.exp2(s - m_new)
            l_new = alpha * l_prev + jnp.sum(p, axis=0, keepdims=True)
            vt = jnp.concatenate([vt_ref[kb * vparts + u, g] for u in range(vparts)], axis=1)
            pv = jnp.dot(vt, p.astype(BF16), preferred_element_type=F32)
            acc_ref[g] = alpha * acc_ref[g] + pv
            return m_new, l_new

        s0_ref[...] = scores(0)
        acc_ref[g] = jnp.zeros(acc_ref.shape[1:], F32)

        def pair(j, m):
            kb = 2 * j
            s1_ref[...] = scores(kb + 1)
            m = consume(s0_ref, kb, m)
            s0_ref[...] = scores(jnp.minimum(kb + 2, nk - 1))
            return consume(s1_ref, kb + 1, m)

        _, l_fin = lax.fori_loop(0, nk // 2, pair, (jnp.full((1, cols), NEG_INF, F32), jnp.zeros((1, cols), F32)))
        acc_ref[g] = acc_ref[g] / l_fin
    ot = acc_ref[...].reshape(N_KV_B * HEAD_DIM, cols).T
    heads = [ot[r * tq:(r + 1) * tq, g * HEAD_DIM:(g + 1) * HEAD_DIM] for g in range(N_KV_B) for r in range(GQA_GROUP)]
    o_ref[0] = jnp.concatenate(heads, axis=1).astype(o_ref.dtype)


def _grid_attn(qt, kb, vt, tq, tk):
    b, seq, _ = kb.shape
    cols = GQA_GROUP * tq
    tv = vt.shape[-1]
    nq, nk = seq // tq, seq // tk
    return pl.pallas_call(
        functools.partial(_grid_attn_body, tq=tq, tk=tk, nk=nk),
        grid=(b, nq),
        in_specs=[pl.BlockSpec((1, N_KV_B, 2 * HEAD_DIM, cols), lambda bi, i: (bi * nq + i, 0, 0, 0)),
                  pl.BlockSpec((1, seq, D_KV_B), lambda bi, i: (bi, 0, 0)),
                  pl.BlockSpec((seq // tv, N_KV_B, HEAD_DIM, tv), lambda bi, i: (bi, 0, 0, 0))],
        out_specs=pl.BlockSpec((1, tq, D_B), lambda bi, i: (bi, i, 0)),
        out_shape=jax.ShapeDtypeStruct((b, seq, D_B), BF16),
        scratch_shapes=[pltpu.VMEM((tk, cols), F32), pltpu.VMEM((tk, cols), F32),
                        pltpu.VMEM((N_KV_B, HEAD_DIM, cols), F32)],
        name="grid_attn",
        compiler_params=_params("parallel", "arbitrary"),
    )(qt, kb, vt)


def _fft_tables(seq):
    n1, n2 = seq // FFT_N2, FFT_N2
    k1 = np.arange(n1)
    ang1 = 2.0 * np.pi * ((k1[:, None] * k1[None, :]) % n1) / n1
    f1 = np.concatenate([np.cos(ang1), -np.sin(ang1)], axis=0)
    k = k1[:, None, None] + n1 * np.arange(n2)[None, :, None]
    ang2 = 2.0 * np.pi * ((k * np.arange(n2)[None, None, :]) % seq) / seq
    g = np.concatenate([np.cos(ang2), np.sin(ang2)], axis=-1)
    c = np.arange(HEAD_DIM)
    ang3 = 2.0 * np.pi * ((c[:, None] * c[None, :]) % HEAD_DIM) / HEAD_DIM
    eye = np.eye(N_GROUPS_C)
    dc, ds = np.kron(eye, np.cos(ang3)), np.kron(eye, np.sin(ang3))
    d = np.concatenate([np.concatenate([dc, -ds], axis=1), np.concatenate([ds, dc], axis=1)], axis=0)
    return tuple(jnp.asarray(t, BF16) for t in (f1, g, d))


def _fft1_body(u_ref, f_ref, a_ref):
    a_ref[0] = jnp.dot(f_ref[...], u_ref[0].astype(BF16), preferred_element_type=F32).astype(BF16)


def _fft2_body(a_ref, g_ref, d_ref, o_ref, *, kb1):
    for j in range(kb1):
        ar, ai = a_ref[0, 0, j], a_ref[0, 1, j]
        p = (jnp.dot(ar, d_ref[:D_C, :], preferred_element_type=F32)
             + jnp.dot(ai, d_ref[D_C:, :], preferred_element_type=F32)).astype(BF16)
        gj = g_ref[j]
        o_ref[0, :, j * D_C:(j + 1) * D_C] = (
            jnp.dot(gj[:, :FFT_N2], p[:, :D_C], preferred_element_type=F32)
            + jnp.dot(gj[:, FFT_N2:], p[:, D_C:], preferred_element_type=F32)).astype(o_ref.dtype)


def _fourier(uc, tables):
    b, seq, _ = uc.shape
    f1, g, d = tables
    n1, n2 = seq // FFT_N2, FFT_N2
    cols = n2 * D_C
    tc = 4096
    a = pl.pallas_call(
        _fft1_body,
        grid=(b, cols // tc),
        in_specs=[pl.BlockSpec((1, n1, tc), lambda bi, j: (bi, 0, j)),
                  pl.BlockSpec((2 * n1, n1), lambda bi, j: (0, 0))],
        out_specs=pl.BlockSpec((1, 2 * n1, tc), lambda bi, j: (bi, 0, j)),
        out_shape=jax.ShapeDtypeStruct((b, 2 * n1, cols), BF16),
        name="fft1",
        compiler_params=_params("parallel", "parallel"),
    )(uc.reshape(b, n1, cols), f1)
    kb1 = 8
    z = pl.pallas_call(
        functools.partial(_fft2_body, kb1=kb1),
        grid=(b, n1 // kb1),
        in_specs=[pl.BlockSpec((1, 2, kb1, n2, D_C), lambda bi, j: (bi, 0, j, 0, 0)),
                  pl.BlockSpec((kb1, n2, 2 * n2), lambda bi, j: (j, 0, 0)),
                  pl.BlockSpec((2 * D_C, 2 * D_C), lambda bi, j: (0, 0))],
        out_specs=pl.BlockSpec((1, n2, kb1 * D_C), lambda bi, j: (bi, 0, j)),
        out_shape=jax.ShapeDtypeStruct((b, n2, n1 * D_C), BF16),
        name="fft2",
        compiler_params=_params("parallel", "parallel"),
    )(a.reshape(b, 2, n1, n2, D_C), g, d)
    return z.reshape(b * seq, D_C)


def _outproj_body(oac_ref, ob_ref, oc_ref, x_ref, gh_ref, w_ref, gpost_ref, gffn_ref, m_ref,
                  x1_ref, hn_ref, oa_ref):
    cls_rows = oa_ref.shape[1] // DIL_CLASSES
    for r in range(DIL_CLASSES):
        for c in range(D_A // LANES):
            lo = r * D_A + c * LANES
            oa_ref[c, pl.ds(r, cls_rows, stride=DIL_CLASSES), :] = oac_ref[0, :, lo:lo + LANES]
    oa = jnp.concatenate([oa_ref[c] for c in range(D_A // LANES)], axis=1)
    mix = None
    col = 0
    width = 2 * LANES
    for o_src in (oa, ob_ref, oc_ref):
        for c in range(o_src.shape[-1] // width):
            o = o_src[:, c * width:(c + 1) * width].astype(F32)
            y = o * lax.rsqrt(_head_mean_sq(o, m_ref) + RMS_EPS) * gh_ref[:, col:col + width]
            part = jnp.dot(y.astype(BF16), w_ref[col:col + width, :], preferred_element_type=F32)
            mix = part if mix is None else mix + part
            col += width
    x1 = x_ref[...] + _rms(mix, gpost_ref[...])
    x1_ref[...] = x1
    hn_ref[...] = _rms(x1, gffn_ref[...]).astype(BF16)


def _outproj(oa_cls, ob, oc, x2d, seq, g_heads, w_out, g_post, g_ffn, head_mask, tm):
    n = x2d.shape[0]
    bps = seq // tm
    row = lambda width: pl.BlockSpec((tm, width), lambda i: (i, 0))
    const = lambda shape: pl.BlockSpec(shape, lambda i: (0, 0))
    return pl.pallas_call(
        _outproj_body,
        grid=(n // tm,),
        in_specs=[pl.BlockSpec((1, tm // DIL_CLASSES, DIL_CLASSES * D_A), lambda i: (i // bps, i % bps, 0)),
                  row(D_B), row(D_C), row(D_MODEL), const((1, D_MIX)),
                  const((D_MIX, D_MODEL)), const((1, D_MODEL)), const((1, D_MODEL)),
                  const((2 * LANES, 2 * LANES))],
        out_specs=[row(D_MODEL), row(D_MODEL)],
        out_shape=[jax.ShapeDtypeStruct((n, D_MODEL), F32), jax.ShapeDtypeStruct((n, D_MODEL), BF16)],
        scratch_shapes=[pltpu.VMEM((D_A // LANES, tm, LANES), F32)],
        name="outproj",
        compiler_params=_params("parallel"),
    )(oa_cls, ob, oc, x2d, g_heads, w_out, g_post, g_ffn, head_mask)


def _ffn_body(h_ref, hp_ref, hx_ref, x_ref, wg_ref, wu_ref, wd_ref, cw_ref, cb_ref, gpost_ref,
              o_ref, hext_ref, acc_ref, *, tm, blocks_per_seq):
    i = pl.program_id(0)
    c = pl.program_id(1)
    pad = BF16_ROWS
    rows = tm + 2 * pad

    @pl.when(c == 0)
    def _():
        first = (i % blocks_per_seq) == 0
        last = (i % blocks_per_seq) == blocks_per_seq - 1
        hext_ref[:pad, :] = jnp.where(first, jnp.zeros_like(hp_ref[0]), hp_ref[0])
        hext_ref[pad:pad + tm, :] = h_ref[...]
        hext_ref[pad + tm:, :] = jnp.where(last, jnp.zeros_like(hx_ref[0]), hx_ref[0])
        acc_ref[...] = jnp.zeros(acc_ref.shape, F32)

    g = jnp.dot(hext_ref[...], wg_ref[...], preferred_element_type=F32)
    cw = cw_ref[...]
    g = (pltpu.roll(g, 1, axis=0)[pad:pad + tm] * cw[0:1, :] + g[pad:pad + tm] * cw[1:2, :]
         + pltpu.roll(g, rows - 1, axis=0)[pad:pad + tm] * cw[2:3, :] + cb_ref[...])
    up = jnp.dot(h_ref[...], wu_ref[...], preferred_element_type=F32)
    inner = g * (GELU_C0 + GELU_C1 * (g * g))
    act = (g * (1.0 + jnp.tanh(inner)) * up).astype(BF16)
    acc_ref[...] += jnp.dot(act, wd_ref[...], preferred_element_type=F32)

    @pl.when(c == pl.num_programs(1) - 1)
    def _():
        o_ref[...] = x_ref[...] + _rms(acc_ref[...], gpost_ref[...])


def _ffn(hn, x1, seq, w_gate, w_up, w_down, conv_w, conv_b, g_post, tm, tf):
    n = x1.shape[0]
    nf = D_FF // tf
    blocks_per_seq = seq // tm
    halo_blocks = tm // BF16_ROWS
    n_halo = n // BF16_ROWS
    hn3 = hn.reshape(n_halo, BF16_ROWS, D_MODEL)
    row = lambda: pl.BlockSpec((tm, D_MODEL), lambda i, c: (i, 0))
    prev = pl.BlockSpec((1, BF16_ROWS, D_MODEL), lambda i, c: (jnp.maximum(i * halo_blocks - 1, 0), 0, 0))
    nxt = pl.BlockSpec((1, BF16_ROWS, D_MODEL),
                       lambda i, c: (jnp.minimum((i + 1) * halo_blocks, n_halo - 1), 0, 0))
    cols = lambda rows: pl.BlockSpec((rows, tf), lambda i, c: (0, c))
    return pl.pallas_call(
        functools.partial(_ffn_body, tm=tm, blocks_per_seq=blocks_per_seq),
        grid=(n // tm, nf),
        in_specs=[row(), prev, nxt, row(), cols(D_MODEL), cols(D_MODEL),
                  pl.BlockSpec((tf, D_MODEL), lambda i, c: (c, 0)), cols(3), cols(1),
                  pl.BlockSpec((1, D_MODEL), lambda i, c: (0, 0))],
        out_specs=row(),
        out_shape=jax.ShapeDtypeStruct((n, D_MODEL), F32),
        scratch_shapes=[pltpu.VMEM((tm + 2 * BF16_ROWS, D_MODEL), BF16), pltpu.VMEM((tm, D_MODEL), F32)],
        name="ffn",
        compiler_params=_params("parallel", "arbitrary"),
    )(hn, hn3, hn3, x1, w_gate, w_up, w_down, conv_w, conv_b, g_post)


def _rope_tables(pos, rot_dim, theta):
    half = rot_dim // 2
    inv = jnp.power(jnp.float32(theta), -jnp.arange(half, dtype=jnp.float32) / half)
    ang = pos.astype(jnp.float32)[:, None] * inv[None, :]
    return jnp.cos(ang), jnp.sin(ang)


def _position_tables(seq):
    rows = seq // GRID_W
    t = jnp.arange(seq, dtype=jnp.int32)
    row = jnp.broadcast_to(jnp.arange(rows, dtype=jnp.int32)[:, None], (rows, GRID_W)).reshape(-1)
    col = jnp.broadcast_to(jnp.arange(GRID_W, dtype=jnp.int32)[None, :], (rows, GRID_W)).reshape(-1)
    cos_t, sin_t = _rope_tables(t, ROT_DIM_PARTIAL, ROPE_THETA_PARTIAL)
    cos_r, sin_r = _rope_tables(row, HEAD_DIM // 2, ROPE_THETA_AXIAL)
    cos_c, sin_c = _rope_tables(col, HEAD_DIM // 2, ROPE_THETA_AXIAL)
    rest = HEAD_DIM - ROT_DIM_PARTIAL
    ca = jnp.concatenate([cos_t, cos_t, jnp.ones((seq, rest), F32)], axis=-1)
    sa = jnp.concatenate([-sin_t, sin_t, jnp.zeros((seq, rest), F32)], axis=-1)
    cb = jnp.concatenate([cos_r, cos_r, cos_c, cos_c], axis=-1)
    sb = jnp.concatenate([-sin_r, sin_r, -sin_c, sin_c], axis=-1)
    reps = LANES // HEAD_DIM
    return tuple(jnp.tile(t_, (1, reps)) for t_ in (ca, sa, cb, sb))


def _head_mask():
    lane = np.arange(2 * LANES)
    same = (lane[:, None] // HEAD_DIM) == (lane[None, :] // HEAD_DIM)
    return jnp.asarray(same / HEAD_DIM, BF16)


def _trunk(x, w):
    b, seq, _ = x.shape
    n = b * seq
    tm, tq, tk = TOKEN_TILE, GRID_TQ, min(GRID_TK, seq // 2)
    tabs = _position_tables(seq)
    fft_tabs = _fft_tables(seq)
    head_mask = _head_mask()
    x2d = x.reshape(n, D_MODEL)
    depth = w["w_in"].shape[0]
    for l in range(depth):
        tile4 = lambda g_: jnp.tile(g_[l][None, :], (1, 2 * LANES // HEAD_DIM))
        qa, qa_cls, ka_pad, ka_cls, va_cls, vat_pad, qt, kb, vt, uc = _inproj(
            x2d, b, seq, w["g_mix_pre"][l][None], w["w_in"][l], tile4(w["g_q"]), tile4(w["g_k"]),
            tabs, head_mask, tm, tq)
        oa_cls = _dilated(qa.reshape(b, seq, D_A), qa_cls, ka_pad, ka_cls, va_cls, vat_pad)
        ob = _grid_attn(qt, kb.reshape(b, seq, D_KV_B), vt, tq, tk).reshape(n, D_B)
        oc = _fourier(uc.reshape(b, seq, D_C), fft_tabs)
        x1, hn = _outproj(oa_cls, ob, oc, x2d, seq, w["g_heads"][l][None], w["w_out"][l], w["g_mix_post"][l][None],
                          w["g_ffn_pre"][l][None], head_mask, tm)
        x2d = _ffn(hn, x1, seq, w["w_gate"][l], w["w_up"][l], w["w_down"][l], w["conv_w"][l],
                   w["conv_b"][l][None], w["g_ffn_post"][l][None], FFN_TILE, FFN_CHUNK)
    return x2d.reshape(b, seq, D_MODEL)


def _prep_weights(g_mix_pre, g_mix_post, w_in, g_q, g_k, g_heads, w_out, g_ffn_pre, g_ffn_post,
                  w_gate, w_up, conv_w, conv_b, w_down):
    return dict(
        g_mix_pre=g_mix_pre, g_mix_post=g_mix_post, g_q=g_q, g_k=g_k, g_heads=g_heads,
        g_ffn_pre=g_ffn_pre, g_ffn_post=g_ffn_post,
        w_in=w_in.astype(BF16), w_out=w_out.astype(BF16),
        w_gate=w_gate.astype(BF16), w_up=(0.5 * w_up).astype(BF16), w_down=w_down.astype(BF16),
        conv_w=conv_w, conv_b=conv_b)


def kernel(x_prompt, x_sample, g_mix_pre, g_mix_post, w_in, g_q, g_k, g_heads, w_out, g_ffn_pre, g_ffn_post,
           w_gate, w_up, conv_w, conv_b, w_down):
    w = _prep_weights(g_mix_pre, g_mix_post, w_in, g_q, g_k, g_heads, w_out, g_ffn_pre, g_ffn_post,
                      w_gate, w_up, conv_w, conv_b, w_down)
    return (_trunk(x_prompt, w), _trunk(x_sample, w))
```

```python
import functools

import numpy as np
import jax
import jax.numpy as jnp
from jax import lax
from jax.experimental import pallas as pl
from jax.experimental.pallas import tpu as pltpu

F32 = jnp.float32
BF16 = jnp.bfloat16

D_MODEL = 1024
HEAD_DIM = 64
N_HEADS_A = 4
DILATED_PATTERNS = ((128, 1), (512, 4), (2048, 16))
N_HEADS_B = 8
N_KV_B = 2
GQA_GROUP = N_HEADS_B // N_KV_B
N_GROUPS_C = 4
D_A = N_HEADS_A * HEAD_DIM
D_B = N_HEADS_B * HEAD_DIM
D_KV_B = N_KV_B * HEAD_DIM
D_C = N_GROUPS_C * HEAD_DIM
D_MIX = D_A + D_B + D_C
D_IN = 3 * D_A + D_B + 2 * D_KV_B + D_C
D_FF = 4 * D_MODEL
ROPE_THETA_PARTIAL = 500000.0
ROT_DIM_PARTIAL = HEAD_DIM // 4
ROPE_THETA_AXIAL = 10000.0
GRID_W = 64
RMS_EPS = 1e-6
NEG_INF = -1e30
Q_SCALE = HEAD_DIM ** -0.5 * float(np.log2(np.e))

LANES = 128
BF16_ROWS = 16
DIL_HALO = max(w // 2 for w, _ in DILATED_PATTERNS)
DIL_TQ = 128
DIL_FAR = DILATED_PATTERNS[-1]
DIL_CLASSES = DIL_FAR[1]
DIL_NEAR_TILES = 4
FFT_N2 = 128
VMEM_LIMIT = 56 * 1024 * 1024
TOKEN_TILE = 512
GRID_TQ = 1024
GRID_TK = 512
FFN_TILE = 1024
FFN_CHUNK = 1024
GELU_C0 = float(np.sqrt(2.0 / np.pi))
GELU_C1 = 0.044715 * GELU_C0


def _params(*sem):
    return pltpu.CompilerParams(dimension_semantics=sem, vmem_limit_bytes=VMEM_LIMIT)


def _rms(x, g):
    return x * lax.rsqrt(jnp.mean(x * x, axis=-1, keepdims=True) + RMS_EPS) * g


def _head_mean_sq(x, m_ref):
    return jnp.dot((x * x).astype(BF16), m_ref[...], preferred_element_type=F32)


def _rotate_half(x, half):
    n = x.shape[-1]
    lane = lax.broadcasted_iota(jnp.int32, (1, n), 1)
    fwd = pltpu.roll(x, n - half, axis=1)
    bwd = pltpu.roll(x, half, axis=1)
    return jnp.where((lane % (2 * half)) < half, fwd, bwd)


def _inproj_body(x_ref, g_ref, w_ref, gq_ref, gk_ref, ca_ref, sa_ref, cb_ref, sb_ref, m_ref,
                 ka_zero, kac_zero, vac_zero, vat_zero,
                 qa_ref, qac_ref, ka_ref, kac_ref, vac_ref, vat_ref, qt_ref, kb_ref, vt_ref, uc_ref,
                 rel_ref):
    del ka_zero, kac_zero, vac_zero, vat_zero
    tm = x_ref.shape[0]
    h = _rms(x_ref[...], g_ref[...]).astype(BF16)

    def proj(lo, width):
        return jnp.dot(h, w_ref[:, lo:lo + width], preferred_element_type=F32)

    ca, sa, cb, sb = ca_ref[...], sa_ref[...], cb_ref[...], sb_ref[...]

    def rope_a(p):
        return p * ca + _rotate_half(p, ROT_DIM_PARTIAL // 2) * sa

    def norm_b(p, gain):
        return p * lax.rsqrt(_head_mean_sq(p, m_ref) + RMS_EPS) * gain

    def rope_b(y):
        return y * cb + _rotate_half(y, HEAD_DIM // 4) * sb

    def to_classes(val, out_ref):
        for c in range(D_A // LANES):
            rel_ref[c] = val[:, c * LANES:(c + 1) * LANES]
        for r in range(DIL_CLASSES):
            for c in range(D_A // LANES):
                lo = r * D_A + c * LANES
                out_ref[0, :, lo:lo + LANES] = rel_ref[c, pl.ds(r, tm // DIL_CLASSES, stride=DIL_CLASSES), :].astype(BF16)

    pq = proj(0, D_A)
    qa = jnp.concatenate([rope_a(pq[:, c * LANES:(c + 1) * LANES]) for c in range(D_A // LANES)], axis=1) * Q_SCALE
    qa_ref[...] = qa.astype(BF16)
    to_classes(qa, qac_ref)
    pk = proj(D_A, D_A)
    ka = jnp.concatenate([rope_a(pk[:, c * LANES:(c + 1) * LANES]) for c in range(D_A // LANES)], axis=1)
    ka_ref[0] = ka.astype(BF16)
    to_classes(ka, kac_ref)
    pv = proj(2 * D_A, D_A)
    to_classes(pv, vac_ref)
    vat_ref[0] = pv.T.astype(BF16)

    gq, gk = gq_ref[...], gk_ref[...]
    zeros = jnp.zeros((HEAD_DIM, GQA_GROUP * tm), BF16)
    for g in range(N_KV_B):
        y = norm_b(proj(3 * D_A + g * 2 * LANES, 2 * LANES), gq)
        for c in range(2):
            yt = (rope_b(y[:, c * LANES:(c + 1) * LANES]) * Q_SCALE).T.astype(BF16)
            for hh in range(LANES // HEAD_DIM):
                r = c * (LANES // HEAD_DIM) + hh
                qt_ref[0, g, g * HEAD_DIM:(g + 1) * HEAD_DIM, r * tm:(r + 1) * tm] = yt[hh * HEAD_DIM:(hh + 1) * HEAD_DIM, :]
        qt_ref[0, g, (1 - g) * HEAD_DIM:(2 - g) * HEAD_DIM, :] = zeros
    pkv = proj(3 * D_A + D_B, 2 * D_KV_B)
    kb_ref[...] = rope_b(norm_b(pkv, gk)[:, :D_KV_B]).astype(BF16)
    vbt = pkv[:, D_KV_B:].T.astype(BF16)
    for g in range(N_KV_B):
        vt_ref[0, g] = vbt[g * HEAD_DIM:(g + 1) * HEAD_DIM, :]
    uc_ref[...] = proj(3 * D_A + D_B + 2 * D_KV_B, D_C)


def _inproj(x2d, batch, seq, g_pre, w_in, gq, gk, tabs, head_mask, tm, tq):
    n = x2d.shape[0]
    bps = seq // tm
    halo_blocks = DIL_HALO // tm
    seq_pad = seq + 2 * DIL_HALO
    parts = tq // tm
    row = lambda width: pl.BlockSpec((tm, width), lambda i: (i, 0))
    const = lambda shape: pl.BlockSpec(shape, lambda i: (0, 0))
    tab = pl.BlockSpec((tm, LANES), lambda i: (i % bps, 0))
    pad_rows = pl.BlockSpec((1, tm, D_A), lambda i: (i // bps, i % bps + halo_blocks, 0))
    pad_cols = pl.BlockSpec((1, D_A, tm), lambda i: (i // bps, 0, i % bps + halo_blocks))
    cls_rows, cls_width = tm // DIL_CLASSES, DIL_CLASSES * D_A
    cls = pl.BlockSpec((1, cls_rows, cls_width), lambda i: (i // bps, i % bps, 0))
    cls_pad = pl.BlockSpec((1, cls_rows, cls_width), lambda i: (i // bps, i % bps + halo_blocks, 0))
    out_specs = [row(D_A), cls, pad_rows, cls_pad, cls_pad, pad_cols,
                 pl.BlockSpec((1, N_KV_B, 2 * HEAD_DIM, GQA_GROUP * tm), lambda i: (i // parts, 0, 0, i % parts)),
                 row(D_KV_B),
                 pl.BlockSpec((1, N_KV_B, HEAD_DIM, tm), lambda i: (i, 0, 0, 0)),
                 row(D_C)]
    out_shape = [jax.ShapeDtypeStruct((n, D_A), BF16),
                 jax.ShapeDtypeStruct((batch, seq // DIL_CLASSES, cls_width), BF16),
                 jax.ShapeDtypeStruct((batch, seq_pad, D_A), BF16),
                 jax.ShapeDtypeStruct((batch, seq_pad // DIL_CLASSES, cls_width), BF16),
                 jax.ShapeDtypeStruct((batch, seq_pad // DIL_CLASSES, cls_width), BF16),
                 jax.ShapeDtypeStruct((batch, D_A, seq_pad), BF16),
                 jax.ShapeDtypeStruct((n // tq, N_KV_B, 2 * HEAD_DIM, GQA_GROUP * tq), BF16),
                 jax.ShapeDtypeStruct((n, D_KV_B), BF16),
                 jax.ShapeDtypeStruct((n // tm, N_KV_B, HEAD_DIM, tm), BF16),
                 jax.ShapeDtypeStruct((n, D_C), F32)]
    padded = (2, 3, 4, 5)
    zero_bufs = [jnp.zeros(out_shape[k].shape, BF16) for k in padded]
    n_in = 10
    return pl.pallas_call(
        _inproj_body,
        grid=(n // tm,),
        in_specs=[row(D_MODEL), const((1, D_MODEL)), const((D_MODEL, D_IN)),
                  const((1, 2 * LANES)), const((1, 2 * LANES)), tab, tab, tab, tab,
                  const((2 * LANES, 2 * LANES))] + [pl.BlockSpec(memory_space=pl.ANY)] * len(padded),
        out_specs=out_specs,
        out_shape=out_shape,
        input_output_aliases={n_in + k: o for k, o in enumerate(padded)},
        scratch_shapes=[pltpu.VMEM((D_A // LANES, tm, LANES), F32)],
        name="inproj",
        compiler_params=_params("parallel"),
    )(x2d, g_pre, w_in, gq, gk, *tabs, head_mask, *zero_bufs)


DIL_NEAR = DILATED_PATTERNS[:-1]
NEAR_HALO = max(w // 2 for w, _ in DIL_NEAR)
FAR_HALO = DIL_FAR[0] // (2 * DIL_FAR[1])
STAT_LANES = LANES


def _band_bias(tq, halo, patterns):
    delta = np.arange(tq + 2 * halo)[None, :] - halo - np.arange(tq)[:, None]
    mult = np.zeros(delta.shape, np.int64)
    for window, dil in patterns:
        mult += (np.abs(delta) <= (window // (2 * dil)) * dil) & (delta % dil == 0)
    return np.where(mult > 0, np.log2(np.maximum(mult, 1)), NEG_INF).astype(np.float32)


def _tile_bias_tables(tq, halo, patterns, length):
    base = _band_bias(tq, halo, patterns).T
    n_tiles, edge = length // tq, -(-halo // tq)
    compress = n_tiles > 2 * edge + 1
    tiles = list(range(edge + 1)) + list(range(n_tiles - edge, n_tiles)) if compress else list(range(n_tiles))
    tabs = []
    for t in tiles:
        kpos = t * tq - halo + np.arange(tq + 2 * halo)
        inside = (kpos >= 0) & (kpos < length)
        tabs.append(np.where(inside[:, None], base, NEG_INF))
    tables = jnp.asarray(np.tile(np.stack(tabs), (1, 1, N_HEADS_A)), F32)

    def index(t):
        if not compress:
            return t
        if isinstance(t, int):
            return t if t < edge else (t - (n_tiles - 2 * edge - 1) if t >= n_tiles - edge else edge)
        return jnp.where(t < edge, t, jnp.where(t >= n_tiles - edge, t - (n_tiles - 2 * edge - 1), edge))

    return tables, index


def _band_attention_t(q, kw, vtw, bias_t):
    tq = q.shape[0]
    qt = q.astype(F32).T.astype(BF16)
    rows = lax.broadcasted_iota(jnp.int32, (D_A, N_HEADS_A * tq), 0) // HEAD_DIM
    cols = lax.broadcasted_iota(jnp.int32, (D_A, N_HEADS_A * tq), 1) // tq
    qbd = jnp.where(rows == cols, jnp.concatenate([qt] * N_HEADS_A, axis=1), jnp.zeros((), BF16))
    st = jnp.dot(kw, qbd, preferred_element_type=F32)
    st = st + bias_t
    m = jnp.max(st, axis=0, keepdims=True)
    p = jnp.exp2(st - m)
    l = jnp.sum(p, axis=0, keepdims=True)
    ot_all = jnp.dot(vtw, p.astype(BF16), preferred_element_type=F32)
    ot = jnp.concatenate([ot_all[h * HEAD_DIM:(h + 1) * HEAD_DIM, h * tq:(h + 1) * tq]
                          for h in range(N_HEADS_A)], axis=0)
    return ot, m, l


def _dilated_near_body(q_ref, k_ref, vt_ref, bias_ref, acc_ref, stat_ref, acc_nat, stat_nat, *, tiles, bias_index):
    tq = DIL_TQ
    win = tq + 2 * NEAR_HALO
    for t in range(tiles):
        tile = pl.program_id(1) * tiles + t
        start = pl.multiple_of(tile * tq + (DIL_HALO - NEAR_HALO), tq)
        ot, m, l = _band_attention_t(q_ref[0, t * tq:(t + 1) * tq, :], k_ref[0, pl.ds(start, win), :],
                                     vt_ref[0, :, pl.ds(start, win)], bias_ref[bias_index(tile)])
        o = ot.T
        for c in range(D_A // LANES):
            acc_nat[c, t * tq:(t + 1) * tq, :] = o[:, c * LANES:(c + 1) * LANES]
        stat_rows = ([m[:, h * tq:(h + 1) * tq] for h in range(N_HEADS_A)]
                     + [l[:, h * tq:(h + 1) * tq] for h in range(N_HEADS_A)]
                     + [jnp.zeros((STAT_LANES - 2 * N_HEADS_A, tq), F32)])
        stat_nat[t * tq:(t + 1) * tq, :] = jnp.concatenate(stat_rows, axis=0).T
    cls_rows = tiles * tq // DIL_CLASSES
    for r in range(DIL_CLASSES):
        for c in range(D_A // LANES):
            lo = r * D_A + c * LANES
            acc_ref[0, :, lo:lo + LANES] = acc_nat[c, pl.ds(r, cls_rows, stride=DIL_CLASSES), :]
        stat_ref[0, :, r * STAT_LANES:(r + 1) * STAT_LANES] = stat_nat[pl.ds(r, cls_rows, stride=DIL_CLASSES), :]


def _dilated_far_body(q_ref, k_ref, v_ref, acc_ref, stat_ref, bias_ref, o_ref, *, cls_len, bias_index):
    tq = DIL_TQ
    win = tq + 2 * FAR_HALO
    vt = v_ref[0].astype(F32).T.astype(BF16)
    for t in range(cls_len // tq):
        rs = slice(t * tq, (t + 1) * tq)
        ot2, m2, l2 = _band_attention_t(q_ref[0, rs, :], k_ref[0, t * tq:t * tq + win, :],
                                        vt[:, t * tq:t * tq + win], bias_ref[bias_index(t)])
        ot1 = acc_ref[0, rs, :].T
        st1 = stat_ref[0, rs, :].T
        outs = []
        for h in range(N_HEADS_A):
            hs = slice(h * HEAD_DIM, (h + 1) * HEAD_DIM)
            cs = slice(h * tq, (h + 1) * tq)
            m1, l1 = st1[h:h + 1, :], st1[N_HEADS_A + h:N_HEADS_A + h + 1, :]
            m = jnp.maximum(m1, m2[:, cs])
            w1, w2 = jnp.exp2(m1 - m), jnp.exp2(m2[:, cs] - m)
            outs.append((w1 * ot1[hs, :] + w2 * ot2[hs, :]) / (w1 * l1 + w2 * l2[:, cs]))
        o_ref[0, rs, :] = jnp.concatenate(outs, axis=0).T


def _dilated(qa, qa_cls, ka_pad, ka_cls, va_cls, vat_pad):
    b, seq, _ = qa.shape
    dil = DIL_CLASSES
    cls_len = seq // dil
    tq = DIL_TQ
    tiles = DIL_NEAR_TILES
    seq_pad = seq + 2 * DIL_HALO
    cls_pad = cls_len + 2 * FAR_HALO
    step_cls = tiles * tq // dil
    near_bias, near_index = _tile_bias_tables(tq, NEAR_HALO, DIL_NEAR, seq)
    far_bias, far_index = _tile_bias_tables(tq, FAR_HALO, ((2 * FAR_HALO, 1),), cls_len)
    step_blk = lambda width: pl.BlockSpec((1, step_cls, dil * width), lambda bi, i: (bi, i, 0))
    acc, stats = pl.pallas_call(
        functools.partial(_dilated_near_body, tiles=tiles, bias_index=near_index),
        grid=(b, seq // (tiles * tq)),
        in_specs=[pl.BlockSpec((1, tiles * tq, D_A), lambda bi, i: (bi, i, 0)),
                  pl.BlockSpec((1, seq_pad, D_A), lambda bi, i: (bi, 0, 0)),
                  pl.BlockSpec((1, D_A, seq_pad), lambda bi, i: (bi, 0, 0)),
                  pl.BlockSpec(near_bias.shape, lambda bi, i: (0, 0, 0))],
        out_specs=[step_blk(D_A), step_blk(STAT_LANES)],
        out_shape=[jax.ShapeDtypeStruct((b, cls_len, dil * D_A), F32),
                   jax.ShapeDtypeStruct((b, cls_len, dil * STAT_LANES), F32)],
        scratch_shapes=[pltpu.VMEM((D_A // LANES, tiles * tq, LANES), F32),
                        pltpu.VMEM((tiles * tq, STAT_LANES), F32)],
        name="dilated_near",
        compiler_params=_params("parallel", "arbitrary"),
    )(qa, ka_pad, vat_pad, near_bias)
    cls_blk = lambda width: pl.BlockSpec((1, cls_len, width), lambda bi, r: (bi, 0, r))
    cls_kv = pl.BlockSpec((1, cls_pad, D_A), lambda bi, r: (bi, 0, r))
    return pl.pallas_call(
        functools.partial(_dilated_far_body, cls_len=cls_len, bias_index=far_index),
        grid=(b, dil),
        in_specs=[cls_blk(D_A), cls_kv, cls_kv, cls_blk(D_A), cls_blk(STAT_LANES),
                  pl.BlockSpec(far_bias.shape, lambda bi, r: (0, 0, 0))],
        out_specs=cls_blk(D_A),
        out_shape=jax.ShapeDtypeStruct((b, cls_len, dil * D_A), F32),
        name="dilated_far",
        compiler_params=_params("parallel", "parallel"),
    )(qa_cls, ka_cls, va_cls, acc, stats, far_bias)


def _grid_attn_body(qt_ref, k_ref, vt_ref, o_ref, s0_ref, s1_ref, acc_ref, *, tq, tk, nk):
    cols = acc_ref.shape[2]
    vparts = tk // vt_ref.shape[-1]
    for g in range(N_KV_B):
        qt = qt_ref[0, g]

        def scores(kb):
            k0 = pl.multiple_of(kb * tk, tk)
            return jnp.dot(k_ref[0, pl.ds(k0, tk), :], qt, preferred_element_type=F32)

        def consume(s_ref, kb, carry):
            m_prev, l_prev = carry
            s = s_ref[...]
            m_new = jnp.maximum(m_prev, jnp.max(s, axis=0, keepdims=True))
            alpha = jnp.exp2(m_prev - m_new)
            p = jnp.exp2(s - m_new)
            l_new = alpha * l_prev + jnp.sum(p, axis=0, keepdims=True)
            vt = jnp.concatenate([vt_ref[kb * vparts + u, g] for u in range(vparts)], axis=1)
            pv = jnp.dot(vt, p.astype(BF16), preferred_element_type=F32)
            acc_ref[g] = alpha * acc_ref[g] + pv
            return m_new, l_new

        s0_ref[...] = scores(0)
        acc_ref[g] = jnp.zeros(acc_ref.shape[1:], F32)

        def pair(j, m):
            kb = 2 * j
            s1_ref[...] = scores(kb + 1)
            m = consume(s0_ref, kb, m)
            s0_ref[...] = scores(jnp.minimum(kb + 2, nk - 1))
            return consume(s1_ref, kb + 1, m)

        _, l_fin = lax.fori_loop(0, nk // 2, pair, (jnp.full((1, cols), NEG_INF, F32), jnp.zeros((1, cols), F32)))
        acc_ref[g] = acc_ref[g] / l_fin
    ot = acc_ref[...].reshape(N_KV_B * HEAD_DIM, cols).T
    tv = vt_ref.shape[-1]
    for part in range(tq // tv):
        heads = [ot[(part * GQA_GROUP + r) * tv:(part * GQA_GROUP + r + 1) * tv, g * HEAD_DIM:(g + 1) * HEAD_DIM]
                 for g in range(N_KV_B) for r in range(GQA_GROUP)]
        o_ref[0, part * tv:(part + 1) * tv, :] = jnp.concatenate(heads, axis=1).astype(o_ref.dtype)


def _grid_attn(qt, kb, vt, tq, tk):
    b, seq, _ = kb.shape
    cols = GQA_GROUP * tq
    tv = vt.shape[-1]
    nq, nk = seq // tq, seq // tk
    return pl.pallas_call(
        functools.partial(_grid_attn_body, tq=tq, tk=tk, nk=nk),
        grid=(b, nq),
        in_specs=[pl.BlockSpec((1, N_KV_B, 2 * HEAD_DIM, cols), lambda bi, i: (bi * nq + i, 0, 0, 0)),
                  pl.BlockSpec((1, seq, D_KV_B), lambda bi, i: (bi, 0, 0)),
                  pl.BlockSpec((seq // tv, N_KV_B, HEAD_DIM, tv), lambda bi, i: (bi, 0, 0, 0))],
        out_specs=pl.BlockSpec((1, tq, D_B), lambda bi, i: (bi, i, 0)),
        out_shape=jax.ShapeDtypeStruct((b, seq, D_B), BF16),
        scratch_shapes=[pltpu.VMEM((tk, cols), F32), pltpu.VMEM((tk, cols), F32),
                        pltpu.VMEM((N_KV_B, HEAD_DIM, cols), F32)],
        name="grid_attn",
        compiler_params=_params("parallel", "arbitrary"),
    )(qt, kb, vt)


def _fft_tables(seq):
    n1, n2 = seq // FFT_N2, FFT_N2
    k1 = np.arange(n1)
    ang1 = 2.0 * np.pi * ((k1[:, None] * k1[None, :]) % n1) / n1
    f1 = np.concatenate([np.cos(ang1), -np.sin(ang1)], axis=0)
    k = k1[:, None, None] + n1 * np.arange(n2)[None, :, None]
    ang2 = 2.0 * np.pi * ((k * np.arange(n2)[None, None, :]) % seq) / seq
    g = np.concatenate([np.cos(ang2), np.sin(ang2)], axis=-1)
    c = np.arange(HEAD_DIM)
    ang3 = 2.0 * np.pi * ((c[:, None] * c[None, :]) % HEAD_DIM) / HEAD_DIM
    eye = np.eye(N_GROUPS_C)
    dc, ds = np.kron(eye, np.cos(ang3)), np.kron(eye, np.sin(ang3))
    d = np.concatenate([np.concatenate([dc, -ds], axis=1), np.concatenate([ds, dc], axis=1)], axis=0)
    return tuple(jnp.asarray(t, BF16) for t in (f1, g, d))


def _fft1_body(u_ref, f_ref, a_ref):
    a_ref[0] = jnp.dot(f_ref[...], u_ref[0].astype(BF16), preferred_element_type=F32).astype(BF16)


def _fft2_body(a_ref, g_ref, d_ref, o_ref, *, kb1):
    for j in range(kb1):
        ar, ai = a_ref[0, 0, j], a_ref[0, 1, j]
        p = (jnp.dot(ar, d_ref[:D_C, :], preferred_element_type=F32)
             + jnp.dot(ai, d_ref[D_C:, :], preferred_element_type=F32)).astype(BF16)
        gj = g_ref[j]
        o_ref[0, :, j * D_C:(j + 1) * D_C] = (
            jnp.dot(gj[:, :FFT_N2], p[:, :D_C], preferred_element_type=F32)
            + jnp.dot(gj[:, FFT_N2:], p[:, D_C:], preferred_element_type=F32)).astype(o_ref.dtype)


def _fourier(uc, tables):
    b, seq, _ = uc.shape
    f1, g, d = tables
    n1, n2 = seq // FFT_N2, FFT_N2
    cols = n2 * D_C
    tc = 4096
    a = pl.pallas_call(
        _fft1_body,
        grid=(b, cols // tc),
        in_specs=[pl.BlockSpec((1, n1, tc), lambda bi, j: (bi, 0, j)),
                  pl.BlockSpec((2 * n1, n1), lambda bi, j: (0, 0))],
        out_specs=pl.BlockSpec((1, 2 * n1, tc), lambda bi, j: (bi, 0, j)),
        out_shape=jax.ShapeDtypeStruct((b, 2 * n1, cols), BF16),
        name="fft1",
        compiler_params=_params("parallel", "parallel"),
    )(uc.reshape(b, n1, cols), f1)
    kb1 = 8
    z = pl.pallas_call(
        functools.partial(_fft2_body, kb1=kb1),
        grid=(b, n1 // kb1),
        in_specs=[pl.BlockSpec((1, 2, kb1, n2, D_C), lambda bi, j: (bi, 0, j, 0, 0)),
                  pl.BlockSpec((kb1, n2, 2 * n2), lambda bi, j: (j, 0, 0)),
                  pl.BlockSpec((2 * D_C, 2 * D_C), lambda bi, j: (0, 0))],
        out_specs=pl.BlockSpec((1, n2, kb1 * D_C), lambda bi, j: (bi, 0, j)),
        out_shape=jax.ShapeDtypeStruct((b, n2, n1 * D_C), BF16),
        name="fft2",
        compiler_params=_params("parallel", "parallel"),
    )(a.reshape(b, 2, n1, n2, D_C), g, d)
    return z.reshape(b * seq, D_C)


def _outproj_body(oac_ref, ob_ref, oc_ref, x_ref, gh_ref, w_ref, gpost_ref, gffn_ref, m_ref,
                  x1_ref, hn_ref, oa_ref):
    cls_rows = oa_ref.shape[1] // DIL_CLASSES
    for r in range(DIL_CLASSES):
        for c in range(D_A // LANES):
            lo = r * D_A + c * LANES
            oa_ref[c, pl.ds(r, cls_rows, stride=DIL_CLASSES), :] = oac_ref[0, :, lo:lo + LANES]
    oa = jnp.concatenate([oa_ref[c] for c in range(D_A // LANES)], axis=1)
    mix = None
    col = 0
    width = 2 * LANES
    for o_src in (oa, ob_ref, oc_ref):
        for c in range(o_src.shape[-1] // width):
            o = o_src[:, c * width:(c + 1) * width].astype(F32)
            y = o * lax.rsqrt(_head_mean_sq(o, m_ref) + RMS_EPS) * gh_ref[:, col:col + width]
            part = jnp.dot(y.astype(BF16), w_ref[col:col + width, :], preferred_element_type=F32)
            mix = part if mix is None else mix + part
            col += width
    x1 = x_ref[...] + _rms(mix, gpost_ref[...])
    x1_ref[...] = x1
    hn_ref[...] = _rms(x1, gffn_ref[...]).astype(BF16)


def _outproj(oa_cls, ob, oc, x2d, seq, g_heads, w_out, g_post, g_ffn, head_mask, tm):
    n = x2d.shape[0]
    bps = seq // tm
    row = lambda width: pl.BlockSpec((tm, width), lambda i: (i, 0))
    const = lambda shape: pl.BlockSpec(shape, lambda i: (0, 0))
    return pl.pallas_call(
        _outproj_body,
        grid=(n // tm,),
        in_specs=[pl.BlockSpec((1, tm // DIL_CLASSES, DIL_CLASSES * D_A), lambda i: (i // bps, i % bps, 0)),
                  row(D_B), row(D_C), row(D_MODEL), const((1, D_MIX)),
                  const((D_MIX, D_MODEL)), const((1, D_MODEL)), const((1, D_MODEL)),
                  const((2 * LANES, 2 * LANES))],
        out_specs=[row(D_MODEL), row(D_MODEL)],
        out_shape=[jax.ShapeDtypeStruct((n, D_MODEL), F32), jax.ShapeDtypeStruct((n, D_MODEL), BF16)],
        scratch_shapes=[pltpu.VMEM((D_A // LANES, tm, LANES), F32)],
        name="outproj",
        compiler_params=_params("parallel"),
    )(oa_cls, ob, oc, x2d, g_heads, w_out, g_post, g_ffn, head_mask)


def _ffn_body(h_ref, hp_ref, hx_ref, x_ref, wg_ref, wu_ref, wd_ref, cw_ref, cb_ref, gpost_ref,
              o_ref, hext_ref, acc_ref, *, tm, blocks_per_seq):
    i = pl.program_id(0)
    c = pl.program_id(1)
    pad = BF16_ROWS
    rows = tm + 2 * pad

    @pl.when(c == 0)
    def _():
        first = (i % blocks_per_seq) == 0
        last = (i % blocks_per_seq) == blocks_per_seq - 1
        hext_ref[:pad, :] = jnp.where(first, jnp.zeros_like(hp_ref[0]), hp_ref[0])
        hext_ref[pad:pad + tm, :] = h_ref[...]
        hext_ref[pad + tm:, :] = jnp.where(last, jnp.zeros_like(hx_ref[0]), hx_ref[0])
        acc_ref[...] = jnp.zeros(acc_ref.shape, F32)

    g = jnp.dot(hext_ref[...], wg_ref[...], preferred_element_type=F32)
    cw = cw_ref[...]
    g = (pltpu.roll(g, 1, axis=0)[pad:pad + tm] * cw[0:1, :] + g[pad:pad + tm] * cw[1:2, :]
         + pltpu.roll(g, rows - 1, axis=0)[pad:pad + tm] * cw[2:3, :] + cb_ref[...])
    up = jnp.dot(h_ref[...], wu_ref[...], preferred_element_type=F32)
    inner = g * (GELU_C0 + GELU_C1 * (g * g))
    act = (g * (1.0 + jnp.tanh(inner)) * up).astype(BF16)
    acc_ref[...] += jnp.dot(act, wd_ref[...], preferred_element_type=F32)

    @pl.when(c == pl.num_programs(1) - 1)
    def _():
        o_ref[...] = x_ref[...] + _rms(acc_ref[...], gpost_ref[...])


def _ffn(hn, x1, seq, w_gate, w_up, w_down, conv_w, conv_b, g_post, tm, tf):
    n = x1.shape[0]
    nf = D_FF // tf
    blocks_per_seq = seq // tm
    halo_blocks = tm // BF16_ROWS
    n_halo = n // BF16_ROWS
    hn3 = hn.reshape(n_halo, BF16_ROWS, D_MODEL)
    row = lambda: pl.BlockSpec((tm, D_MODEL), lambda i, c: (i, 0))
    prev = pl.BlockSpec((1, BF16_ROWS, D_MODEL), lambda i, c: (jnp.maximum(i * halo_blocks - 1, 0), 0, 0))
    nxt = pl.BlockSpec((1, BF16_ROWS, D_MODEL),
                       lambda i, c: (jnp.minimum((i + 1) * halo_blocks, n_halo - 1), 0, 0))
    cols = lambda rows: pl.BlockSpec((rows, tf), lambda i, c: (0, c))
    return pl.pallas_call(
        functools.partial(_ffn_body, tm=tm, blocks_per_seq=blocks_per_seq),
        grid=(n // tm, nf),
        in_specs=[row(), prev, nxt, row(), cols(D_MODEL), cols(D_MODEL),
                  pl.BlockSpec((tf, D_MODEL), lambda i, c: (c, 0)), cols(3), cols(1),
                  pl.BlockSpec((1, D_MODEL), lambda i, c: (0, 0))],
        out_specs=row(),
        out_shape=jax.ShapeDtypeStruct((n, D_MODEL), F32),
        scratch_shapes=[pltpu.VMEM((tm + 2 * BF16_ROWS, D_MODEL), BF16), pltpu.VMEM((tm, D_MODEL), F32)],
        name="ffn",
        compiler_params=_params("parallel", "arbitrary"),
    )(hn, hn3, hn3, x1, w_gate, w_up, w_down, conv_w, conv_b, g_post)


def _rope_tables(pos, rot_dim, theta):
    half = rot_dim // 2
    inv = jnp.power(jnp.float32(theta), -jnp.arange(half, dtype=jnp.float32) / half)
    ang = pos.astype(jnp.float32)[:, None] * inv[None, :]
    return jnp.cos(ang), jnp.sin(ang)


def _position_tables(seq):
    rows = seq // GRID_W
    t = jnp.arange(seq, dtype=jnp.int32)
    row = jnp.broadcast_to(jnp.arange(rows, dtype=jnp.int32)[:, None], (rows, GRID_W)).reshape(-1)
    col = jnp.broadcast_to(jnp.arange(GRID_W, dtype=jnp.int32)[None, :], (rows, GRID_W)).reshape(-1)
    cos_t, sin_t = _rope_tables(t, ROT_DIM_PARTIAL, ROPE_THETA_PARTIAL)
    cos_r, sin_r = _rope_tables(row, HEAD_DIM // 2, ROPE_THETA_AXIAL)
    cos_c, sin_c = _rope_tables(col, HEAD_DIM // 2, ROPE_THETA_AXIAL)
    rest = HEAD_DIM - ROT_DIM_PARTIAL
    ca = jnp.concatenate([cos_t, cos_t, jnp.ones((seq, rest), F32)], axis=-1)
    sa = jnp.concatenate([-sin_t, sin_t, jnp.zeros((seq, rest), F32)], axis=-1)
    cb = jnp.concatenate([cos_r, cos_r, cos_c, cos_c], axis=-1)
    sb = jnp.concatenate([-sin_r, sin_r, -sin_c, sin_c], axis=-1)
    reps = LANES // HEAD_DIM
    return tuple(jnp.tile(t_, (1, reps)) for t_ in (ca, sa, cb, sb))


def _head_mask():
    lane = np.arange(2 * LANES)
    same = (lane[:, None] // HEAD_DIM) == (lane[None, :] // HEAD_DIM)
    return jnp.asarray(same / HEAD_DIM, BF16)


def _trunk(x, w):
    b, seq, _ = x.shape
    n = b * seq
    tm, tq, tk = TOKEN_TILE, min(GRID_TQ, seq), min(GRID_TK, seq // 2)
    tabs = _position_tables(seq)
    fft_tabs = _fft_tables(seq)
    head_mask = _head_mask()
    x2d = x.reshape(n, D_MODEL)
    depth = w["w_in"].shape[0]
    for l in range(depth):
        tile4 = lambda g_: jnp.tile(g_[l][None, :], (1, 2 * LANES // HEAD_DIM))
        qa, qa_cls, ka_pad, ka_cls, va_cls, vat_pad, qt, kb, vt, uc = _inproj(
            x2d, b, seq, w["g_mix_pre"][l][None], w["w_in"][l], tile4(w["g_q"]), tile4(w["g_k"]),
            tabs, head_mask, tm, tq)
        oa_cls = _dilated(qa.reshape(b, seq, D_A), qa_cls, ka_pad, ka_cls, va_cls, vat_pad)
        ob = _grid_attn(qt, kb.reshape(b, seq, D_KV_B), vt, tq, tk).reshape(n, D_B)
        oc = _fourier(uc.reshape(b, seq, D_C), fft_tabs)
        x1, hn = _outproj(oa_cls, ob, oc, x2d, seq, w["g_heads"][l][None], w["w_out"][l], w["g_mix_post"][l][None],
                          w["g_ffn_pre"][l][None], head_mask, tm)
        x2d = _ffn(hn, x1, seq, w["w_gate"][l], w["w_up"][l], w["w_down"][l], w["conv_w"][l],
                   w["conv_b"][l][None], w["g_ffn_post"][l][None], FFN_TILE, FFN_CHUNK)
    return x2d.reshape(b, seq, D_MODEL)


def _prep_weights(g_mix_pre, g_mix_post, w_in, g_q, g_k, g_heads, w_out, g_ffn_pre, g_ffn_post,
                  w_gate, w_up, conv_w, conv_b, w_down):
    return dict(
        g_mix_pre=g_mix_pre, g_mix_post=g_mix_post, g_q=g_q, g_k=g_k, g_heads=g_heads,
        g_ffn_pre=g_ffn_pre, g_ffn_post=g_ffn_post,
        w_in=w_in.astype(BF16), w_out=w_out.astype(BF16),
        w_gate=w_gate.astype(BF16), w_up=(0.5 * w_up).astype(BF16), w_down=w_down.astype(BF16),
        conv_w=conv_w, conv_b=conv_b)


def kernel(x_prompt, x_sample, g_mix_pre, g_mix_post, w_in, g_q, g_k, g_heads, w_out, g_ffn_pre, g_ffn_post,
           w_gate, w_up, conv_w, conv_b, w_down):
    w = _prep_weights(g_mix_pre, g_mix_post, w_in, g_q, g_k, g_heads, w_out, g_ffn_pre, g_ffn_post,
                      w_gate, w_up, conv_w, conv_b, w_down)
    return (_trunk(x_prompt, w), _trunk(x_sample, w))
```

```python
import functools

import numpy as np
import jax
import jax.numpy as jnp
from jax import lax
from jax.experimental import pallas as pl
from jax.experimental.pallas import tpu as pltpu

F32 = jnp.float32
BF16 = jnp.bfloat16

D_MODEL = 1024
HEAD_DIM = 64
N_HEADS_A = 4
DILATED_PATTERNS = ((128, 1), (512, 4), (2048, 16))
N_HEADS_B = 8
N_KV_B = 2
GQA_GROUP = N_HEADS_B // N_KV_B
N_GROUPS_C = 4
D_A = N_HEADS_A * HEAD_DIM
D_B = N_HEADS_B * HEAD_DIM
D_KV_B = N_KV_B * HEAD_DIM
D_C = N_GROUPS_C * HEAD_DIM
D_MIX = D_A + D_B + D_C
D_IN = 3 * D_A + D_B + 2 * D_KV_B + D_C
D_FF = 4 * D_MODEL
ROPE_THETA_PARTIAL = 500000.0
ROT_DIM_PARTIAL = HEAD_DIM // 4
ROPE_THETA_AXIAL = 10000.0
GRID_W = 64
RMS_EPS = 1e-6
NEG_INF = -1e30
Q_SCALE = HEAD_DIM ** -0.5 * float(np.log2(np.e))

LANES = 128
BF16_ROWS = 16
DIL_HALO = max(w // 2 for w, _ in DILATED_PATTERNS)
DIL_TQ = 128
DIL_FAR = DILATED_PATTERNS[-1]
DIL_CLASSES = DIL_FAR[1]
DIL_NEAR_TILES = 8
FFT_N2 = 128
VMEM_LIMIT = 56 * 1024 * 1024
TOKEN_TILE = 1024
GRID_TQ = 1024
GRID_TK = 512
FFN_TILE = 1024
FFN_CHUNK = 1024
GELU_C0 = float(np.sqrt(2.0 / np.pi))
GELU_C1 = 0.044715 * GELU_C0


def _params(*sem):
    return pltpu.CompilerParams(dimension_semantics=sem, vmem_limit_bytes=VMEM_LIMIT)


def _rms(x, g):
    return x * lax.rsqrt(jnp.mean(x * x, axis=-1, keepdims=True) + RMS_EPS) * g


def _head_mean_sq(x, m_ref):
    return jnp.dot((x * x).astype(BF16), m_ref[...], preferred_element_type=F32)


def _rotate_half(x, half):
    n = x.shape[-1]
    lane = lax.broadcasted_iota(jnp.int32, (1, n), 1)
    fwd = pltpu.roll(x, n - half, axis=1)
    bwd = pltpu.roll(x, half, axis=1)
    return jnp.where((lane % (2 * half)) < half, fwd, bwd)


def _inproj_body(x_ref, g_ref, w_ref, gq_ref, gk_ref, ca_ref, sa_ref, cb_ref, sb_ref, m_ref,
                 ka_zero, kac_zero, vac_zero, vat_zero,
                 qa_ref, qac_ref, ka_ref, kac_ref, vac_ref, vat_ref, qt_ref, kb_ref, vt_ref, uc_ref,
                 rel_ref):
    del ka_zero, kac_zero, vac_zero, vat_zero
    tm = x_ref.shape[0]
    h = _rms(x_ref[...], g_ref[...]).astype(BF16)

    def proj(lo, width):
        return jnp.dot(h, w_ref[:, lo:lo + width], preferred_element_type=F32)

    ca, sa, cb, sb = ca_ref[...], sa_ref[...], cb_ref[...], sb_ref[...]

    def rope_a(p):
        return p * ca + _rotate_half(p, ROT_DIM_PARTIAL // 2) * sa

    def norm_b(p, gain):
        return p * lax.rsqrt(_head_mean_sq(p, m_ref) + RMS_EPS) * gain

    def rope_b(y):
        return y * cb + _rotate_half(y, HEAD_DIM // 4) * sb

    def to_classes(val, out_ref):
        for c in range(D_A // LANES):
            rel_ref[c] = val[:, c * LANES:(c + 1) * LANES]
        for r in range(DIL_CLASSES):
            for c in range(D_A // LANES):
                lo = r * D_A + c * LANES
                out_ref[0, :, lo:lo + LANES] = rel_ref[c, pl.ds(r, tm // DIL_CLASSES, stride=DIL_CLASSES), :].astype(BF16)

    pq = proj(0, D_A)
    qa = jnp.concatenate([rope_a(pq[:, c * LANES:(c + 1) * LANES]) for c in range(D_A // LANES)], axis=1) * Q_SCALE
    qa_ref[...] = qa.astype(BF16)
    to_classes(qa, qac_ref)
    pk = proj(D_A, D_A)
    ka = jnp.concatenate([rope_a(pk[:, c * LANES:(c + 1) * LANES]) for c in range(D_A // LANES)], axis=1)
    ka_ref[0] = ka.astype(BF16)
    to_classes(ka, kac_ref)
    pv = proj(2 * D_A, D_A)
    to_classes(pv, vac_ref)
    vat_ref[0] = pv.T.astype(BF16)

    gq, gk = gq_ref[...], gk_ref[...]
    zeros = jnp.zeros((HEAD_DIM, GQA_GROUP * tm), BF16)
    for g in range(N_KV_B):
        y = norm_b(proj(3 * D_A + g * 2 * LANES, 2 * LANES), gq)
        for c in range(2):
            yt = (rope_b(y[:, c * LANES:(c + 1) * LANES]) * Q_SCALE).T.astype(BF16)
            for hh in range(LANES // HEAD_DIM):
                r = c * (LANES // HEAD_DIM) + hh
                qt_ref[0, g, g * HEAD_DIM:(g + 1) * HEAD_DIM, r * tm:(r + 1) * tm] = yt[hh * HEAD_DIM:(hh + 1) * HEAD_DIM, :]
        qt_ref[0, g, (1 - g) * HEAD_DIM:(2 - g) * HEAD_DIM, :] = zeros
    pkv = proj(3 * D_A + D_B, 2 * D_KV_B)
    kb_ref[...] = rope_b(norm_b(pkv, gk)[:, :D_KV_B]).astype(BF16)
    vbt = pkv[:, D_KV_B:].T.astype(BF16)
    for g in range(N_KV_B):
        vt_ref[0, g] = vbt[g * HEAD_DIM:(g + 1) * HEAD_DIM, :]
    uc_ref[...] = proj(3 * D_A + D_B + 2 * D_KV_B, D_C)


def _inproj(x2d, batch, seq, g_pre, w_in, gq, gk, tabs, head_mask, tm, tq):
    n = x2d.shape[0]
    bps = seq // tm
    halo_blocks = DIL_HALO // tm
    seq_pad = seq + 2 * DIL_HALO
    parts = tq // tm
    row = lambda width: pl.BlockSpec((tm, width), lambda i: (i, 0))
    const = lambda shape: pl.BlockSpec(shape, lambda i: (0, 0))
    tab = pl.BlockSpec((tm, LANES), lambda i: (i % bps, 0))
    pad_rows = pl.BlockSpec((1, tm, D_A), lambda i: (i // bps, i % bps + halo_blocks, 0))
    pad_cols = pl.BlockSpec((1, D_A, tm), lambda i: (i // bps, 0, i % bps + halo_blocks))
    cls_rows, cls_width = tm // DIL_CLASSES, DIL_CLASSES * D_A
    cls = pl.BlockSpec((1, cls_rows, cls_width), lambda i: (i // bps, i % bps, 0))
    cls_pad = pl.BlockSpec((1, cls_rows, cls_width), lambda i: (i // bps, i % bps + halo_blocks, 0))
    out_specs = [row(D_A), cls, pad_rows, cls_pad, cls_pad, pad_cols,
                 pl.BlockSpec((1, N_KV_B, 2 * HEAD_DIM, GQA_GROUP * tm), lambda i: (i // parts, 0, 0, i % parts)),
                 row(D_KV_B),
                 pl.BlockSpec((1, N_KV_B, HEAD_DIM, tm), lambda i: (i, 0, 0, 0)),
                 row(D_C)]
    out_shape = [jax.ShapeDtypeStruct((n, D_A), BF16),
                 jax.ShapeDtypeStruct((batch, seq // DIL_CLASSES, cls_width), BF16),
                 jax.ShapeDtypeStruct((batch, seq_pad, D_A), BF16),
                 jax.ShapeDtypeStruct((batch, seq_pad // DIL_CLASSES, cls_width), BF16),
                 jax.ShapeDtypeStruct((batch, seq_pad // DIL_CLASSES, cls_width), BF16),
                 jax.ShapeDtypeStruct((batch, D_A, seq_pad), BF16),
                 jax.ShapeDtypeStruct((n // tq, N_KV_B, 2 * HEAD_DIM, GQA_GROUP * tq), BF16),
                 jax.ShapeDtypeStruct((n, D_KV_B), BF16),
                 jax.ShapeDtypeStruct((n // tm, N_KV_B, HEAD_DIM, tm), BF16),
                 jax.ShapeDtypeStruct((n, D_C), F32)]
    padded = (2, 3, 4, 5)
    zero_bufs = [jnp.zeros(out_shape[k].shape, BF16) for k in padded]
    n_in = 10
    return pl.pallas_call(
        _inproj_body,
        grid=(n // tm,),
        in_specs=[row(D_MODEL), const((1, D_MODEL)), const((D_MODEL, D_IN)),
                  const((1, 2 * LANES)), const((1, 2 * LANES)), tab, tab, tab, tab,
                  const((2 * LANES, 2 * LANES))] + [pl.BlockSpec(memory_space=pl.ANY)] * len(padded),
        out_specs=out_specs,
        out_shape=out_shape,
        input_output_aliases={n_in + k: o for k, o in enumerate(padded)},
        scratch_shapes=[pltpu.VMEM((D_A // LANES, tm, LANES), F32)],
        name="inproj",
        compiler_params=_params("parallel"),
    )(x2d, g_pre, w_in, gq, gk, *tabs, head_mask, *zero_bufs)


DIL_NEAR = DILATED_PATTERNS[:-1]
NEAR_HALO = max(w // 2 for w, _ in DIL_NEAR)
FAR_HALO = DIL_FAR[0] // (2 * DIL_FAR[1])
STAT_LANES = LANES


def _band_bias(tq, halo, patterns):
    delta = np.arange(tq + 2 * halo)[None, :] - halo - np.arange(tq)[:, None]
    mult = np.zeros(delta.shape, np.int64)
    for window, dil in patterns:
        mult += (np.abs(delta) <= (window // (2 * dil)) * dil) & (delta % dil == 0)
    return np.where(mult > 0, np.log2(np.maximum(mult, 1)), NEG_INF).astype(np.float32)


def _tile_bias_tables(tq, halo, patterns, length):
    base = _band_bias(tq, halo, patterns).T
    n_tiles, edge = length // tq, -(-halo // tq)
    compress = n_tiles > 2 * edge + 1
    tiles = list(range(edge + 1)) + list(range(n_tiles - edge, n_tiles)) if compress else list(range(n_tiles))
    tabs = []
    for t in tiles:
        kpos = t * tq - halo + np.arange(tq + 2 * halo)
        inside = (kpos >= 0) & (kpos < length)
        tabs.append(np.where(inside[:, None], base, NEG_INF))
    tables = jnp.asarray(np.tile(np.stack(tabs), (1, 1, N_HEADS_A)), F32)

    def index(t):
        if not compress:
            return t
        if isinstance(t, int):
            return t if t < edge else (t - (n_tiles - 2 * edge - 1) if t >= n_tiles - edge else edge)
        return jnp.where(t < edge, t, jnp.where(t >= n_tiles - edge, t - (n_tiles - 2 * edge - 1), edge))

    return tables, index


def _band_attention_t(q, kw, vtw, bias_t):
    tq = q.shape[0]
    qt = q.astype(F32).T.astype(BF16)
    rows = lax.broadcasted_iota(jnp.int32, (D_A, N_HEADS_A * tq), 0) // HEAD_DIM
    cols = lax.broadcasted_iota(jnp.int32, (D_A, N_HEADS_A * tq), 1) // tq
    qbd = jnp.where(rows == cols, jnp.concatenate([qt] * N_HEADS_A, axis=1), jnp.zeros((), BF16))
    st = jnp.dot(kw, qbd, preferred_element_type=F32)
    st = st + bias_t
    m = jnp.max(st, axis=0, keepdims=True)
    p = jnp.exp2(st - m)
    l = jnp.sum(p, axis=0, keepdims=True)
    ot_all = jnp.dot(vtw, p.astype(BF16), preferred_element_type=F32)
    ot = jnp.concatenate([ot_all[h * HEAD_DIM:(h + 1) * HEAD_DIM, h * tq:(h + 1) * tq]
                          for h in range(N_HEADS_A)], axis=0)
    return ot, m, l


def _dilated_near_body(q_ref, k_ref, vt_ref, bias_ref, acc_ref, stat_ref, acc_nat, stat_nat, *, tiles, bias_index):
    tq = DIL_TQ
    win = tq + 2 * NEAR_HALO
    for t in range(tiles):
        tile = pl.program_id(1) * tiles + t
        start = pl.multiple_of(tile * tq + (DIL_HALO - NEAR_HALO), tq)
        ot, m, l = _band_attention_t(q_ref[0, t * tq:(t + 1) * tq, :], k_ref[0, pl.ds(start, win), :],
                                     vt_ref[0, :, pl.ds(start, win)], bias_ref[bias_index(tile)])
        o = ot.T
        for c in range(D_A // LANES):
            acc_nat[c, t * tq:(t + 1) * tq, :] = o[:, c * LANES:(c + 1) * LANES]
        stat_rows = ([m[:, h * tq:(h + 1) * tq] for h in range(N_HEADS_A)]
                     + [l[:, h * tq:(h + 1) * tq] for h in range(N_HEADS_A)]
                     + [jnp.zeros((STAT_LANES - 2 * N_HEADS_A, tq), F32)])
        stat_nat[t * tq:(t + 1) * tq, :] = jnp.concatenate(stat_rows, axis=0).T
    cls_rows = tiles * tq // DIL_CLASSES
    for r in range(DIL_CLASSES):
        for c in range(D_A // LANES):
            lo = r * D_A + c * LANES
            acc_ref[0, :, lo:lo + LANES] = acc_nat[c, pl.ds(r, cls_rows, stride=DIL_CLASSES), :]
        stat_ref[0, :, r * STAT_LANES:(r + 1) * STAT_LANES] = stat_nat[pl.ds(r, cls_rows, stride=DIL_CLASSES), :]


def _dilated_far_body(q_ref, k_ref, v_ref, acc_ref, stat_ref, bias_ref, o_ref, *, cls_len, bias_index):
    tq = DIL_TQ
    win = tq + 2 * FAR_HALO
    vt = v_ref[0].astype(F32).T.astype(BF16)
    for t in range(cls_len // tq):
        rs = slice(t * tq, (t + 1) * tq)
        ot2, m2, l2 = _band_attention_t(q_ref[0, rs, :], k_ref[0, t * tq:t * tq + win, :],
                                        vt[:, t * tq:t * tq + win], bias_ref[bias_index(t)])
        ot1 = acc_ref[0, rs, :].T
        st1 = stat_ref[0, rs, :].T
        outs = []
        for h in range(N_HEADS_A):
            hs = slice(h * HEAD_DIM, (h + 1) * HEAD_DIM)
            cs = slice(h * tq, (h + 1) * tq)
            m1, l1 = st1[h:h + 1, :], st1[N_HEADS_A + h:N_HEADS_A + h + 1, :]
            m = jnp.maximum(m1, m2[:, cs])
            w1, w2 = jnp.exp2(m1 - m), jnp.exp2(m2[:, cs] - m)
            outs.append((w1 * ot1[hs, :] + w2 * ot2[hs, :]) / (w1 * l1 + w2 * l2[:, cs]))
        o_ref[0, rs, :] = jnp.concatenate(outs, axis=0).T


def _dilated(qa, qa_cls, ka_pad, ka_cls, va_cls, vat_pad):
    b, seq, _ = qa.shape
    dil = DIL_CLASSES
    cls_len = seq // dil
    tq = DIL_TQ
    tiles = DIL_NEAR_TILES
    seq_pad = seq + 2 * DIL_HALO
    cls_pad = cls_len + 2 * FAR_HALO
    step_cls = tiles * tq // dil
    near_bias, near_index = _tile_bias_tables(tq, NEAR_HALO, DIL_NEAR, seq)
    far_bias, far_index = _tile_bias_tables(tq, FAR_HALO, ((2 * FAR_HALO, 1),), cls_len)
    step_blk = lambda width: pl.BlockSpec((1, step_cls, dil * width), lambda bi, i: (bi, i, 0))
    acc, stats = pl.pallas_call(
        functools.partial(_dilated_near_body, tiles=tiles, bias_index=near_index),
        grid=(b, seq // (tiles * tq)),
        in_specs=[pl.BlockSpec((1, tiles * tq, D_A), lambda bi, i: (bi, i, 0)),
                  pl.BlockSpec((1, seq_pad, D_A), lambda bi, i: (bi, 0, 0)),
                  pl.BlockSpec((1, D_A, seq_pad), lambda bi, i: (bi, 0, 0)),
                  pl.BlockSpec(near_bias.shape, lambda bi, i: (0, 0, 0))],
        out_specs=[step_blk(D_A), step_blk(STAT_LANES)],
        out_shape=[jax.ShapeDtypeStruct((b, cls_len, dil * D_A), F32),
                   jax.ShapeDtypeStruct((b, cls_len, dil * STAT_LANES), F32)],
        scratch_shapes=[pltpu.VMEM((D_A // LANES, tiles * tq, LANES), F32),
                        pltpu.VMEM((tiles * tq, STAT_LANES), F32)],
        name="dilated_near",
        compiler_params=_params("parallel", "arbitrary"),
    )(qa, ka_pad, vat_pad, near_bias)
    cls_blk = lambda width: pl.BlockSpec((1, cls_len, width), lambda bi, r: (bi, 0, r))
    cls_kv = pl.BlockSpec((1, cls_pad, D_A), lambda bi, r: (bi, 0, r))
    return pl.pallas_call(
        functools.partial(_dilated_far_body, cls_len=cls_len, bias_index=far_index),
        grid=(b, dil),
        in_specs=[cls_blk(D_A), cls_kv, cls_kv, cls_blk(D_A), cls_blk(STAT_LANES),
                  pl.BlockSpec(far_bias.shape, lambda bi, r: (0, 0, 0))],
        out_specs=cls_blk(D_A),
        out_shape=jax.ShapeDtypeStruct((b, cls_len, dil * D_A), F32),
        name="dilated_far",
        compiler_params=_params("parallel", "parallel"),
    )(qa_cls, ka_cls, va_cls, acc, stats, far_bias)


def _grid_attn_body(qt_ref, k_ref, vt_ref, o_ref, s0_ref, s1_ref, acc_ref, *, tq, tk, nk):
    cols = acc_ref.shape[2]
    vparts = tk // vt_ref.shape[-1]
    vsplit = vt_ref.shape[-1] // tk
    for g in range(N_KV_B):
        qt = qt_ref[0, g]

        def scores(kb):
            k0 = pl.multiple_of(kb * tk, tk)
            return jnp.dot(k_ref[0, pl.ds(k0, tk), :], qt, preferred_element_type=F32)

        def consume(s_ref, kb, carry):
            m_prev, l_prev = carry
            s = s_ref[...]
            m_new = jnp.maximum(m_prev, jnp.max(s, axis=0, keepdims=True))
            alpha = jnp.exp2(m_prev - m_new)
            p = jnp.exp2(s - m_new)
            l_new = alpha * l_prev + jnp.sum(p, axis=0, keepdims=True)
            if vparts:
                vt = jnp.concatenate([vt_ref[kb * vparts + u, g] for u in range(vparts)], axis=1)
            else:
                off = pl.multiple_of((kb % vsplit) * tk, tk)
                vt = vt_ref[kb // vsplit, g, :, pl.ds(off, tk)]
            pv = jnp.dot(vt, p.astype(BF16), preferred_element_type=F32)
            acc_ref[g] = alpha * acc_ref[g] + pv
            return m_new, l_new

        s0_ref[...] = scores(0)
        acc_ref[g] = jnp.zeros(acc_ref.shape[1:], F32)

        def pair(j, m):
            kb = 2 * j
            s1_ref[...] = scores(kb + 1)
            m = consume(s0_ref, kb, m)
            s0_ref[...] = scores(jnp.minimum(kb + 2, nk - 1))
            return consume(s1_ref, kb + 1, m)

        _, l_fin = lax.fori_loop(0, nk // 2, pair, (jnp.full((1, cols), NEG_INF, F32), jnp.zeros((1, cols), F32)))
        acc_ref[g] = acc_ref[g] / l_fin
    ot = acc_ref[...].reshape(N_KV_B * HEAD_DIM, cols).T
    tv = vt_ref.shape[-1]
    for part in range(tq // tv):
        heads = [ot[(part * GQA_GROUP + r) * tv:(part * GQA_GROUP + r + 1) * tv, g * HEAD_DIM:(g + 1) * HEAD_DIM]
                 for g in range(N_KV_B) for r in range(GQA_GROUP)]
        o_ref[0, part * tv:(part + 1) * tv, :] = jnp.concatenate(heads, axis=1).astype(o_ref.dtype)


def _grid_attn(qt, kb, vt, tq, tk):
    b, seq, _ = kb.shape
    cols = GQA_GROUP * tq
    tv = vt.shape[-1]
    nq, nk = seq // tq, seq // tk
    return pl.pallas_call(
        functools.partial(_grid_attn_body, tq=tq, tk=tk, nk=nk),
        grid=(b, nq),
        in_specs=[pl.BlockSpec((1, N_KV_B, 2 * HEAD_DIM, cols), lambda bi, i: (bi * nq + i, 0, 0, 0)),
                  pl.BlockSpec((1, seq, D_KV_B), lambda bi, i: (bi, 0, 0)),
                  pl.BlockSpec((seq // tv, N_KV_B, HEAD_DIM, tv), lambda bi, i: (bi, 0, 0, 0))],
        out_specs=pl.BlockSpec((1, tq, D_B), lambda bi, i: (bi, i, 0)),
        out_shape=jax.ShapeDtypeStruct((b, seq, D_B), BF16),
        scratch_shapes=[pltpu.VMEM((tk, cols), F32), pltpu.VMEM((tk, cols), F32),
                        pltpu.VMEM((N_KV_B, HEAD_DIM, cols), F32)],
        name="grid_attn",
        compiler_params=_params("parallel", "arbitrary"),
    )(qt, kb, vt)


def _fft_tables(seq):
    n1, n2 = seq // FFT_N2, FFT_N2
    k1 = np.arange(n1)
    ang1 = 2.0 * np.pi * ((k1[:, None] * k1[None, :]) % n1) / n1
    f1 = np.concatenate([np.cos(ang1), -np.sin(ang1)], axis=0)
    k = k1[:, None, None] + n1 * np.arange(n2)[None, :, None]
    ang2 = 2.0 * np.pi * ((k * np.arange(n2)[None, None, :]) % seq) / seq
    g = np.concatenate([np.cos(ang2), np.sin(ang2)], axis=-1)
    c = np.arange(HEAD_DIM)
    ang3 = 2.0 * np.pi * ((c[:, None] * c[None, :]) % HEAD_DIM) / HEAD_DIM
    eye = np.eye(N_GROUPS_C)
    dc, ds = np.kron(eye, np.cos(ang3)), np.kron(eye, np.sin(ang3))
    d = np.concatenate([np.concatenate([dc, -ds], axis=1), np.concatenate([ds, dc], axis=1)], axis=0)
    return tuple(jnp.asarray(t, BF16) for t in (f1, g, d))


def _fft1_body(u_ref, f_ref, a_ref):
    a_ref[0] = jnp.dot(f_ref[...], u_ref[0].astype(BF16), preferred_element_type=F32).astype(BF16)


def _fft2_body(a_ref, g_ref, d_ref, o_ref, *, kb1):
    for j in range(kb1):
        ar, ai = a_ref[0, 0, j], a_ref[0, 1, j]
        p = (jnp.dot(ar, d_ref[:D_C, :], preferred_element_type=F32)
             + jnp.dot(ai, d_ref[D_C:, :], preferred_element_type=F32)).astype(BF16)
        gj = g_ref[j]
        o_ref[0, :, j * D_C:(j + 1) * D_C] = (
            jnp.dot(gj[:, :FFT_N2], p[:, :D_C], preferred_element_type=F32)
            + jnp.dot(gj[:, FFT_N2:], p[:, D_C:], preferred_element_type=F32)).astype(o_ref.dtype)


def _fourier(uc, tables):
    b, seq, _ = uc.shape
    f1, g, d = tables
    n1, n2 = seq // FFT_N2, FFT_N2
    cols = n2 * D_C
    tc = 4096
    a = pl.pallas_call(
        _fft1_body,
        grid=(b, cols // tc),
        in_specs=[pl.BlockSpec((1, n1, tc), lambda bi, j: (bi, 0, j)),
                  pl.BlockSpec((2 * n1, n1), lambda bi, j: (0, 0))],
        out_specs=pl.BlockSpec((1, 2 * n1, tc), lambda bi, j: (bi, 0, j)),
        out_shape=jax.ShapeDtypeStruct((b, 2 * n1, cols), BF16),
        name="fft1",
        compiler_params=_params("parallel", "parallel"),
    )(uc.reshape(b, n1, cols), f1)
    kb1 = 8
    z = pl.pallas_call(
        functools.partial(_fft2_body, kb1=kb1),
        grid=(b, n1 // kb1),
        in_specs=[pl.BlockSpec((1, 2, kb1, n2, D_C), lambda bi, j: (bi, 0, j, 0, 0)),
                  pl.BlockSpec((kb1, n2, 2 * n2), lambda bi, j: (j, 0, 0)),
                  pl.BlockSpec((2 * D_C, 2 * D_C), lambda bi, j: (0, 0))],
        out_specs=pl.BlockSpec((1, n2, kb1 * D_C), lambda bi, j: (bi, 0, j)),
        out_shape=jax.ShapeDtypeStruct((b, n2, n1 * D_C), BF16),
        name="fft2",
        compiler_params=_params("parallel", "parallel"),
    )(a.reshape(b, 2, n1, n2, D_C), g, d)
    return z.reshape(b * seq, D_C)


def _outproj_body(oac_ref, ob_ref, oc_ref, x_ref, gh_ref, w_ref, gpost_ref, gffn_ref, m_ref,
                  x1_ref, hn_ref, oa_ref):
    cls_rows = oa_ref.shape[1] // DIL_CLASSES
    for r in range(DIL_CLASSES):
        for c in range(D_A // LANES):
            lo = r * D_A + c * LANES
            oa_ref[c, pl.ds(r, cls_rows, stride=DIL_CLASSES), :] = oac_ref[0, :, lo:lo + LANES]
    oa = jnp.concatenate([oa_ref[c] for c in range(D_A // LANES)], axis=1)
    mix = None
    col = 0
    width = 2 * LANES
    for o_src in (oa, ob_ref, oc_ref):
        for c in range(o_src.shape[-1] // width):
            o = o_src[:, c * width:(c + 1) * width].astype(F32)
            y = o * lax.rsqrt(_head_mean_sq(o, m_ref) + RMS_EPS) * gh_ref[:, col:col + width]
            part = jnp.dot(y.astype(BF16), w_ref[col:col + width, :], preferred_element_type=F32)
            mix = part if mix is None else mix + part
            col += width
    x1 = x_ref[...] + _rms(mix, gpost_ref[...])
    x1_ref[...] = x1
    hn_ref[...] = _rms(x1, gffn_ref[...]).astype(BF16)


def _outproj(oa_cls, ob, oc, x2d, seq, g_heads, w_out, g_post, g_ffn, head_mask, tm):
    n = x2d.shape[0]
    bps = seq // tm
    row = lambda width: pl.BlockSpec((tm, width), lambda i: (i, 0))
    const = lambda shape: pl.BlockSpec(shape, lambda i: (0, 0))
    return pl.pallas_call(
        _outproj_body,
        grid=(n // tm,),
        in_specs=[pl.BlockSpec((1, tm // DIL_CLASSES, DIL_CLASSES * D_A), lambda i: (i // bps, i % bps, 0)),
                  row(D_B), row(D_C), row(D_MODEL), const((1, D_MIX)),
                  const((D_MIX, D_MODEL)), const((1, D_MODEL)), const((1, D_MODEL)),
                  const((2 * LANES, 2 * LANES))],
        out_specs=[row(D_MODEL), row(D_MODEL)],
        out_shape=[jax.ShapeDtypeStruct((n, D_MODEL), F32), jax.ShapeDtypeStruct((n, D_MODEL), BF16)],
        scratch_shapes=[pltpu.VMEM((D_A // LANES, tm, LANES), F32)],
        name="outproj",
        compiler_params=_params("parallel"),
    )(oa_cls, ob, oc, x2d, g_heads, w_out, g_post, g_ffn, head_mask)


def _ffn_body(h_ref, hp_ref, hx_ref, x_ref, wg_ref, wu_ref, wd_ref, cw_ref, cb_ref, gpost_ref,
              o_ref, hext_ref, acc_ref, *, tm, blocks_per_seq):
    i = pl.program_id(0)
    c = pl.program_id(1)
    pad = BF16_ROWS
    rows = tm + 2 * pad

    @pl.when(c == 0)
    def _():
        first = (i % blocks_per_seq) == 0
        last = (i % blocks_per_seq) == blocks_per_seq - 1
        hext_ref[:pad, :] = jnp.where(first, jnp.zeros_like(hp_ref[0]), hp_ref[0])
        hext_ref[pad:pad + tm, :] = h_ref[...]
        hext_ref[pad + tm:, :] = jnp.where(last, jnp.zeros_like(hx_ref[0]), hx_ref[0])
        acc_ref[...] = jnp.zeros(acc_ref.shape, F32)

    g = jnp.dot(hext_ref[...], wg_ref[...], preferred_element_type=F32)
    cw = cw_ref[...]
    g = (pltpu.roll(g, 1, axis=0)[pad:pad + tm] * cw[0:1, :] + g[pad:pad + tm] * cw[1:2, :]
         + pltpu.roll(g, rows - 1, axis=0)[pad:pad + tm] * cw[2:3, :] + cb_ref[...])
    up = jnp.dot(h_ref[...], wu_ref[...], preferred_element_type=F32)
    inner = g * (GELU_C0 + GELU_C1 * (g * g))
    act = (g * (1.0 + jnp.tanh(inner)) * up).astype(BF16)
    acc_ref[...] += jnp.dot(act, wd_ref[...], preferred_element_type=F32)

    @pl.when(c == pl.num_programs(1) - 1)
    def _():
        o_ref[...] = x_ref[...] + _rms(acc_ref[...], gpost_ref[...])


def _ffn(hn, x1, seq, w_gate, w_up, w_down, conv_w, conv_b, g_post, tm, tf):
    n = x1.shape[0]
    nf = D_FF // tf
    blocks_per_seq = seq // tm
    halo_blocks = tm // BF16_ROWS
    n_halo = n // BF16_ROWS
    hn3 = hn.reshape(n_halo, BF16_ROWS, D_MODEL)
    row = lambda: pl.BlockSpec((tm, D_MODEL), lambda i, c: (i, 0))
    prev = pl.BlockSpec((1, BF16_ROWS, D_MODEL), lambda i, c: (jnp.maximum(i * halo_blocks - 1, 0), 0, 0))
    nxt = pl.BlockSpec((1, BF16_ROWS, D_MODEL),
                       lambda i, c: (jnp.minimum((i + 1) * halo_blocks, n_halo - 1), 0, 0))
    cols = lambda rows: pl.BlockSpec((rows, tf), lambda i, c: (0, c))
    return pl.pallas_call(
        functools.partial(_ffn_body, tm=tm, blocks_per_seq=blocks_per_seq),
        grid=(n // tm, nf),
        in_specs=[row(), prev, nxt, row(), cols(D_MODEL), cols(D_MODEL),
                  pl.BlockSpec((tf, D_MODEL), lambda i, c: (c, 0)), cols(3), cols(1),
                  pl.BlockSpec((1, D_MODEL), lambda i, c: (0, 0))],
        out_specs=row(),
        out_shape=jax.ShapeDtypeStruct((n, D_MODEL), F32),
        scratch_shapes=[pltpu.VMEM((tm + 2 * BF16_ROWS, D_MODEL), BF16), pltpu.VMEM((tm, D_MODEL), F32)],
        name="ffn",
        compiler_params=_params("parallel", "arbitrary"),
    )(hn, hn3, hn3, x1, w_gate, w_up, w_down, conv_w, conv_b, g_post)


def _rope_tables(pos, rot_dim, theta):
    half = rot_dim // 2
    inv = jnp.power(jnp.float32(theta), -jnp.arange(half, dtype=jnp.float32) / half)
    ang = pos.astype(jnp.float32)[:, None] * inv[None, :]
    return jnp.cos(ang), jnp.sin(ang)


def _position_tables(seq):
    rows = seq // GRID_W
    t = jnp.arange(seq, dtype=jnp.int32)
    row = jnp.broadcast_to(jnp.arange(rows, dtype=jnp.int32)[:, None], (rows, GRID_W)).reshape(-1)
    col = jnp.broadcast_to(jnp.arange(GRID_W, dtype=jnp.int32)[None, :], (rows, GRID_W)).reshape(-1)
    cos_t, sin_t = _rope_tables(t, ROT_DIM_PARTIAL, ROPE_THETA_PARTIAL)
    cos_r, sin_r = _rope_tables(row, HEAD_DIM // 2, ROPE_THETA_AXIAL)
    cos_c, sin_c = _rope_tables(col, HEAD_DIM // 2, ROPE_THETA_AXIAL)
    rest = HEAD_DIM - ROT_DIM_PARTIAL
    ca = jnp.concatenate([cos_t, cos_t, jnp.ones((seq, rest), F32)], axis=-1)
    sa = jnp.concatenate([-sin_t, sin_t, jnp.zeros((seq, rest), F32)], axis=-1)
    cb = jnp.concatenate([cos_r, cos_r, cos_c, cos_c], axis=-1)
    sb = jnp.concatenate([-sin_r, sin_r, -sin_c, sin_c], axis=-1)
    reps = LANES // HEAD_DIM
    return tuple(jnp.tile(t_, (1, reps)) for t_ in (ca, sa, cb, sb))


def _head_mask():
    lane = np.arange(2 * LANES)
    same = (lane[:, None] // HEAD_DIM) == (lane[None, :] // HEAD_DIM)
    return jnp.asarray(same / HEAD_DIM, BF16)


def _trunk(x, w):
    b, seq, _ = x.shape
    n = b * seq
    tm, tq, tk = TOKEN_TILE, min(GRID_TQ, seq), min(GRID_TK, seq // 2)
    tabs = _position_tables(seq)
    fft_tabs = _fft_tables(seq)
    head_mask = _head_mask()
    x2d = x.reshape(n, D_MODEL)
    depth = w["w_in"].shape[0]
    for l in range(depth):
        tile4 = lambda g_: jnp.tile(g_[l][None, :], (1, 2 * LANES // HEAD_DIM))
        qa, qa_cls, ka_pad, ka_cls, va_cls, vat_pad, qt, kb, vt, uc = _inproj(
            x2d, b, seq, w["g_mix_pre"][l][None], w["w_in"][l], tile4(w["g_q"]), tile4(w["g_k"]),
            tabs, head_mask, tm, tq)
        oa_cls = _dilated(qa.reshape(b, seq, D_A), qa_cls, ka_pad, ka_cls, va_cls, vat_pad)
        ob = _grid_attn(qt, kb.reshape(b, seq, D_KV_B), vt, tq, tk).reshape(n, D_B)
        oc = _fourier(uc.reshape(b, seq, D_C), fft_tabs)
        x1, hn = _outproj(oa_cls, ob, oc, x2d, seq, w["g_heads"][l][None], w["w_out"][l], w["g_mix_post"][l][None],
                          w["g_ffn_pre"][l][None], head_mask, tm)
        x2d = _ffn(hn, x1, seq, w["w_gate"][l], w["w_up"][l], w["w_down"][l], w["conv_w"][l],
                   w["conv_b"][l][None], w["g_ffn_post"][l][None], FFN_TILE, FFN_CHUNK)
    return x2d.reshape(b, seq, D_MODEL)


def _prep_weights(g_mix_pre, g_mix_post, w_in, g_q, g_k, g_heads, w_out, g_ffn_pre, g_ffn_post,
                  w_gate, w_up, conv_w, conv_b, w_down):
    return dict(
        g_mix_pre=g_mix_pre, g_mix_post=g_mix_post, g_q=g_q, g_k=g_k, g_heads=g_heads,
        g_ffn_pre=g_ffn_pre, g_ffn_post=g_ffn_post,
        w_in=w_in.astype(BF16), w_out=w_out.astype(BF16),
        w_gate=w_gate.astype(BF16), w_up=(0.5 * w_up).astype(BF16), w_down=w_down.astype(BF16),
        conv_w=conv_w, conv_b=conv_b)


def kernel(x_prompt, x_sample, g_mix_pre, g_mix_post, w_in, g_q, g_k, g_heads, w_out, g_ffn_pre, g_ffn_post,
           w_gate, w_up, conv_w, conv_b, w_down):
    w = _prep_weights(g_mix_pre, g_mix_post, w_in, g_q, g_k, g_heads, w_out, g_ffn_pre, g_ffn_post,
                      w_gate, w_up, conv_w, conv_b, w_down)
    return (_trunk(x_prompt, w), _trunk(x_sample, w))
```

```python
import functools

import numpy as np
import jax
import jax.numpy as jnp
from jax import lax
from jax.experimental import pallas as pl
from jax.experimental.pallas import tpu as pltpu

F32 = jnp.float32
BF16 = jnp.bfloat16

D_MODEL = 1024
HEAD_DIM = 64
N_HEADS_A = 4
DILATED_PATTERNS = ((128, 1), (512, 4), (2048, 16))
N_HEADS_B = 8
N_KV_B = 2
GQA_GROUP = N_HEADS_B // N_KV_B
N_GROUPS_C = 4
D_A = N_HEADS_A * HEAD_DIM
D_B = N_HEADS_B * HEAD_DIM
D_KV_B = N_KV_B * HEAD_DIM
D_C = N_GROUPS_C * HEAD_DIM
D_MIX = D_A + D_B + D_C
D_IN = 3 * D_A + D_B + 2 * D_KV_B + D_C
D_FF = 4 * D_MODEL
ROPE_THETA_PARTIAL = 500000.0
ROT_DIM_PARTIAL = HEAD_DIM // 4
ROPE_THETA_AXIAL = 10000.0
GRID_W = 64
RMS_EPS = 1e-6
NEG_INF = -1e30
Q_SCALE = HEAD_DIM ** -0.5 * float(np.log2(np.e))

LANES = 128
BF16_ROWS = 16
DIL_HALO = max(w // 2 for w, _ in DILATED_PATTERNS)
DIL_TQ = 128
DIL_FAR = DILATED_PATTERNS[-1]
DIL_CLASSES = DIL_FAR[1]
DIL_NEAR_TILES = 8
FFT_N2 = 128
VMEM_LIMIT = 56 * 1024 * 1024
TOKEN_TILE = 1024
GRID_TQ = 1024
GRID_TK = 512
FFN_TILE = 1024
FFN_CHUNK = 1024
GELU_C0 = float(np.sqrt(2.0 / np.pi))
GELU_C1 = 0.044715 * GELU_C0


def _params(*sem):
    return pltpu.CompilerParams(dimension_semantics=sem, vmem_limit_bytes=VMEM_LIMIT)


def _rms(x, g):
    return x * lax.rsqrt(jnp.mean(x * x, axis=-1, keepdims=True) + RMS_EPS) * g


def _head_mean_sq(x, m_ref):
    return jnp.dot((x * x).astype(BF16), m_ref[...], preferred_element_type=F32)


def _rotate_half(x, half):
    n = x.shape[-1]
    lane = lax.broadcasted_iota(jnp.int32, (1, n), 1)
    fwd = pltpu.roll(x, n - half, axis=1)
    bwd = pltpu.roll(x, half, axis=1)
    return jnp.where((lane % (2 * half)) < half, fwd, bwd)


def _inproj_body(x_ref, g_ref, w_ref, gq_ref, gk_ref, ca_ref, sa_ref, cb_ref, sb_ref, m_ref,
                 ka_zero, kac_zero, vac_zero, vat_zero,
                 qa_ref, qac_ref, ka_ref, kac_ref, vac_ref, vat_ref, qt_ref, kb_ref, vt_ref, uc_ref,
                 rel_ref):
    del ka_zero, kac_zero, vac_zero, vat_zero
    tm = x_ref.shape[0]
    h = _rms(x_ref[...], g_ref[...]).astype(BF16)

    def proj(lo, width):
        return jnp.dot(h, w_ref[:, lo:lo + width], preferred_element_type=F32)

    ca, sa, cb, sb = ca_ref[...], sa_ref[...], cb_ref[...], sb_ref[...]

    def rope_a(p):
        return p * ca + _rotate_half(p, ROT_DIM_PARTIAL // 2) * sa

    def norm_b(p, gain):
        return p * lax.rsqrt(_head_mean_sq(p, m_ref) + RMS_EPS) * gain

    def rope_b(y):
        return y * cb + _rotate_half(y, HEAD_DIM // 4) * sb

    def to_classes(val, out_ref):
        for c in range(D_A // LANES):
            rel_ref[c] = val[:, c * LANES:(c + 1) * LANES]
        for r in range(DIL_CLASSES):
            for c in range(D_A // LANES):
                lo = r * D_A + c * LANES
                out_ref[0, :, lo:lo + LANES] = rel_ref[c, pl.ds(r, tm // DIL_CLASSES, stride=DIL_CLASSES), :].astype(BF16)

    pq = proj(0, D_A)
    qa = jnp.concatenate([rope_a(pq[:, c * LANES:(c + 1) * LANES]) for c in range(D_A // LANES)], axis=1) * Q_SCALE
    qa_ref[...] = qa.astype(BF16)
    to_classes(qa, qac_ref)
    pk = proj(D_A, D_A)
    ka = jnp.concatenate([rope_a(pk[:, c * LANES:(c + 1) * LANES]) for c in range(D_A // LANES)], axis=1)
    ka_ref[0] = ka.astype(BF16)
    to_classes(ka, kac_ref)
    pv = proj(2 * D_A, D_A)
    to_classes(pv, vac_ref)
    vat_ref[0] = pv.T.astype(BF16)

    gq, gk = gq_ref[...], gk_ref[...]
    for g in range(N_KV_B):
        y = norm_b(proj(3 * D_A + g * 2 * LANES, 2 * LANES), gq)
        for c in range(2):
            yt = (rope_b(y[:, c * LANES:(c + 1) * LANES]) * Q_SCALE).T.astype(BF16)
            for hh in range(LANES // HEAD_DIM):
                r = c * (LANES // HEAD_DIM) + hh
                qt_ref[0, g, :, r * tm:(r + 1) * tm] = yt[hh * HEAD_DIM:(hh + 1) * HEAD_DIM, :]
    pkv = proj(3 * D_A + D_B, 2 * D_KV_B)
    kb = rope_b(norm_b(pkv, gk)[:, :D_KV_B]).astype(BF16)
    for g in range(N_KV_B):
        kb_ref[g] = kb[:, g * HEAD_DIM:(g + 1) * HEAD_DIM]
    vbt = pkv[:, D_KV_B:].T.astype(BF16)
    for g in range(N_KV_B):
        vt_ref[0, g] = vbt[g * HEAD_DIM:(g + 1) * HEAD_DIM, :]
    uc_ref[...] = proj(3 * D_A + D_B + 2 * D_KV_B, D_C)


def _inproj(x2d, batch, seq, g_pre, w_in, gq, gk, tabs, head_mask, tm, tq):
    n = x2d.shape[0]
    bps = seq // tm
    halo_blocks = DIL_HALO // tm
    seq_pad = seq + 2 * DIL_HALO
    parts = tq // tm
    row = lambda width: pl.BlockSpec((tm, width), lambda i: (i, 0))
    const = lambda shape: pl.BlockSpec(shape, lambda i: (0, 0))
    tab = pl.BlockSpec((tm, LANES), lambda i: (i % bps, 0))
    pad_rows = pl.BlockSpec((1, tm, D_A), lambda i: (i // bps, i % bps + halo_blocks, 0))
    pad_cols = pl.BlockSpec((1, D_A, tm), lambda i: (i // bps, 0, i % bps + halo_blocks))
    cls_rows, cls_width = tm // DIL_CLASSES, DIL_CLASSES * D_A
    cls = pl.BlockSpec((1, cls_rows, cls_width), lambda i: (i // bps, i % bps, 0))
    cls_pad = pl.BlockSpec((1, cls_rows, cls_width), lambda i: (i // bps, i % bps + halo_blocks, 0))
    out_specs = [row(D_A), cls, pad_rows, cls_pad, cls_pad, pad_cols,
                 pl.BlockSpec((1, N_KV_B, HEAD_DIM, GQA_GROUP * tm), lambda i: (i // parts, 0, 0, i % parts)),
                 pl.BlockSpec((N_KV_B, tm, HEAD_DIM), lambda i: (0, i, 0)),
                 pl.BlockSpec((1, N_KV_B, HEAD_DIM, tm), lambda i: (i, 0, 0, 0)),
                 row(D_C)]
    out_shape = [jax.ShapeDtypeStruct((n, D_A), BF16),
                 jax.ShapeDtypeStruct((batch, seq // DIL_CLASSES, cls_width), BF16),
                 jax.ShapeDtypeStruct((batch, seq_pad, D_A), BF16),
                 jax.ShapeDtypeStruct((batch, seq_pad // DIL_CLASSES, cls_width), BF16),
                 jax.ShapeDtypeStruct((batch, seq_pad // DIL_CLASSES, cls_width), BF16),
                 jax.ShapeDtypeStruct((batch, D_A, seq_pad), BF16),
                 jax.ShapeDtypeStruct((n // tq, N_KV_B, HEAD_DIM, GQA_GROUP * tq), BF16),
                 jax.ShapeDtypeStruct((N_KV_B, n, HEAD_DIM), BF16),
                 jax.ShapeDtypeStruct((n // tm, N_KV_B, HEAD_DIM, tm), BF16),
                 jax.ShapeDtypeStruct((n, D_C), F32)]
    padded = (2, 3, 4, 5)
    zero_bufs = [jnp.zeros(out_shape[k].shape, BF16) for k in padded]
    n_in = 10
    return pl.pallas_call(
        _inproj_body,
        grid=(n // tm,),
        in_specs=[row(D_MODEL), const((1, D_MODEL)), const((D_MODEL, D_IN)),
                  const((1, 2 * LANES)), const((1, 2 * LANES)), tab, tab, tab, tab,
                  const((2 * LANES, 2 * LANES))] + [pl.BlockSpec(memory_space=pl.ANY)] * len(padded),
        out_specs=out_specs,
        out_shape=out_shape,
        input_output_aliases={n_in + k: o for k, o in enumerate(padded)},
        scratch_shapes=[pltpu.VMEM((D_A // LANES, tm, LANES), F32)],
        name="inproj",
        compiler_params=_params("parallel"),
    )(x2d, g_pre, w_in, gq, gk, *tabs, head_mask, *zero_bufs)


DIL_NEAR = DILATED_PATTERNS[:-1]
NEAR_HALO = max(w // 2 for w, _ in DIL_NEAR)
FAR_HALO = DIL_FAR[0] // (2 * DIL_FAR[1])
STAT_LANES = LANES


def _band_bias(tq, halo, patterns):
    delta = np.arange(tq + 2 * halo)[None, :] - halo - np.arange(tq)[:, None]
    mult = np.zeros(delta.shape, np.int64)
    for window, dil in patterns:
        mult += (np.abs(delta) <= (window // (2 * dil)) * dil) & (delta % dil == 0)
    return np.where(mult > 0, np.log2(np.maximum(mult, 1)), NEG_INF).astype(np.float32)


def _tile_bias_tables(tq, halo, patterns, length):
    base = _band_bias(tq, halo, patterns).T
    n_tiles, edge = length // tq, -(-halo // tq)
    compress = n_tiles > 2 * edge + 1
    tiles = list(range(edge + 1)) + list(range(n_tiles - edge, n_tiles)) if compress else list(range(n_tiles))
    tabs = []
    for t in tiles:
        kpos = t * tq - halo + np.arange(tq + 2 * halo)
        inside = (kpos >= 0) & (kpos < length)
        tabs.append(np.where(inside[:, None], base, NEG_INF))
    tables = jnp.asarray(np.tile(np.stack(tabs), (1, 1, N_HEADS_A)), F32)

    def index(t):
        if not compress:
            return t
        if isinstance(t, int):
            return t if t < edge else (t - (n_tiles - 2 * edge - 1) if t >= n_tiles - edge else edge)
        return jnp.where(t < edge, t, jnp.where(t >= n_tiles - edge, t - (n_tiles - 2 * edge - 1), edge))

    return tables, index


def _band_attention_t(q, kw, vtw, bias_t):
    tq = q.shape[0]
    qt = q.astype(F32).T.astype(BF16)
    rows = lax.broadcasted_iota(jnp.int32, (D_A, N_HEADS_A * tq), 0) // HEAD_DIM
    cols = lax.broadcasted_iota(jnp.int32, (D_A, N_HEADS_A * tq), 1) // tq
    qbd = jnp.where(rows == cols, jnp.concatenate([qt] * N_HEADS_A, axis=1), jnp.zeros((), BF16))
    st = jnp.dot(kw, qbd, preferred_element_type=F32)
    st = st + bias_t
    m = jnp.max(st, axis=0, keepdims=True)
    p = jnp.exp2(st - m)
    l = jnp.sum(p, axis=0, keepdims=True)
    ot_all = jnp.dot(vtw, p.astype(BF16), preferred_element_type=F32)
    ot = jnp.concatenate([ot_all[h * HEAD_DIM:(h + 1) * HEAD_DIM, h * tq:(h + 1) * tq]
                          for h in range(N_HEADS_A)], axis=0)
    return ot, m, l


def _dilated_near_body(q_ref, k_ref, vt_ref, bias_ref, acc_ref, stat_ref, acc_nat, stat_nat, *, tiles, bias_index):
    tq = DIL_TQ
    win = tq + 2 * NEAR_HALO
    for t in range(tiles):
        tile = pl.program_id(1) * tiles + t
        start = pl.multiple_of(tile * tq + (DIL_HALO - NEAR_HALO), tq)
        ot, m, l = _band_attention_t(q_ref[0, t * tq:(t + 1) * tq, :], k_ref[0, pl.ds(start, win), :],
                                     vt_ref[0, :, pl.ds(start, win)], bias_ref[bias_index(tile)])
        o = ot.T
        for c in range(D_A // LANES):
            acc_nat[c, t * tq:(t + 1) * tq, :] = o[:, c * LANES:(c + 1) * LANES]
        stat_rows = ([m[:, h * tq:(h + 1) * tq] for h in range(N_HEADS_A)]
                     + [l[:, h * tq:(h + 1) * tq] for h in range(N_HEADS_A)]
                     + [jnp.zeros((STAT_LANES - 2 * N_HEADS_A, tq), F32)])
        stat_nat[t * tq:(t + 1) * tq, :] = jnp.concatenate(stat_rows, axis=0).T
    cls_rows = tiles * tq // DIL_CLASSES
    for r in range(DIL_CLASSES):
        for c in range(D_A // LANES):
            lo = r * D_A + c * LANES
            acc_ref[0, :, lo:lo + LANES] = acc_nat[c, pl.ds(r, cls_rows, stride=DIL_CLASSES), :]
        stat_ref[0, :, r * STAT_LANES:(r + 1) * STAT_LANES] = stat_nat[pl.ds(r, cls_rows, stride=DIL_CLASSES), :]


def _dilated_far_body(q_ref, k_ref, v_ref, acc_ref, stat_ref, bias_ref, o_ref, *, cls_len, bias_index):
    tq = DIL_TQ
    win = tq + 2 * FAR_HALO
    vt = v_ref[0].astype(F32).T.astype(BF16)
    for t in range(cls_len // tq):
        rs = slice(t * tq, (t + 1) * tq)
        ot2, m2, l2 = _band_attention_t(q_ref[0, rs, :], k_ref[0, t * tq:t * tq + win, :],
                                        vt[:, t * tq:t * tq + win], bias_ref[bias_index(t)])
        ot1 = acc_ref[0, rs, :].T
        st1 = stat_ref[0, rs, :].T
        outs = []
        for h in range(N_HEADS_A):
            hs = slice(h * HEAD_DIM, (h + 1) * HEAD_DIM)
            cs = slice(h * tq, (h + 1) * tq)
            m1, l1 = st1[h:h + 1, :], st1[N_HEADS_A + h:N_HEADS_A + h + 1, :]
            m = jnp.maximum(m1, m2[:, cs])
            w1, w2 = jnp.exp2(m1 - m), jnp.exp2(m2[:, cs] - m)
            outs.append((w1 * ot1[hs, :] + w2 * ot2[hs, :]) / (w1 * l1 + w2 * l2[:, cs]))
        o_ref[0, rs, :] = jnp.concatenate(outs, axis=0).T


def _dilated(qa, qa_cls, ka_pad, ka_cls, va_cls, vat_pad):
    b, seq, _ = qa.shape
    dil = DIL_CLASSES
    cls_len = seq // dil
    tq = DIL_TQ
    tiles = DIL_NEAR_TILES
    seq_pad = seq + 2 * DIL_HALO
    cls_pad = cls_len + 2 * FAR_HALO
    step_cls = tiles * tq // dil
    near_bias, near_index = _tile_bias_tables(tq, NEAR_HALO, DIL_NEAR, seq)
    far_bias, far_index = _tile_bias_tables(tq, FAR_HALO, ((2 * FAR_HALO, 1),), cls_len)
    step_blk = lambda width: pl.BlockSpec((1, step_cls, dil * width), lambda bi, i: (bi, i, 0))
    acc, stats = pl.pallas_call(
        functools.partial(_dilated_near_body, tiles=tiles, bias_index=near_index),
        grid=(b, seq // (tiles * tq)),
        in_specs=[pl.BlockSpec((1, tiles * tq, D_A), lambda bi, i: (bi, i, 0)),
                  pl.BlockSpec((1, seq_pad, D_A), lambda bi, i: (bi, 0, 0)),
                  pl.BlockSpec((1, D_A, seq_pad), lambda bi, i: (bi, 0, 0)),
                  pl.BlockSpec(near_bias.shape, lambda bi, i: (0, 0, 0))],
        out_specs=[step_blk(D_A), step_blk(STAT_LANES)],
        out_shape=[jax.ShapeDtypeStruct((b, cls_len, dil * D_A), F32),
                   jax.ShapeDtypeStruct((b, cls_len, dil * STAT_LANES), F32)],
        scratch_shapes=[pltpu.VMEM((D_A // LANES, tiles * tq, LANES), F32),
                        pltpu.VMEM((tiles * tq, STAT_LANES), F32)],
        name="dilated_near",
        compiler_params=_params("parallel", "arbitrary"),
    )(qa, ka_pad, vat_pad, near_bias)
    cls_blk = lambda width: pl.BlockSpec((1, cls_len, width), lambda bi, r: (bi, 0, r))
    cls_kv = pl.BlockSpec((1, cls_pad, D_A), lambda bi, r: (bi, 0, r))
    return pl.pallas_call(
        functools.partial(_dilated_far_body, cls_len=cls_len, bias_index=far_index),
        grid=(b, dil),
        in_specs=[cls_blk(D_A), cls_kv, cls_kv, cls_blk(D_A), cls_blk(STAT_LANES),
                  pl.BlockSpec(far_bias.shape, lambda bi, r: (0, 0, 0))],
        out_specs=cls_blk(D_A),
        out_shape=jax.ShapeDtypeStruct((b, cls_len, dil * D_A), F32),
        name="dilated_far",
        compiler_params=_params("parallel", "parallel"),
    )(qa_cls, ka_cls, va_cls, acc, stats, far_bias)


def _grid_attn_body(qt_ref, k_ref, vt_ref, o_ref, s0_ref, s1_ref, acc_ref, *, tq, tk, nk):
    cols = acc_ref.shape[2]
    vparts = tk // vt_ref.shape[-1]
    vsplit = vt_ref.shape[-1] // tk
    for g in range(N_KV_B):
        qt = qt_ref[0, g]

        def scores(kb):
            k0 = pl.multiple_of(kb * tk, tk)
            return jnp.dot(k_ref[g, pl.ds(k0, tk), :], qt, preferred_element_type=F32)

        def consume(s_ref, kb, carry):
            m_prev, l_prev = carry
            s = s_ref[...]
            m_new = jnp.maximum(m_prev, jnp.max(s, axis=0, keepdims=True))
            alpha = jnp.exp2(m_prev - m_new)
            p = jnp.exp2(s - m_new)
            l_new = alpha * l_prev + jnp.sum(p, axis=0, keepdims=True)
            if vparts:
                vt = jnp.concatenate([vt_ref[kb * vparts + u, g] for u in range(vparts)], axis=1)
            else:
                off = pl.multiple_of((kb % vsplit) * tk, tk)
                vt = vt_ref[kb // vsplit, g, :, pl.ds(off, tk)]
            pv = jnp.dot(vt, p.astype(BF16), preferred_element_type=F32)
            acc_ref[g] = alpha * acc_ref[g] + pv
            return m_new, l_new

        if g == 0:
            s0_ref[...] = scores(0)
        acc_ref[g] = jnp.zeros(acc_ref.shape[1:], F32)

        def pair(kb, carry, prefetch):
            s1_ref[...] = scores(kb + 1)
            carry = consume(s0_ref, kb, carry)
            if prefetch is not None:
                s0_ref[...] = prefetch()
            return consume(s1_ref, kb + 1, carry)

        init = (jnp.full((1, cols), NEG_INF, F32), jnp.zeros((1, cols), F32))
        carry = lax.fori_loop(0, nk // 2 - 1, lambda j, c: pair(2 * j, c, lambda: scores(2 * j + 2)), init)
        if g + 1 < N_KV_B:
            k_next, qt_next = k_ref.at[g + 1], qt_ref[0, g + 1]
            prefetch = lambda: jnp.dot(k_next[pl.ds(0, tk), :], qt_next, preferred_element_type=F32)
        else:
            prefetch = None
        _, l_fin = pair(nk - 2, carry, prefetch)
        acc_ref[g] = acc_ref[g] / l_fin
    ot = acc_ref[...].reshape(N_KV_B * HEAD_DIM, cols).T
    tv = vt_ref.shape[-1]
    for part in range(tq // tv):
        heads = [ot[(part * GQA_GROUP + r) * tv:(part * GQA_GROUP + r + 1) * tv, g * HEAD_DIM:(g + 1) * HEAD_DIM]
                 for g in range(N_KV_B) for r in range(GQA_GROUP)]
        o_ref[0, part * tv:(part + 1) * tv, :] = jnp.concatenate(heads, axis=1).astype(o_ref.dtype)


def _grid_attn(qt, kb, vt, b, tq, tk):
    seq = qt.shape[0] // b * tq
    cols = GQA_GROUP * tq
    tv = vt.shape[-1]
    nq, nk = seq // tq, seq // tk
    return pl.pallas_call(
        functools.partial(_grid_attn_body, tq=tq, tk=tk, nk=nk),
        grid=(b, nq),
        in_specs=[pl.BlockSpec((1, N_KV_B, HEAD_DIM, cols), lambda bi, i: (bi * nq + i, 0, 0, 0)),
                  pl.BlockSpec((N_KV_B, seq, HEAD_DIM), lambda bi, i: (0, bi, 0)),
                  pl.BlockSpec((seq // tv, N_KV_B, HEAD_DIM, tv), lambda bi, i: (bi, 0, 0, 0))],
        out_specs=pl.BlockSpec((1, tq, D_B), lambda bi, i: (bi, i, 0)),
        out_shape=jax.ShapeDtypeStruct((b, seq, D_B), BF16),
        scratch_shapes=[pltpu.VMEM((tk, cols), F32), pltpu.VMEM((tk, cols), F32),
                        pltpu.VMEM((N_KV_B, HEAD_DIM, cols), F32)],
        name="grid_attn",
        compiler_params=_params("parallel", "arbitrary"),
    )(qt, kb, vt)


def _fft_tables(seq):
    n1, n2 = seq // FFT_N2, FFT_N2
    k1 = np.arange(n1)
    ang1 = 2.0 * np.pi * ((k1[:, None] * k1[None, :]) % n1) / n1
    f1 = np.concatenate([np.cos(ang1), -np.sin(ang1)], axis=0)
    k = k1[:, None, None] + n1 * np.arange(n2)[None, :, None]
    ang2 = 2.0 * np.pi * ((k * np.arange(n2)[None, None, :]) % seq) / seq
    g = np.concatenate([np.cos(ang2), np.sin(ang2)], axis=-1)
    c = np.arange(HEAD_DIM)
    ang3 = 2.0 * np.pi * ((c[:, None] * c[None, :]) % HEAD_DIM) / HEAD_DIM
    eye = np.eye(N_GROUPS_C)
    dc, ds = np.kron(eye, np.cos(ang3)), np.kron(eye, np.sin(ang3))
    d = np.concatenate([np.concatenate([dc, -ds], axis=1), np.concatenate([ds, dc], axis=1)], axis=0)
    return tuple(jnp.asarray(t, BF16) for t in (f1, g, d))


def _fft1_body(u_ref, f_ref, a_ref):
    a_ref[0] = jnp.dot(f_ref[...], u_ref[0].astype(BF16), preferred_element_type=F32).astype(BF16)


def _fft2_body(a_ref, g_ref, d_ref, o_ref, *, kb1):
    for j in range(kb1):
        ar, ai = a_ref[0, 0, j], a_ref[0, 1, j]
        p = (jnp.dot(ar, d_ref[:D_C, :], preferred_element_type=F32)
             + jnp.dot(ai, d_ref[D_C:, :], preferred_element_type=F32)).astype(BF16)
        gj = g_ref[j]
        o_ref[0, :, j * D_C:(j + 1) * D_C] = (
            jnp.dot(gj[:, :FFT_N2], p[:, :D_C], preferred_element_type=F32)
            + jnp.dot(gj[:, FFT_N2:], p[:, D_C:], preferred_element_type=F32)).astype(o_ref.dtype)


def _fourier(uc, tables):
    b, seq, _ = uc.shape
    f1, g, d = tables
    n1, n2 = seq // FFT_N2, FFT_N2
    cols = n2 * D_C
    tc = 4096
    a = pl.pallas_call(
        _fft1_body,
        grid=(b, cols // tc),
        in_specs=[pl.BlockSpec((1, n1, tc), lambda bi, j: (bi, 0, j)),
                  pl.BlockSpec((2 * n1, n1), lambda bi, j: (0, 0))],
        out_specs=pl.BlockSpec((1, 2 * n1, tc), lambda bi, j: (bi, 0, j)),
        out_shape=jax.ShapeDtypeStruct((b, 2 * n1, cols), BF16),
        name="fft1",
        compiler_params=_params("parallel", "parallel"),
    )(uc.reshape(b, n1, cols), f1)
    kb1 = 8
    z = pl.pallas_call(
        functools.partial(_fft2_body, kb1=kb1),
        grid=(b, n1 // kb1),
        in_specs=[pl.BlockSpec((1, 2, kb1, n2, D_C), lambda bi, j: (bi, 0, j, 0, 0)),
                  pl.BlockSpec((kb1, n2, 2 * n2), lambda bi, j: (j, 0, 0)),
                  pl.BlockSpec((2 * D_C, 2 * D_C), lambda bi, j: (0, 0))],
        out_specs=pl.BlockSpec((1, n2, kb1 * D_C), lambda bi, j: (bi, 0, j)),
        out_shape=jax.ShapeDtypeStruct((b, n2, n1 * D_C), BF16),
        name="fft2",
        compiler_params=_params("parallel", "parallel"),
    )(a.reshape(b, 2, n1, n2, D_C), g, d)
    return z.reshape(b * seq, D_C)


def _outproj_body(oac_ref, ob_ref, oc_ref, x_ref, gh_ref, w_ref, gpost_ref, gffn_ref, m_ref,
                  x1_ref, hn_ref, oa_ref):
    cls_rows = oa_ref.shape[1] // DIL_CLASSES
    for r in range(DIL_CLASSES):
        for c in range(D_A // LANES):
            lo = r * D_A + c * LANES
            oa_ref[c, pl.ds(r, cls_rows, stride=DIL_CLASSES), :] = oac_ref[0, :, lo:lo + LANES]
    oa = jnp.concatenate([oa_ref[c] for c in range(D_A // LANES)], axis=1)
    mix = None
    col = 0
    width = 2 * LANES
    for o_src in (oa, ob_ref, oc_ref):
        for c in range(o_src.shape[-1] // width):
            o = o_src[:, c * width:(c + 1) * width].astype(F32)
            y = o * lax.rsqrt(_head_mean_sq(o, m_ref) + RMS_EPS) * gh_ref[:, col:col + width]
            part = jnp.dot(y.astype(BF16), w_ref[col:col + width, :], preferred_element_type=F32)
            mix = part if mix is None else mix + part
            col += width
    x1 = x_ref[...] + _rms(mix, gpost_ref[...])
    x1_ref[...] = x1
    hn_ref[...] = _rms(x1, gffn_ref[...]).astype(BF16)


def _outproj(oa_cls, ob, oc, x2d, seq, g_heads, w_out, g_post, g_ffn, head_mask, tm):
    n = x2d.shape[0]
    bps = seq // tm
    row = lambda width: pl.BlockSpec((tm, width), lambda i: (i, 0))
    const = lambda shape: pl.BlockSpec(shape, lambda i: (0, 0))
    return pl.pallas_call(
        _outproj_body,
        grid=(n // tm,),
        in_specs=[pl.BlockSpec((1, tm // DIL_CLASSES, DIL_CLASSES * D_A), lambda i: (i // bps, i % bps, 0)),
                  row(D_B), row(D_C), row(D_MODEL), const((1, D_MIX)),
                  const((D_MIX, D_MODEL)), const((1, D_MODEL)), const((1, D_MODEL)),
                  const((2 * LANES, 2 * LANES))],
        out_specs=[row(D_MODEL), row(D_MODEL)],
        out_shape=[jax.ShapeDtypeStruct((n, D_MODEL), F32), jax.ShapeDtypeStruct((n, D_MODEL), BF16)],
        scratch_shapes=[pltpu.VMEM((D_A // LANES, tm, LANES), F32)],
        name="outproj",
        compiler_params=_params("parallel"),
    )(oa_cls, ob, oc, x2d, g_heads, w_out, g_post, g_ffn, head_mask)


def _ffn_body(h_ref, hp_ref, hx_ref, x_ref, wg_ref, wu_ref, wd_ref, cw_ref, cb_ref, gpost_ref,
              o_ref, hext_ref, acc_ref, *, tm, blocks_per_seq):
    i = pl.program_id(0)
    c = pl.program_id(1)
    pad = BF16_ROWS
    rows = tm + 2 * pad

    @pl.when(c == 0)
    def _():
        first = (i % blocks_per_seq) == 0
        last = (i % blocks_per_seq) == blocks_per_seq - 1
        hext_ref[:pad, :] = jnp.where(first, jnp.zeros_like(hp_ref[0]), hp_ref[0])
        hext_ref[pad:pad + tm, :] = h_ref[...]
        hext_ref[pad + tm:, :] = jnp.where(last, jnp.zeros_like(hx_ref[0]), hx_ref[0])
        acc_ref[...] = jnp.zeros(acc_ref.shape, F32)

    g = jnp.dot(hext_ref[...], wg_ref[...], preferred_element_type=F32)
    cw = cw_ref[...]
    g = (pltpu.roll(g, 1, axis=0)[pad:pad + tm] * cw[0:1, :] + g[pad:pad + tm] * cw[1:2, :]
         + pltpu.roll(g, rows - 1, axis=0)[pad:pad + tm] * cw[2:3, :] + cb_ref[...])
    up = jnp.dot(h_ref[...], wu_ref[...], preferred_element_type=F32)
    inner = g * (GELU_C0 + GELU_C1 * (g * g))
    act = (g * (1.0 + jnp.tanh(inner)) * up).astype(BF16)
    acc_ref[...] += jnp.dot(act, wd_ref[...], preferred_element_type=F32)

    @pl.when(c == pl.num_programs(1) - 1)
    def _():
        o_ref[...] = x_ref[...] + _rms(acc_ref[...], gpost_ref[...])


def _ffn(hn, x1, seq, w_gate, w_up, w_down, conv_w, conv_b, g_post, tm, tf):
    n = x1.shape[0]
    nf = D_FF // tf
    blocks_per_seq = seq // tm
    halo_blocks = tm // BF16_ROWS
    n_halo = n // BF16_ROWS
    hn3 = hn.reshape(n_halo, BF16_ROWS, D_MODEL)
    row = lambda: pl.BlockSpec((tm, D_MODEL), lambda i, c: (i, 0))
    prev = pl.BlockSpec((1, BF16_ROWS, D_MODEL), lambda i, c: (jnp.maximum(i * halo_blocks - 1, 0), 0, 0))
    nxt = pl.BlockSpec((1, BF16_ROWS, D_MODEL),
                       lambda i, c: (jnp.minimum((i + 1) * halo_blocks, n_halo - 1), 0, 0))
    cols = lambda rows: pl.BlockSpec((rows, tf), lambda i, c: (0, c))
    return pl.pallas_call(
        functools.partial(_ffn_body, tm=tm, blocks_per_seq=blocks_per_seq),
        grid=(n // tm, nf),
        in_specs=[row(), prev, nxt, row(), cols(D_MODEL), cols(D_MODEL),
                  pl.BlockSpec((tf, D_MODEL), lambda i, c: (c, 0)), cols(3), cols(1),
                  pl.BlockSpec((1, D_MODEL), lambda i, c: (0, 0))],
        out_specs=row(),
        out_shape=jax.ShapeDtypeStruct((n, D_MODEL), F32),
        scratch_shapes=[pltpu.VMEM((tm + 2 * BF16_ROWS, D_MODEL), BF16), pltpu.VMEM((tm, D_MODEL), F32)],
        name="ffn",
        compiler_params=_params("parallel", "arbitrary"),
    )(hn, hn3, hn3, x1, w_gate, w_up, w_down, conv_w, conv_b, g_post)


def _rope_tables(pos, rot_dim, theta):
    half = rot_dim // 2
    inv = jnp.power(jnp.float32(theta), -jnp.arange(half, dtype=jnp.float32) / half)
    ang = pos.astype(jnp.float32)[:, None] * inv[None, :]
    return jnp.cos(ang), jnp.sin(ang)


def _position_tables(seq):
    rows = seq // GRID_W
    t = jnp.arange(seq, dtype=jnp.int32)
    row = jnp.broadcast_to(jnp.arange(rows, dtype=jnp.int32)[:, None], (rows, GRID_W)).reshape(-1)
    col = jnp.broadcast_to(jnp.arange(GRID_W, dtype=jnp.int32)[None, :], (rows, GRID_W)).reshape(-1)
    cos_t, sin_t = _rope_tables(t, ROT_DIM_PARTIAL, ROPE_THETA_PARTIAL)
    cos_r, sin_r = _rope_tables(row, HEAD_DIM // 2, ROPE_THETA_AXIAL)
    cos_c, sin_c = _rope_tables(col, HEAD_DIM // 2, ROPE_THETA_AXIAL)
    rest = HEAD_DIM - ROT_DIM_PARTIAL
    ca = jnp.concatenate([cos_t, cos_t, jnp.ones((seq, rest), F32)], axis=-1)
    sa = jnp.concatenate([-sin_t, sin_t, jnp.zeros((seq, rest), F32)], axis=-1)
    cb = jnp.concatenate([cos_r, cos_r, cos_c, cos_c], axis=-1)
    sb = jnp.concatenate([-sin_r, sin_r, -sin_c, sin_c], axis=-1)
    reps = LANES // HEAD_DIM
    return tuple(jnp.tile(t_, (1, reps)) for t_ in (ca, sa, cb, sb))


def _head_mask():
    lane = np.arange(2 * LANES)
    same = (lane[:, None] // HEAD_DIM) == (lane[None, :] // HEAD_DIM)
    return jnp.asarray(same / HEAD_DIM, BF16)


def _trunk(x, w):
    b, seq, _ = x.shape
    n = b * seq
    tm, tq, tk = TOKEN_TILE, min(GRID_TQ, seq), min(GRID_TK, seq // 2)
    tabs = _position_tables(seq)
    fft_tabs = _fft_tables(seq)
    head_mask = _head_mask()
    x2d = x.reshape(n, D_MODEL)
    depth = w["w_in"].shape[0]
    for l in range(depth):
        tile4 = lambda g_: jnp.tile(g_[l][None, :], (1, 2 * LANES // HEAD_DIM))
        qa, qa_cls, ka_pad, ka_cls, va_cls, vat_pad, qt, kb, vt, uc = _inproj(
            x2d, b, seq, w["g_mix_pre"][l][None], w["w_in"][l], tile4(w["g_q"]), tile4(w["g_k"]),
            tabs, head_mask, tm, tq)
        oa_cls = _dilated(qa.reshape(b, seq, D_A), qa_cls, ka_pad, ka_cls, va_cls, vat_pad)
        ob = _grid_attn(qt, kb, vt, b, tq, tk).reshape(n, D_B)
        oc = _fourier(uc.reshape(b, seq, D_C), fft_tabs)
        x1, hn = _outproj(oa_cls, ob, oc, x2d, seq, w["g_heads"][l][None], w["w_out"][l], w["g_mix_post"][l][None],
                          w["g_ffn_pre"][l][None], head_mask, tm)
        x2d = _ffn(hn, x1, seq, w["w_gate"][l], w["w_up"][l], w["w_down"][l], w["conv_w"][l],
                   w["conv_b"][l][None], w["g_ffn_post"][l][None], FFN_TILE, FFN_CHUNK)
    return x2d.reshape(b, seq, D_MODEL)


def _prep_weights(g_mix_pre, g_mix_post, w_in, g_q, g_k, g_heads, w_out, g_ffn_pre, g_ffn_post,
                  w_gate, w_up, conv_w, conv_b, w_down):
    return dict(
        g_mix_pre=g_mix_pre, g_mix_post=g_mix_post, g_q=g_q, g_k=g_k, g_heads=g_heads,
        g_ffn_pre=g_ffn_pre, g_ffn_post=g_ffn_post,
        w_in=w_in.astype(BF16), w_out=w_out.astype(BF16),
        w_gate=w_gate.astype(BF16), w_up=(0.5 * w_up).astype(BF16), w_down=w_down.astype(BF16),
        conv_w=conv_w, conv_b=conv_b)


def kernel(x_prompt, x_sample, g_mix_pre, g_mix_post, w_in, g_q, g_k, g_heads, w_out, g_ffn_pre, g_ffn_post,
           w_gate, w_up, conv_w, conv_b, w_down):
    w = _prep_weights(g_mix_pre, g_mix_post, w_in, g_q, g_k, g_heads, w_out, g_ffn_pre, g_ffn_post,
                      w_gate, w_up, conv_w, conv_b, w_down)
    return (_trunk(x_prompt, w), _trunk(x_sample, w))
```

```python
import functools

import numpy as np
import jax
import jax.numpy as jnp
from jax import lax
from jax.experimental import pallas as pl
from jax.experimental.pallas import tpu as pltpu

F32 = jnp.float32
BF16 = jnp.bfloat16

D_MODEL = 1024
HEAD_DIM = 64
N_HEADS_A = 4
DILATED_PATTERNS = ((128, 1), (512, 4), (2048, 16))
N_HEADS_B = 8
N_KV_B = 2
GQA_GROUP = N_HEADS_B // N_KV_B
N_GROUPS_C = 4
D_A = N_HEADS_A * HEAD_DIM
D_B = N_HEADS_B * HEAD_DIM
D_KV_B = N_KV_B * HEAD_DIM
D_C = N_GROUPS_C * HEAD_DIM
D_MIX = D_A + D_B + D_C
D_IN = 3 * D_A + D_B + 2 * D_KV_B + D_C
D_FF = 4 * D_MODEL
ROPE_THETA_PARTIAL = 500000.0
ROT_DIM_PARTIAL = HEAD_DIM // 4
ROPE_THETA_AXIAL = 10000.0
GRID_W = 64
RMS_EPS = 1e-6
NEG_INF = -1e30
Q_SCALE = HEAD_DIM ** -0.5 * float(np.log2(np.e))

LANES = 128
BF16_ROWS = 16
DIL_HALO = max(w // 2 for w, _ in DILATED_PATTERNS)
DIL_TQ = 128
DIL_FAR = DILATED_PATTERNS[-1]
DIL_CLASSES = DIL_FAR[1]
DIL_NEAR_TILES = 8
FFT_N2 = 128
VMEM_LIMIT = 56 * 1024 * 1024
TOKEN_TILE = 1024
GRID_TQ = 1024
GRID_TK = 512
FFN_TILE = 1024
FFN_CHUNK = 1024
GELU_C0 = float(np.sqrt(2.0 / np.pi))
GELU_C1 = 0.044715 * GELU_C0


def _params(*sem):
    return pltpu.CompilerParams(dimension_semantics=sem, vmem_limit_bytes=VMEM_LIMIT)


def _rms(x, g):
    return x * lax.rsqrt(jnp.mean(x * x, axis=-1, keepdims=True) + RMS_EPS) * g


def _head_mean_sq(x, m_ref):
    return jnp.dot((x * x).astype(BF16), m_ref[...], preferred_element_type=F32)


def _rotate_half(x, half):
    n = x.shape[-1]
    lane = lax.broadcasted_iota(jnp.int32, (1, n), 1)
    fwd = pltpu.roll(x, n - half, axis=1)
    bwd = pltpu.roll(x, half, axis=1)
    return jnp.where((lane % (2 * half)) < half, fwd, bwd)


def _inproj_body(x_ref, g_ref, w_ref, gq_ref, gk_ref, ca_ref, sa_ref, cb_ref, sb_ref, m_ref,
                 ka_zero, kac_zero, vac_zero, vat_zero,
                 qa_ref, qac_ref, ka_ref, kac_ref, vac_ref, vat_ref, qt_ref, kb_ref, vt_ref, uc_ref,
                 rel_ref):
    del ka_zero, kac_zero, vac_zero, vat_zero
    tm = x_ref.shape[0]
    h = _rms(x_ref[...], g_ref[...]).astype(BF16)

    def proj(lo, width):
        return jnp.dot(h, w_ref[:, lo:lo + width], preferred_element_type=F32)

    ca, sa, cb, sb = ca_ref[...], sa_ref[...], cb_ref[...], sb_ref[...]

    def rope_a(p):
        return p * ca + _rotate_half(p, ROT_DIM_PARTIAL // 2) * sa

    def norm_b(p, gain):
        return p * lax.rsqrt(_head_mean_sq(p, m_ref) + RMS_EPS) * gain

    def rope_b(y):
        return y * cb + _rotate_half(y, HEAD_DIM // 4) * sb

    def to_classes(val, out_ref):
        for c in range(D_A // LANES):
            rel_ref[c] = val[:, c * LANES:(c + 1) * LANES]
        for r in range(DIL_CLASSES):
            for c in range(D_A // LANES):
                lo = r * D_A + c * LANES
                out_ref[0, :, lo:lo + LANES] = rel_ref[c, pl.ds(r, tm // DIL_CLASSES, stride=DIL_CLASSES), :].astype(BF16)

    pq = proj(0, D_A)
    qa = jnp.concatenate([rope_a(pq[:, c * LANES:(c + 1) * LANES]) for c in range(D_A // LANES)], axis=1) * Q_SCALE
    qa_ref[...] = qa.astype(BF16)
    to_classes(qa, qac_ref)
    pk = proj(D_A, D_A)
    ka = jnp.concatenate([rope_a(pk[:, c * LANES:(c + 1) * LANES]) for c in range(D_A // LANES)], axis=1)
    ka_ref[0] = ka.astype(BF16)
    to_classes(ka, kac_ref)
    pv = proj(2 * D_A, D_A)
    to_classes(pv, vac_ref)
    vat_ref[0] = pv.T.astype(BF16)

    gq, gk = gq_ref[...], gk_ref[...]
    for g in range(N_KV_B):
        y = norm_b(proj(3 * D_A + g * 2 * LANES, 2 * LANES), gq)
        for c in range(2):
            yt = (rope_b(y[:, c * LANES:(c + 1) * LANES]) * Q_SCALE).T.astype(BF16)
            for hh in range(LANES // HEAD_DIM):
                r = c * (LANES // HEAD_DIM) + hh
                qt_ref[0, g, :, r * tm:(r + 1) * tm] = yt[hh * HEAD_DIM:(hh + 1) * HEAD_DIM, :]
    pkv = proj(3 * D_A + D_B, 2 * D_KV_B)
    kb = rope_b(norm_b(pkv, gk)[:, :D_KV_B]).astype(BF16)
    for g in range(N_KV_B):
        kb_ref[g] = kb[:, g * HEAD_DIM:(g + 1) * HEAD_DIM]
    vbt = pkv[:, D_KV_B:].T.astype(BF16)
    for g in range(N_KV_B):
        vt_ref[0, g] = vbt[g * HEAD_DIM:(g + 1) * HEAD_DIM, :]
    uc_ref[...] = proj(3 * D_A + D_B + 2 * D_KV_B, D_C)


def _inproj(x2d, batch, seq, g_pre, w_in, gq, gk, tabs, head_mask, tm, tq):
    n = x2d.shape[0]
    bps = seq // tm
    halo_blocks = DIL_HALO // tm
    seq_pad = seq + 2 * DIL_HALO
    parts = tq // tm
    row = lambda width: pl.BlockSpec((tm, width), lambda i: (i, 0))
    const = lambda shape: pl.BlockSpec(shape, lambda i: (0, 0))
    tab = pl.BlockSpec((tm, LANES), lambda i: (i % bps, 0))
    pad_rows = pl.BlockSpec((1, tm, D_A), lambda i: (i // bps, i % bps + halo_blocks, 0))
    pad_cols = pl.BlockSpec((1, D_A, tm), lambda i: (i // bps, 0, i % bps + halo_blocks))
    cls_rows, cls_width = tm // DIL_CLASSES, DIL_CLASSES * D_A
    cls = pl.BlockSpec((1, cls_rows, cls_width), lambda i: (i // bps, i % bps, 0))
    cls_pad = pl.BlockSpec((1, cls_rows, cls_width), lambda i: (i // bps, i % bps + halo_blocks, 0))
    out_specs = [row(D_A), cls, pad_rows, cls_pad, cls_pad, pad_cols,
                 pl.BlockSpec((1, N_KV_B, HEAD_DIM, GQA_GROUP * tm), lambda i: (i // parts, 0, 0, i % parts)),
                 pl.BlockSpec((N_KV_B, tm, HEAD_DIM), lambda i: (0, i, 0)),
                 pl.BlockSpec((1, N_KV_B, HEAD_DIM, tm), lambda i: (i, 0, 0, 0)),
                 row(D_C)]
    out_shape = [jax.ShapeDtypeStruct((n, D_A), BF16),
                 jax.ShapeDtypeStruct((batch, seq // DIL_CLASSES, cls_width), BF16),
                 jax.ShapeDtypeStruct((batch, seq_pad, D_A), BF16),
                 jax.ShapeDtypeStruct((batch, seq_pad // DIL_CLASSES, cls_width), BF16),
                 jax.ShapeDtypeStruct((batch, seq_pad // DIL_CLASSES, cls_width), BF16),
                 jax.ShapeDtypeStruct((batch, D_A, seq_pad), BF16),
                 jax.ShapeDtypeStruct((n // tq, N_KV_B, HEAD_DIM, GQA_GROUP * tq), BF16),
                 jax.ShapeDtypeStruct((N_KV_B, n, HEAD_DIM), BF16),
                 jax.ShapeDtypeStruct((n // tm, N_KV_B, HEAD_DIM, tm), BF16),
                 jax.ShapeDtypeStruct((n, D_C), F32)]
    padded = (2, 3, 4, 5)
    zero_bufs = [jnp.zeros(out_shape[k].shape, BF16) for k in padded]
    n_in = 10
    return pl.pallas_call(
        _inproj_body,
        grid=(n // tm,),
        in_specs=[row(D_MODEL), const((1, D_MODEL)), const((D_MODEL, D_IN)),
                  const((1, 2 * LANES)), const((1, 2 * LANES)), tab, tab, tab, tab,
                  const((2 * LANES, 2 * LANES))] + [pl.BlockSpec(memory_space=pl.ANY)] * len(padded),
        out_specs=out_specs,
        out_shape=out_shape,
        input_output_aliases={n_in + k: o for k, o in enumerate(padded)},
        scratch_shapes=[pltpu.VMEM((D_A // LANES, tm, LANES), F32)],
        name="inproj",
        compiler_params=_params("parallel"),
    )(x2d, g_pre, w_in, gq, gk, *tabs, head_mask, *zero_bufs)


DIL_NEAR = DILATED_PATTERNS[:-1]
NEAR_HALO = max(w // 2 for w, _ in DIL_NEAR)
FAR_HALO = DIL_FAR[0] // (2 * DIL_FAR[1])
STAT_LANES = LANES


def _band_bias(tq, halo, patterns):
    delta = np.arange(tq + 2 * halo)[None, :] - halo - np.arange(tq)[:, None]
    mult = np.zeros(delta.shape, np.int64)
    for window, dil in patterns:
        mult += (np.abs(delta) <= (window // (2 * dil)) * dil) & (delta % dil == 0)
    return np.where(mult > 0, np.log2(np.maximum(mult, 1)), NEG_INF).astype(np.float32)


def _tile_bias_tables(tq, halo, patterns, length):
    base = _band_bias(tq, halo, patterns).T
    n_tiles, edge = length // tq, -(-halo // tq)
    compress = n_tiles > 2 * edge + 1
    tiles = list(range(edge + 1)) + list(range(n_tiles - edge, n_tiles)) if compress else list(range(n_tiles))
    tabs = []
    for t in tiles:
        kpos = t * tq - halo + np.arange(tq + 2 * halo)
        inside = (kpos >= 0) & (kpos < length)
        tabs.append(np.where(inside[:, None], base, NEG_INF))
    tables = jnp.asarray(np.tile(np.stack(tabs), (1, 1, N_HEADS_A)), F32)

    def index(t):
        if not compress:
            return t
        if isinstance(t, int):
            return t if t < edge else (t - (n_tiles - 2 * edge - 1) if t >= n_tiles - edge else edge)
        return jnp.where(t < edge, t, jnp.where(t >= n_tiles - edge, t - (n_tiles - 2 * edge - 1), edge))

    return tables, index


def _band_attention_t(q, kw, vtw, bias_t):
    tq = q.shape[0]
    qt = q.astype(F32).T.astype(BF16)
    rows = lax.broadcasted_iota(jnp.int32, (D_A, N_HEADS_A * tq), 0) // HEAD_DIM
    cols = lax.broadcasted_iota(jnp.int32, (D_A, N_HEADS_A * tq), 1) // tq
    qbd = jnp.where(rows == cols, jnp.concatenate([qt] * N_HEADS_A, axis=1), jnp.zeros((), BF16))
    st = jnp.dot(kw, qbd, preferred_element_type=F32)
    st = st + bias_t
    m = jnp.max(st, axis=0, keepdims=True)
    p = jnp.exp2(st - m)
    l = jnp.sum(p, axis=0, keepdims=True)
    pb = p.astype(BF16)
    blocks = []
    per = 2 * LANES // tq
    for c in range(N_HEADS_A // per):
        rs = slice(c * per * HEAD_DIM, (c + 1) * per * HEAD_DIM)
        o = jnp.dot(vtw[rs, :], pb[:, c * per * tq:(c + 1) * per * tq], preferred_element_type=F32)
        blocks += [o[h * HEAD_DIM:(h + 1) * HEAD_DIM, h * tq:(h + 1) * tq] for h in range(per)]
    return jnp.concatenate(blocks, axis=0), m, l


def _dilated_near_body(q_ref, k_ref, vt_ref, bias_ref, acc_ref, stat_ref, acc_nat, stat_nat, *, tiles, bias_index):
    tq = DIL_TQ
    win = tq + 2 * NEAR_HALO
    for t in range(tiles):
        tile = pl.program_id(1) * tiles + t
        start = pl.multiple_of(tile * tq + (DIL_HALO - NEAR_HALO), tq)
        ot, m, l = _band_attention_t(q_ref[0, t * tq:(t + 1) * tq, :], k_ref[0, pl.ds(start, win), :],
                                     vt_ref[0, :, pl.ds(start, win)], bias_ref[bias_index(tile)])
        o = ot.T
        for c in range(D_A // LANES):
            acc_nat[c, t * tq:(t + 1) * tq, :] = o[:, c * LANES:(c + 1) * LANES]
        stat_rows = ([m[:, h * tq:(h + 1) * tq] for h in range(N_HEADS_A)]
                     + [l[:, h * tq:(h + 1) * tq] for h in range(N_HEADS_A)]
                     + [jnp.zeros((STAT_LANES - 2 * N_HEADS_A, tq), F32)])
        stat_nat[t * tq:(t + 1) * tq, :] = jnp.concatenate(stat_rows, axis=0).T
    cls_rows = tiles * tq // DIL_CLASSES
    for r in range(DIL_CLASSES):
        for c in range(D_A // LANES):
            lo = r * D_A + c * LANES
            acc_ref[0, :, lo:lo + LANES] = acc_nat[c, pl.ds(r, cls_rows, stride=DIL_CLASSES), :]
        stat_ref[0, :, r * STAT_LANES:(r + 1) * STAT_LANES] = stat_nat[pl.ds(r, cls_rows, stride=DIL_CLASSES), :]


def _dilated_far_body(q_ref, k_ref, v_ref, acc_ref, stat_ref, bias_ref, o_ref, *, cls_len, bias_index):
    tq = DIL_TQ
    win = tq + 2 * FAR_HALO
    vt = v_ref[0].astype(F32).T.astype(BF16)
    for t in range(cls_len // tq):
        rs = slice(t * tq, (t + 1) * tq)
        ot2, m2, l2 = _band_attention_t(q_ref[0, rs, :], k_ref[0, t * tq:t * tq + win, :],
                                        vt[:, t * tq:t * tq + win], bias_ref[bias_index(t)])
        ot1 = acc_ref[0, rs, :].T
        st1 = stat_ref[0, rs, :].T
        outs = []
        for h in range(N_HEADS_A):
            hs = slice(h * HEAD_DIM, (h + 1) * HEAD_DIM)
            cs = slice(h * tq, (h + 1) * tq)
            m1, l1 = st1[h:h + 1, :], st1[N_HEADS_A + h:N_HEADS_A + h + 1, :]
            m = jnp.maximum(m1, m2[:, cs])
            w1, w2 = jnp.exp2(m1 - m), jnp.exp2(m2[:, cs] - m)
            outs.append((w1 * ot1[hs, :] + w2 * ot2[hs, :]) / (w1 * l1 + w2 * l2[:, cs]))
        o_ref[0, rs, :] = jnp.concatenate(outs, axis=0).T


def _dilated(qa, qa_cls, ka_pad, ka_cls, va_cls, vat_pad):
    b, seq, _ = qa.shape
    dil = DIL_CLASSES
    cls_len = seq // dil
    tq = DIL_TQ
    tiles = DIL_NEAR_TILES
    seq_pad = seq + 2 * DIL_HALO
    cls_pad = cls_len + 2 * FAR_HALO
    step_cls = tiles * tq // dil
    near_bias, near_index = _tile_bias_tables(tq, NEAR_HALO, DIL_NEAR, seq)
    far_bias, far_index = _tile_bias_tables(tq, FAR_HALO, ((2 * FAR_HALO, 1),), cls_len)
    step_blk = lambda width: pl.BlockSpec((1, step_cls, dil * width), lambda bi, i: (bi, i, 0))
    acc, stats = pl.pallas_call(
        functools.partial(_dilated_near_body, tiles=tiles, bias_index=near_index),
        grid=(b, seq // (tiles * tq)),
        in_specs=[pl.BlockSpec((1, tiles * tq, D_A), lambda bi, i: (bi, i, 0)),
                  pl.BlockSpec((1, seq_pad, D_A), lambda bi, i: (bi, 0, 0)),
                  pl.BlockSpec((1, D_A, seq_pad), lambda bi, i: (bi, 0, 0)),
                  pl.BlockSpec(near_bias.shape, lambda bi, i: (0, 0, 0))],
        out_specs=[step_blk(D_A), step_blk(STAT_LANES)],
        out_shape=[jax.ShapeDtypeStruct((b, cls_len, dil * D_A), F32),
                   jax.ShapeDtypeStruct((b, cls_len, dil * STAT_LANES), F32)],
        scratch_shapes=[pltpu.VMEM((D_A // LANES, tiles * tq, LANES), F32),
                        pltpu.VMEM((tiles * tq, STAT_LANES), F32)],
        name="dilated_near",
        compiler_params=_params("parallel", "arbitrary"),
    )(qa, ka_pad, vat_pad, near_bias)
    cls_blk = lambda width: pl.BlockSpec((1, cls_len, width), lambda bi, r: (bi, 0, r))
    cls_kv = pl.BlockSpec((1, cls_pad, D_A), lambda bi, r: (bi, 0, r))
    return pl.pallas_call(
        functools.partial(_dilated_far_body, cls_len=cls_len, bias_index=far_index),
        grid=(b, dil),
        in_specs=[cls_blk(D_A), cls_kv, cls_kv, cls_blk(D_A), cls_blk(STAT_LANES),
                  pl.BlockSpec(far_bias.shape, lambda bi, r: (0, 0, 0))],
        out_specs=cls_blk(D_A),
        out_shape=jax.ShapeDtypeStruct((b, cls_len, dil * D_A), F32),
        name="dilated_far",
        compiler_params=_params("parallel", "parallel"),
    )(qa_cls, ka_cls, va_cls, acc, stats, far_bias)


def _grid_attn_body(qt_ref, k_ref, vt_ref, o_ref, s0_ref, s1_ref, acc_ref, *, tq, tk, nk):
    cols = acc_ref.shape[2]
    vparts = tk // vt_ref.shape[-1]
    vsplit = vt_ref.shape[-1] // tk
    for g in range(N_KV_B):
        qt = qt_ref[0, g]

        def scores(kb):
            k0 = pl.multiple_of(kb * tk, tk)
            return jnp.dot(k_ref[g, pl.ds(k0, tk), :], qt, preferred_element_type=F32)

        def consume(s_ref, kb, carry):
            m_prev, l_prev = carry
            s = s_ref[...]
            m_new = jnp.maximum(m_prev, jnp.max(s, axis=0, keepdims=True))
            alpha = jnp.exp2(m_prev - m_new)
            p = jnp.exp2(s - m_new)
            l_new = alpha * l_prev + jnp.sum(p, axis=0, keepdims=True)
            if vparts:
                vt = jnp.concatenate([vt_ref[kb * vparts + u, g] for u in range(vparts)], axis=1)
            else:
                off = pl.multiple_of((kb % vsplit) * tk, tk)
                vt = vt_ref[kb // vsplit, g, :, pl.ds(off, tk)]
            pv = jnp.dot(vt, p.astype(BF16), preferred_element_type=F32)
            acc_ref[g] = alpha * acc_ref[g] + pv
            return m_new, l_new

        if g == 0:
            s0_ref[...] = scores(0)
        acc_ref[g] = jnp.zeros(acc_ref.shape[1:], F32)

        def pair(kb, carry, prefetch):
            s1_ref[...] = scores(kb + 1)
            carry = consume(s0_ref, kb, carry)
            if prefetch is not None:
                s0_ref[...] = prefetch()
            return consume(s1_ref, kb + 1, carry)

        init = (jnp.full((1, cols), NEG_INF, F32), jnp.zeros((1, cols), F32))
        carry = lax.fori_loop(0, nk // 2 - 1, lambda j, c: pair(2 * j, c, lambda: scores(2 * j + 2)), init)
        if g + 1 < N_KV_B:
            k_next, qt_next = k_ref.at[g + 1], qt_ref[0, g + 1]
            prefetch = lambda: jnp.dot(k_next[pl.ds(0, tk), :], qt_next, preferred_element_type=F32)
        else:
            prefetch = None
        _, l_fin = pair(nk - 2, carry, prefetch)
        acc_ref[g] = acc_ref[g] / l_fin
    ot = acc_ref[...].reshape(N_KV_B * HEAD_DIM, cols).T
    tv = vt_ref.shape[-1]
    for part in range(tq // tv):
        heads = [ot[(part * GQA_GROUP + r) * tv:(part * GQA_GROUP + r + 1) * tv, g * HEAD_DIM:(g + 1) * HEAD_DIM]
                 for g in range(N_KV_B) for r in range(GQA_GROUP)]
        o_ref[0, part * tv:(part + 1) * tv, :] = jnp.concatenate(heads, axis=1).astype(o_ref.dtype)


def _grid_attn(qt, kb, vt, b, tq, tk):
    seq = qt.shape[0] // b * tq
    cols = GQA_GROUP * tq
    tv = vt.shape[-1]
    nq, nk = seq // tq, seq // tk
    return pl.pallas_call(
        functools.partial(_grid_attn_body, tq=tq, tk=tk, nk=nk),
        grid=(b, nq),
        in_specs=[pl.BlockSpec((1, N_KV_B, HEAD_DIM, cols), lambda bi, i: (bi * nq + i, 0, 0, 0)),
                  pl.BlockSpec((N_KV_B, seq, HEAD_DIM), lambda bi, i: (0, bi, 0)),
                  pl.BlockSpec((seq // tv, N_KV_B, HEAD_DIM, tv), lambda bi, i: (bi, 0, 0, 0))],
        out_specs=pl.BlockSpec((1, tq, D_B), lambda bi, i: (bi, i, 0)),
        out_shape=jax.ShapeDtypeStruct((b, seq, D_B), BF16),
        scratch_shapes=[pltpu.VMEM((tk, cols), F32), pltpu.VMEM((tk, cols), F32),
                        pltpu.VMEM((N_KV_B, HEAD_DIM, cols), F32)],
        name="grid_attn",
        compiler_params=_params("parallel", "arbitrary"),
    )(qt, kb, vt)


def _fft_tables(seq):
    n1, n2 = seq // FFT_N2, FFT_N2
    k1 = np.arange(n1)
    ang1 = 2.0 * np.pi * ((k1[:, None] * k1[None, :]) % n1) / n1
    f1 = np.concatenate([np.cos(ang1), -np.sin(ang1)], axis=0)
    k = k1[:, None, None] + n1 * np.arange(n2)[None, :, None]
    ang2 = 2.0 * np.pi * ((k * np.arange(n2)[None, None, :]) % seq) / seq
    g = np.concatenate([np.cos(ang2), np.sin(ang2)], axis=-1)
    c = np.arange(HEAD_DIM)
    ang3 = 2.0 * np.pi * ((c[:, None] * c[None, :]) % HEAD_DIM) / HEAD_DIM
    eye = np.eye(N_GROUPS_C)
    dc, ds = np.kron(eye, np.cos(ang3)), np.kron(eye, np.sin(ang3))
    d = np.concatenate([np.concatenate([dc, -ds], axis=1), np.concatenate([ds, dc], axis=1)], axis=0)
    return tuple(jnp.asarray(t, BF16) for t in (f1, g, d))


def _fft1_body(u_ref, f_ref, a_ref):
    a_ref[0] = jnp.dot(f_ref[...], u_ref[0].astype(BF16), preferred_element_type=F32).astype(BF16)


def _fft2_body(a_ref, g_ref, d_ref, o_ref, *, kb1):
    for j in range(kb1):
        ar, ai = a_ref[0, 0, j], a_ref[0, 1, j]
        p = (jnp.dot(ar, d_ref[:D_C, :], preferred_element_type=F32)
             + jnp.dot(ai, d_ref[D_C:, :], preferred_element_type=F32)).astype(BF16)
        gj = g_ref[j]
        o_ref[0, :, j * D_C:(j + 1) * D_C] = (
            jnp.dot(gj[:, :FFT_N2], p[:, :D_C], preferred_element_type=F32)
            + jnp.dot(gj[:, FFT_N2:], p[:, D_C:], preferred_element_type=F32)).astype(o_ref.dtype)


def _fourier(uc, tables):
    b, seq, _ = uc.shape
    f1, g, d = tables
    n1, n2 = seq // FFT_N2, FFT_N2
    cols = n2 * D_C
    tc = 4096
    a = pl.pallas_call(
        _fft1_body,
        grid=(b, cols // tc),
        in_specs=[pl.BlockSpec((1, n1, tc), lambda bi, j: (bi, 0, j)),
                  pl.BlockSpec((2 * n1, n1), lambda bi, j: (0, 0))],
        out_specs=pl.BlockSpec((1, 2 * n1, tc), lambda bi, j: (bi, 0, j)),
        out_shape=jax.ShapeDtypeStruct((b, 2 * n1, cols), BF16),
        name="fft1",
        compiler_params=_params("parallel", "parallel"),
    )(uc.reshape(b, n1, cols), f1)
    kb1 = 8
    z = pl.pallas_call(
        functools.partial(_fft2_body, kb1=kb1),
        grid=(b, n1 // kb1),
        in_specs=[pl.BlockSpec((1, 2, kb1, n2, D_C), lambda bi, j: (bi, 0, j, 0, 0)),
                  pl.BlockSpec((kb1, n2, 2 * n2), lambda bi, j: (j, 0, 0)),
                  pl.BlockSpec((2 * D_C, 2 * D_C), lambda bi, j: (0, 0))],
        out_specs=pl.BlockSpec((1, n2, kb1 * D_C), lambda bi, j: (bi, 0, j)),
        out_shape=jax.ShapeDtypeStruct((b, n2, n1 * D_C), BF16),
        name="fft2",
        compiler_params=_params("parallel", "parallel"),
    )(a.reshape(b, 2, n1, n2, D_C), g, d)
    return z.reshape(b * seq, D_C)


def _outproj_body(oac_ref, ob_ref, oc_ref, x_ref, gh_ref, w_ref, gpost_ref, gffn_ref, m_ref,
                  x1_ref, hn_ref, oa_ref):
    cls_rows = oa_ref.shape[1] // DIL_CLASSES
    for r in range(DIL_CLASSES):
        for c in range(D_A // LANES):
            lo = r * D_A + c * LANES
            oa_ref[c, pl.ds(r, cls_rows, stride=DIL_CLASSES), :] = oac_ref[0, :, lo:lo + LANES]
    oa = jnp.concatenate([oa_ref[c] for c in range(D_A // LANES)], axis=1)
    mix = None
    col = 0
    width = 2 * LANES
    for o_src in (oa, ob_ref, oc_ref):
        for c in range(o_src.shape[-1] // width):
            o = o_src[:, c * width:(c + 1) * width].astype(F32)
            y = o * lax.rsqrt(_head_mean_sq(o, m_ref) + RMS_EPS) * gh_ref[:, col:col + width]
            part = jnp.dot(y.astype(BF16), w_ref[col:col + width, :], preferred_element_type=F32)
            mix = part if mix is None else mix + part
            col += width
    x1 = x_ref[...] + _rms(mix, gpost_ref[...])
    x1_ref[...] = x1
    hn_ref[...] = _rms(x1, gffn_ref[...]).astype(BF16)


def _outproj(oa_cls, ob, oc, x2d, seq, g_heads, w_out, g_post, g_ffn, head_mask, tm):
    n = x2d.shape[0]
    bps = seq // tm
    row = lambda width: pl.BlockSpec((tm, width), lambda i: (i, 0))
    const = lambda shape: pl.BlockSpec(shape, lambda i: (0, 0))
    return pl.pallas_call(
        _outproj_body,
        grid=(n // tm,),
        in_specs=[pl.BlockSpec((1, tm // DIL_CLASSES, DIL_CLASSES * D_A), lambda i: (i // bps, i % bps, 0)),
                  row(D_B), row(D_C), row(D_MODEL), const((1, D_MIX)),
                  const((D_MIX, D_MODEL)), const((1, D_MODEL)), const((1, D_MODEL)),
                  const((2 * LANES, 2 * LANES))],
        out_specs=[row(D_MODEL), row(D_MODEL)],
        out_shape=[jax.ShapeDtypeStruct((n, D_MODEL), F32), jax.ShapeDtypeStruct((n, D_MODEL), BF16)],
        scratch_shapes=[pltpu.VMEM((D_A // LANES, tm, LANES), F32)],
        name="outproj",
        compiler_params=_params("parallel"),
    )(oa_cls, ob, oc, x2d, g_heads, w_out, g_post, g_ffn, head_mask)


def _ffn_body(h_ref, hp_ref, hx_ref, x_ref, wg_ref, wu_ref, wd_ref, cw_ref, cb_ref, gpost_ref,
              o_ref, hext_ref, acc_ref, *, tm, blocks_per_seq):
    i = pl.program_id(0)
    c = pl.program_id(1)
    pad = BF16_ROWS
    rows = tm + 2 * pad

    @pl.when(c == 0)
    def _():
        first = (i % blocks_per_seq) == 0
        last = (i % blocks_per_seq) == blocks_per_seq - 1
        hext_ref[:pad, :] = jnp.where(first, jnp.zeros_like(hp_ref[0]), hp_ref[0])
        hext_ref[pad:pad + tm, :] = h_ref[...]
        hext_ref[pad + tm:, :] = jnp.where(last, jnp.zeros_like(hx_ref[0]), hx_ref[0])
        acc_ref[...] = jnp.zeros(acc_ref.shape, F32)

    g = jnp.dot(hext_ref[...], wg_ref[...], preferred_element_type=F32)
    cw = cw_ref[...]
    g = (pltpu.roll(g, 1, axis=0)[pad:pad + tm] * cw[0:1, :] + g[pad:pad + tm] * cw[1:2, :]
         + pltpu.roll(g, rows - 1, axis=0)[pad:pad + tm] * cw[2:3, :] + cb_ref[...])
    up = jnp.dot(h_ref[...], wu_ref[...], preferred_element_type=F32)
    inner = g * (GELU_C0 + GELU_C1 * (g * g))
    act = (g * (1.0 + jnp.tanh(inner)) * up).astype(BF16)
    acc_ref[...] += jnp.dot(act, wd_ref[...], preferred_element_type=F32)

    @pl.when(c == pl.num_programs(1) - 1)
    def _():
        o_ref[...] = x_ref[...] + _rms(acc_ref[...], gpost_ref[...])


def _ffn(hn, x1, seq, w_gate, w_up, w_down, conv_w, conv_b, g_post, tm, tf):
    n = x1.shape[0]
    nf = D_FF // tf
    blocks_per_seq = seq // tm
    halo_blocks = tm // BF16_ROWS
    n_halo = n // BF16_ROWS
    hn3 = hn.reshape(n_halo, BF16_ROWS, D_MODEL)
    row = lambda: pl.BlockSpec((tm, D_MODEL), lambda i, c: (i, 0))
    prev = pl.BlockSpec((1, BF16_ROWS, D_MODEL), lambda i, c: (jnp.maximum(i * halo_blocks - 1, 0), 0, 0))
    nxt = pl.BlockSpec((1, BF16_ROWS, D_MODEL),
                       lambda i, c: (jnp.minimum((i + 1) * halo_blocks, n_halo - 1), 0, 0))
    cols = lambda rows: pl.BlockSpec((rows, tf), lambda i, c: (0, c))
    return pl.pallas_call(
        functools.partial(_ffn_body, tm=tm, blocks_per_seq=blocks_per_seq),
        grid=(n // tm, nf),
        in_specs=[row(), prev, nxt, row(), cols(D_MODEL), cols(D_MODEL),
                  pl.BlockSpec((tf, D_MODEL), lambda i, c: (c, 0)), cols(3), cols(1),
                  pl.BlockSpec((1, D_MODEL), lambda i, c: (0, 0))],
        out_specs=row(),
        out_shape=jax.ShapeDtypeStruct((n, D_MODEL), F32),
        scratch_shapes=[pltpu.VMEM((tm + 2 * BF16_ROWS, D_MODEL), BF16), pltpu.VMEM((tm, D_MODEL), F32)],
        name="ffn",
        compiler_params=_params("parallel", "arbitrary"),
    )(hn, hn3, hn3, x1, w_gate, w_up, w_down, conv_w, conv_b, g_post)


def _rope_tables(pos, rot_dim, theta):
    half = rot_dim // 2
    inv = jnp.power(jnp.float32(theta), -jnp.arange(half, dtype=jnp.float32) / half)
    ang = pos.astype(jnp.float32)[:, None] * inv[None, :]
    return jnp.cos(ang), jnp.sin(ang)


def _position_tables(seq):
    rows = seq // GRID_W
    t = jnp.arange(seq, dtype=jnp.int32)
    row = jnp.broadcast_to(jnp.arange(rows, dtype=jnp.int32)[:, None], (rows, GRID_W)).reshape(-1)
    col = jnp.broadcast_to(jnp.arange(GRID_W, dtype=jnp.int32)[None, :], (rows, GRID_W)).reshape(-1)
    cos_t, sin_t = _rope_tables(t, ROT_DIM_PARTIAL, ROPE_THETA_PARTIAL)
    cos_r, sin_r = _rope_tables(row, HEAD_DIM // 2, ROPE_THETA_AXIAL)
    cos_c, sin_c = _rope_tables(col, HEAD_DIM // 2, ROPE_THETA_AXIAL)
    rest = HEAD_DIM - ROT_DIM_PARTIAL
    ca = jnp.concatenate([cos_t, cos_t, jnp.ones((seq, rest), F32)], axis=-1)
    sa = jnp.concatenate([-sin_t, sin_t, jnp.zeros((seq, rest), F32)], axis=-1)
    cb = jnp.concatenate([cos_r, cos_r, cos_c, cos_c], axis=-1)
    sb = jnp.concatenate([-sin_r, sin_r, -sin_c, sin_c], axis=-1)
    reps = LANES // HEAD_DIM
    return tuple(jnp.tile(t_, (1, reps)) for t_ in (ca, sa, cb, sb))


def _head_mask():
    lane = np.arange(2 * LANES)
    same = (lane[:, None] // HEAD_DIM) == (lane[None, :] // HEAD_DIM)
    return jnp.asarray(same / HEAD_DIM, BF16)


def _trunk(x, w):
    b, seq, _ = x.shape
    n = b * seq
    tm, tq, tk = TOKEN_TILE, min(GRID_TQ, seq), min(GRID_TK, seq // 2)
    tabs = _position_tables(seq)
    fft_tabs = _fft_tables(seq)
    head_mask = _head_mask()
    x2d = x.reshape(n, D_MODEL)
    depth = w["w_in"].shape[0]
    for l in range(depth):
        tile4 = lambda g_: jnp.tile(g_[l][None, :], (1, 2 * LANES // HEAD_DIM))
        qa, qa_cls, ka_pad, ka_cls, va_cls, vat_pad, qt, kb, vt, uc = _inproj(
            x2d, b, seq, w["g_mix_pre"][l][None], w["w_in"][l], tile4(w["g_q"]), tile4(w["g_k"]),
            tabs, head_mask, tm, tq)
        oa_cls = _dilated(qa.reshape(b, seq, D_A), qa_cls, ka_pad, ka_cls, va_cls, vat_pad)
        ob = _grid_attn(qt, kb, vt, b, tq, tk).reshape(n, D_B)
        oc = _fourier(uc.reshape(b, seq, D_C), fft_tabs)
        x1, hn = _outproj(oa_cls, ob, oc, x2d, seq, w["g_heads"][l][None], w["w_out"][l], w["g_mix_post"][l][None],
                          w["g_ffn_pre"][l][None], head_mask, tm)
        x2d = _ffn(hn, x1, seq, w["w_gate"][l], w["w_up"][l], w["w_down"][l], w["conv_w"][l],
                   w["conv_b"][l][None], w["g_ffn_post"][l][None], FFN_TILE, FFN_CHUNK)
    return x2d.reshape(b, seq, D_MODEL)


def _prep_weights(g_mix_pre, g_mix_post, w_in, g_q, g_k, g_heads, w_out, g_ffn_pre, g_ffn_post,
                  w_gate, w_up, conv_w, conv_b, w_down):
    return dict(
        g_mix_pre=g_mix_pre, g_mix_post=g_mix_post, g_q=g_q, g_k=g_k, g_heads=g_heads,
        g_ffn_pre=g_ffn_pre, g_ffn_post=g_ffn_post,
        w_in=w_in.astype(BF16), w_out=w_out.astype(BF16),
        w_gate=w_gate.astype(BF16), w_up=(0.5 * w_up).astype(BF16), w_down=w_down.astype(BF16),
        conv_w=conv_w, conv_b=conv_b)


def kernel(x_prompt, x_sample, g_mix_pre, g_mix_post, w_in, g_q, g_k, g_heads, w_out, g_ffn_pre, g_ffn_post,
           w_gate, w_up, conv_w, conv_b, w_down):
    w = _prep_weights(g_mix_pre, g_mix_post, w_in, g_q, g_k, g_heads, w_out, g_ffn_pre, g_ffn_post,
                      w_gate, w_up, conv_w, conv_b, w_down)
    return (_trunk(x_prompt, w), _trunk(x_sample, w))
```

```python
import functools

import numpy as np
import jax
import jax.numpy as jnp
from jax import lax
from jax.experimental import pallas as pl
from jax.experimental.pallas import tpu as pltpu

F32 = jnp.float32
BF16 = jnp.bfloat16

D_MODEL = 1024
HEAD_DIM = 64
N_HEADS_A = 4
DILATED_PATTERNS = ((128, 1), (512, 4), (2048, 16))
N_HEADS_B = 8
N_KV_B = 2
GQA_GROUP = N_HEADS_B // N_KV_B
N_GROUPS_C = 4
D_A = N_HEADS_A * HEAD_DIM
D_B = N_HEADS_B * HEAD_DIM
D_KV_B = N_KV_B * HEAD_DIM
D_C = N_GROUPS_C * HEAD_DIM
D_MIX = D_A + D_B + D_C
D_IN = 3 * D_A + D_B + 2 * D_KV_B + D_C
D_FF = 4 * D_MODEL
ROPE_THETA_PARTIAL = 500000.0
ROT_DIM_PARTIAL = HEAD_DIM // 4
ROPE_THETA_AXIAL = 10000.0
GRID_W = 64
RMS_EPS = 1e-6
NEG_INF = -1e30
Q_SCALE = HEAD_DIM ** -0.5 * float(np.log2(np.e))

LANES = 128
BF16_ROWS = 16
DIL_HALO = max(w // 2 for w, _ in DILATED_PATTERNS)
DIL_TQ = 128
DIL_FAR = DILATED_PATTERNS[-1]
DIL_CLASSES = DIL_FAR[1]
DIL_NEAR_TILES = 8
FFT_N2 = 128
VMEM_LIMIT = 56 * 1024 * 1024
TOKEN_TILE = 1024
GRID_TQ = 1024
GRID_TK = 512
FFN_TILE = 1024
FFN_CHUNK = 1024
GELU_C0 = float(np.sqrt(2.0 / np.pi))
GELU_C1 = 0.044715 * GELU_C0


def _params(*sem):
    return pltpu.CompilerParams(dimension_semantics=sem, vmem_limit_bytes=VMEM_LIMIT)


def _rms(x, g):
    return x * lax.rsqrt(jnp.mean(x * x, axis=-1, keepdims=True) + RMS_EPS) * g


def _head_mean_sq(x, m_ref):
    return jnp.dot((x * x).astype(BF16), m_ref[...], preferred_element_type=F32)


def _rotate_half(x, half):
    n = x.shape[-1]
    lane = lax.broadcasted_iota(jnp.int32, (1, n), 1)
    fwd = pltpu.roll(x, n - half, axis=1)
    bwd = pltpu.roll(x, half, axis=1)
    return jnp.where((lane % (2 * half)) < half, fwd, bwd)


def _inproj_body(x_ref, g_ref, w_ref, gq_ref, gk_ref, ca_ref, sa_ref, cb_ref, sb_ref, m_ref,
                 ka_zero, kac_zero, vac_zero, vat_zero,
                 qa_ref, qac_ref, ka_ref, kac_ref, vac_ref, vat_ref, qt_ref, kb_ref, vt_ref, uc_ref,
                 rel_ref):
    del ka_zero, kac_zero, vac_zero, vat_zero
    tm = x_ref.shape[0]
    h = _rms(x_ref[...], g_ref[...]).astype(BF16)

    def proj(lo, width):
        return jnp.dot(h, w_ref[:, lo:lo + width], preferred_element_type=F32)

    ca, sa, cb, sb = ca_ref[...], sa_ref[...], cb_ref[...], sb_ref[...]

    def rope_a(p):
        return p * ca + _rotate_half(p, ROT_DIM_PARTIAL // 2) * sa

    def norm_b(p, gain):
        return p * lax.rsqrt(_head_mean_sq(p, m_ref) + RMS_EPS) * gain

    def rope_b(y):
        return y * cb + _rotate_half(y, HEAD_DIM // 4) * sb

    def to_classes(val, out_ref):
        for c in range(D_A // LANES):
            rel_ref[c] = val[:, c * LANES:(c + 1) * LANES]
        for r in range(DIL_CLASSES):
            for c in range(D_A // LANES):
                lo = r * D_A + c * LANES
                out_ref[0, :, lo:lo + LANES] = rel_ref[c, pl.ds(r, tm // DIL_CLASSES, stride=DIL_CLASSES), :].astype(BF16)

    pq = proj(0, D_A)
    qa = jnp.concatenate([rope_a(pq[:, c * LANES:(c + 1) * LANES]) for c in range(D_A // LANES)], axis=1) * Q_SCALE
    qa_ref[...] = qa.astype(BF16)
    to_classes(qa, qac_ref)
    pk = proj(D_A, D_A)
    ka = jnp.concatenate([rope_a(pk[:, c * LANES:(c + 1) * LANES]) for c in range(D_A // LANES)], axis=1)
    ka_ref[0] = ka.astype(BF16)
    to_classes(ka, kac_ref)
    pv = proj(2 * D_A, D_A)
    to_classes(pv, vac_ref)
    vat_ref[0] = pv.T.astype(BF16)

    gq, gk = gq_ref[...], gk_ref[...]
    for g in range(N_KV_B):
        y = norm_b(proj(3 * D_A + g * 2 * LANES, 2 * LANES), gq)
        for c in range(2):
            yt = (rope_b(y[:, c * LANES:(c + 1) * LANES]) * Q_SCALE).T.astype(BF16)
            for hh in range(LANES // HEAD_DIM):
                r = c * (LANES // HEAD_DIM) + hh
                qt_ref[0, g, :, r * tm:(r + 1) * tm] = yt[hh * HEAD_DIM:(hh + 1) * HEAD_DIM, :]
    pkv = proj(3 * D_A + D_B, 2 * D_KV_B)
    kb = rope_b(norm_b(pkv, gk)[:, :D_KV_B]).astype(BF16)
    for g in range(N_KV_B):
        kb_ref[g] = kb[:, g * HEAD_DIM:(g + 1) * HEAD_DIM]
    vbt = pkv[:, D_KV_B:].T.astype(BF16)
    for g in range(N_KV_B):
        vt_ref[0, g] = vbt[g * HEAD_DIM:(g + 1) * HEAD_DIM, :]
    uc_ref[...] = proj(3 * D_A + D_B + 2 * D_KV_B, D_C)


def _inproj(x2d, batch, seq, g_pre, w_in, gq, gk, tabs, head_mask, tm, tq):
    n = x2d.shape[0]
    bps = seq // tm
    halo_blocks = DIL_HALO // tm
    seq_pad = seq + 2 * DIL_HALO
    parts = tq // tm
    row = lambda width: pl.BlockSpec((tm, width), lambda i: (i, 0))
    const = lambda shape: pl.BlockSpec(shape, lambda i: (0, 0))
    tab = pl.BlockSpec((tm, LANES), lambda i: (i % bps, 0))
    pad_rows = pl.BlockSpec((1, tm, D_A), lambda i: (i // bps, i % bps + halo_blocks, 0))
    pad_cols = pl.BlockSpec((1, D_A, tm), lambda i: (i // bps, 0, i % bps + halo_blocks))
    cls_rows, cls_width = tm // DIL_CLASSES, DIL_CLASSES * D_A
    cls = pl.BlockSpec((1, cls_rows, cls_width), lambda i: (i // bps, i % bps, 0))
    cls_pad = pl.BlockSpec((1, cls_rows, cls_width), lambda i: (i // bps, i % bps + halo_blocks, 0))
    out_specs = [row(D_A), cls, pad_rows, cls_pad, cls_pad, pad_cols,
                 pl.BlockSpec((1, N_KV_B, HEAD_DIM, GQA_GROUP * tm), lambda i: (i // parts, 0, 0, i % parts)),
                 pl.BlockSpec((N_KV_B, tm, HEAD_DIM), lambda i: (0, i, 0)),
                 pl.BlockSpec((1, N_KV_B, HEAD_DIM, tm), lambda i: (i, 0, 0, 0)),
                 row(D_C)]
    out_shape = [jax.ShapeDtypeStruct((n, D_A), BF16),
                 jax.ShapeDtypeStruct((batch, seq // DIL_CLASSES, cls_width), BF16),
                 jax.ShapeDtypeStruct((batch, seq_pad, D_A), BF16),
                 jax.ShapeDtypeStruct((batch, seq_pad // DIL_CLASSES, cls_width), BF16),
                 jax.ShapeDtypeStruct((batch, seq_pad // DIL_CLASSES, cls_width), BF16),
                 jax.ShapeDtypeStruct((batch, D_A, seq_pad), BF16),
                 jax.ShapeDtypeStruct((n // tq, N_KV_B, HEAD_DIM, GQA_GROUP * tq), BF16),
                 jax.ShapeDtypeStruct((N_KV_B, n, HEAD_DIM), BF16),
                 jax.ShapeDtypeStruct((n // tm, N_KV_B, HEAD_DIM, tm), BF16),
                 jax.ShapeDtypeStruct((n, D_C), F32)]
    padded = (2, 3, 4, 5)
    zero_bufs = [jnp.zeros(out_shape[k].shape, BF16) for k in padded]
    n_in = 10
    return pl.pallas_call(
        _inproj_body,
        grid=(n // tm,),
        in_specs=[row(D_MODEL), const((1, D_MODEL)), const((D_MODEL, D_IN)),
                  const((1, 2 * LANES)), const((1, 2 * LANES)), tab, tab, tab, tab,
                  const((2 * LANES, 2 * LANES))] + [pl.BlockSpec(memory_space=pl.ANY)] * len(padded),
        out_specs=out_specs,
        out_shape=out_shape,
        input_output_aliases={n_in + k: o for k, o in enumerate(padded)},
        scratch_shapes=[pltpu.VMEM((D_A // LANES, tm, LANES), F32)],
        name="inproj",
        compiler_params=_params("parallel"),
    )(x2d, g_pre, w_in, gq, gk, *tabs, head_mask, *zero_bufs)


DIL_NEAR = DILATED_PATTERNS[:-1]
NEAR_HALO = max(w // 2 for w, _ in DIL_NEAR)
FAR_HALO = DIL_FAR[0] // (2 * DIL_FAR[1])
STAT_LANES = LANES


def _band_bias(tq, halo, patterns):
    delta = np.arange(tq + 2 * halo)[None, :] - halo - np.arange(tq)[:, None]
    mult = np.zeros(delta.shape, np.int64)
    for window, dil in patterns:
        mult += (np.abs(delta) <= (window // (2 * dil)) * dil) & (delta % dil == 0)
    return np.where(mult > 0, np.log2(np.maximum(mult, 1)), NEG_INF).astype(np.float32)


def _tile_bias_tables(tq, halo, patterns, length):
    base = _band_bias(tq, halo, patterns).T
    n_tiles, edge = length // tq, -(-halo // tq)
    compress = n_tiles > 2 * edge + 1
    tiles = list(range(edge + 1)) + list(range(n_tiles - edge, n_tiles)) if compress else list(range(n_tiles))
    tabs = []
    for t in tiles:
        kpos = t * tq - halo + np.arange(tq + 2 * halo)
        inside = (kpos >= 0) & (kpos < length)
        tabs.append(np.where(inside[:, None], base, NEG_INF))
    tables = jnp.asarray(np.tile(np.stack(tabs), (1, 1, N_HEADS_A)), F32)

    def index(t):
        if not compress:
            return t
        if isinstance(t, int):
            return t if t < edge else (t - (n_tiles - 2 * edge - 1) if t >= n_tiles - edge else edge)
        return jnp.where(t < edge, t, jnp.where(t >= n_tiles - edge, t - (n_tiles - 2 * edge - 1), edge))

    return tables, index


def _band_attention_t(q, kw, vtw, bias_t):
    tq = q.shape[0]
    qt = q.astype(F32).T.astype(BF16)
    rows = lax.broadcasted_iota(jnp.int32, (D_A, N_HEADS_A * tq), 0) // HEAD_DIM
    cols = lax.broadcasted_iota(jnp.int32, (D_A, N_HEADS_A * tq), 1) // tq
    qbd = jnp.where(rows == cols, jnp.concatenate([qt] * N_HEADS_A, axis=1), jnp.zeros((), BF16))
    st = jnp.dot(kw, qbd, preferred_element_type=F32)
    st = st + bias_t
    m = jnp.max(st, axis=0, keepdims=True)
    p = jnp.exp2(st - m)
    l = jnp.sum(p, axis=0, keepdims=True)
    pb = p.astype(BF16)
    blocks = []
    per = 2 * LANES // tq
    for c in range(N_HEADS_A // per):
        rs = slice(c * per * HEAD_DIM, (c + 1) * per * HEAD_DIM)
        o = jnp.dot(vtw[rs, :], pb[:, c * per * tq:(c + 1) * per * tq], preferred_element_type=F32)
        blocks += [o[h * HEAD_DIM:(h + 1) * HEAD_DIM, h * tq:(h + 1) * tq] for h in range(per)]
    return jnp.concatenate(blocks, axis=0), m, l


def _dilated_near_body(q_ref, k_ref, vt_ref, bias_ref, acc_ref, stat_ref, acc_nat, stat_nat, *, tiles, bias_index):
    tq = DIL_TQ
    win = tq + 2 * NEAR_HALO
    for t in range(tiles):
        tile = pl.program_id(1) * tiles + t
        start = pl.multiple_of(tile * tq + (DIL_HALO - NEAR_HALO), tq)
        ot, m, l = _band_attention_t(q_ref[0, t * tq:(t + 1) * tq, :], k_ref[0, pl.ds(start, win), :],
                                     vt_ref[0, :, pl.ds(start, win)], bias_ref[bias_index(tile)])
        o = ot.T
        for c in range(D_A // LANES):
            acc_nat[c, t * tq:(t + 1) * tq, :] = o[:, c * LANES:(c + 1) * LANES]
        stat_rows = ([m[:, h * tq:(h + 1) * tq] for h in range(N_HEADS_A)]
                     + [l[:, h * tq:(h + 1) * tq] for h in range(N_HEADS_A)]
                     + [jnp.zeros((STAT_LANES - 2 * N_HEADS_A, tq), F32)])
        stat_nat[t * tq:(t + 1) * tq, :] = jnp.concatenate(stat_rows, axis=0).T
    cls_rows = tiles * tq // DIL_CLASSES
    for r in range(DIL_CLASSES):
        for c in range(D_A // LANES):
            lo = r * D_A + c * LANES
            acc_ref[0, :, lo:lo + LANES] = acc_nat[c, pl.ds(r, cls_rows, stride=DIL_CLASSES), :]
        stat_ref[0, :, r * STAT_LANES:(r + 1) * STAT_LANES] = stat_nat[pl.ds(r, cls_rows, stride=DIL_CLASSES), :]


def _dilated_far_body(q_ref, k_ref, v_ref, acc_ref, stat_ref, bias_ref, o_ref, *, cls_len, bias_index):
    tq = DIL_TQ
    win = tq + 2 * FAR_HALO
    vt = v_ref[0].astype(F32).T.astype(BF16)
    for t in range(cls_len // tq):
        rs = slice(t * tq, (t + 1) * tq)
        ot2, m2, l2 = _band_attention_t(q_ref[0, rs, :], k_ref[0, t * tq:t * tq + win, :],
                                        vt[:, t * tq:t * tq + win], bias_ref[bias_index(t)])
        ot1 = acc_ref[0, rs, :].T
        st1 = stat_ref[0, rs, :].T
        outs = []
        for h in range(N_HEADS_A):
            hs = slice(h * HEAD_DIM, (h + 1) * HEAD_DIM)
            cs = slice(h * tq, (h + 1) * tq)
            m1, l1 = st1[h:h + 1, :], st1[N_HEADS_A + h:N_HEADS_A + h + 1, :]
            m = jnp.maximum(m1, m2[:, cs])
            w1, w2 = jnp.exp2(m1 - m), jnp.exp2(m2[:, cs] - m)
            outs.append((w1 * ot1[hs, :] + w2 * ot2[hs, :]) / (w1 * l1 + w2 * l2[:, cs]))
        o_ref[0, rs, :] = jnp.concatenate(outs, axis=0).T


def _dilated(qa, qa_cls, ka_pad, ka_cls, va_cls, vat_pad):
    b, seq, _ = qa.shape
    dil = DIL_CLASSES
    cls_len = seq // dil
    tq = DIL_TQ
    tiles = DIL_NEAR_TILES
    seq_pad = seq + 2 * DIL_HALO
    cls_pad = cls_len + 2 * FAR_HALO
    step_cls = tiles * tq // dil
    near_bias, near_index = _tile_bias_tables(tq, NEAR_HALO, DIL_NEAR, seq)
    far_bias, far_index = _tile_bias_tables(tq, FAR_HALO, ((2 * FAR_HALO, 1),), cls_len)
    step_blk = lambda width: pl.BlockSpec((1, step_cls, dil * width), lambda bi, i: (bi, i, 0))
    acc, stats = pl.pallas_call(
        functools.partial(_dilated_near_body, tiles=tiles, bias_index=near_index),
        grid=(b, seq // (tiles * tq)),
        in_specs=[pl.BlockSpec((1, tiles * tq, D_A), lambda bi, i: (bi, i, 0)),
                  pl.BlockSpec((1, seq_pad, D_A), lambda bi, i: (bi, 0, 0)),
                  pl.BlockSpec((1, D_A, seq_pad), lambda bi, i: (bi, 0, 0)),
                  pl.BlockSpec(near_bias.shape, lambda bi, i: (0, 0, 0))],
        out_specs=[step_blk(D_A), step_blk(STAT_LANES)],
        out_shape=[jax.ShapeDtypeStruct((b, cls_len, dil * D_A), F32),
                   jax.ShapeDtypeStruct((b, cls_len, dil * STAT_LANES), F32)],
        scratch_shapes=[pltpu.VMEM((D_A // LANES, tiles * tq, LANES), F32),
                        pltpu.VMEM((tiles * tq, STAT_LANES), F32)],
        name="dilated_near",
        compiler_params=_params("parallel", "arbitrary"),
    )(qa, ka_pad, vat_pad, near_bias)
    cls_blk = lambda width: pl.BlockSpec((1, cls_len, width), lambda bi, r: (bi, 0, r))
    cls_kv = pl.BlockSpec((1, cls_pad, D_A), lambda bi, r: (bi, 0, r))
    return pl.pallas_call(
        functools.partial(_dilated_far_body, cls_len=cls_len, bias_index=far_index),
        grid=(b, dil),
        in_specs=[cls_blk(D_A), cls_kv, cls_kv, cls_blk(D_A), cls_blk(STAT_LANES),
                  pl.BlockSpec(far_bias.shape, lambda bi, r: (0, 0, 0))],
        out_specs=cls_blk(D_A),
        out_shape=jax.ShapeDtypeStruct((b, cls_len, dil * D_A), F32),
        name="dilated_far",
        compiler_params=_params("parallel", "parallel"),
    )(qa_cls, ka_cls, va_cls, acc, stats, far_bias)


def _grid_attn_body(qt_ref, k_ref, vt_ref, o_ref, s0_ref, s1_ref, acc_ref, *, tq, tk, nk):
    cols = acc_ref.shape[2]
    vparts = tk // vt_ref.shape[-1]
    vsplit = vt_ref.shape[-1] // tk
    for g in range(N_KV_B):
        qt = qt_ref[0, g]

        def scores(kb):
            k0 = pl.multiple_of(kb * tk, tk)
            return jnp.dot(k_ref[g, pl.ds(k0, tk), :], qt, preferred_element_type=F32)

        def consume(s_ref, kb, carry):
            m_prev, l_prev = carry
            s = s_ref[...]
            m_new = jnp.maximum(m_prev, jnp.max(s, axis=0, keepdims=True))
            alpha = jnp.exp2(m_prev - m_new)
            p = jnp.exp2(s - m_new)
            l_new = alpha * l_prev + jnp.sum(p, axis=0, keepdims=True)
            if vparts:
                vt = jnp.concatenate([vt_ref[kb * vparts + u, g] for u in range(vparts)], axis=1)
            else:
                off = pl.multiple_of((kb % vsplit) * tk, tk)
                vt = vt_ref[kb // vsplit, g, :, pl.ds(off, tk)]
            pv = jnp.dot(vt, p.astype(BF16), preferred_element_type=F32)
            acc_ref[g] = alpha * acc_ref[g] + pv
            return m_new, l_new

        if g == 0:
            s0_ref[...] = scores(0)
        acc_ref[g] = jnp.zeros(acc_ref.shape[1:], F32)

        def pair(kb, carry, prefetch):
            s1_ref[...] = scores(kb + 1)
            carry = consume(s0_ref, kb, carry)
            if prefetch is not None:
                s0_ref[...] = prefetch()
            return consume(s1_ref, kb + 1, carry)

        init = (jnp.full((1, cols), NEG_INF, F32), jnp.zeros((1, cols), F32))
        carry = lax.fori_loop(0, nk // 2 - 1, lambda j, c: pair(2 * j, c, lambda: scores(2 * j + 2)), init)
        if g + 1 < N_KV_B:
            k_next, qt_next = k_ref.at[g + 1], qt_ref[0, g + 1]
            prefetch = lambda: jnp.dot(k_next[pl.ds(0, tk), :], qt_next, preferred_element_type=F32)
        else:
            prefetch = None
        _, l_fin = pair(nk - 2, carry, prefetch)
        acc_ref[g] = acc_ref[g] / l_fin
    ot = acc_ref[...].reshape(N_KV_B * HEAD_DIM, cols).T
    tv = vt_ref.shape[-1]
    for part in range(tq // tv):
        heads = [ot[(part * GQA_GROUP + r) * tv:(part * GQA_GROUP + r + 1) * tv, g * HEAD_DIM:(g + 1) * HEAD_DIM]
                 for g in range(N_KV_B) for r in range(GQA_GROUP)]
        o_ref[0, part * tv:(part + 1) * tv, :] = jnp.concatenate(heads, axis=1).astype(o_ref.dtype)


def _grid_attn(qt, kb, vt, b, tq, tk):
    seq = qt.shape[0] // b * tq
    cols = GQA_GROUP * tq
    tv = vt.shape[-1]
    nq, nk = seq // tq, seq // tk
    return pl.pallas_call(
        functools.partial(_grid_attn_body, tq=tq, tk=tk, nk=nk),
        grid=(b, nq),
        in_specs=[pl.BlockSpec((1, N_KV_B, HEAD_DIM, cols), lambda bi, i: (bi * nq + i, 0, 0, 0)),
                  pl.BlockSpec((N_KV_B, seq, HEAD_DIM), lambda bi, i: (0, bi, 0)),
                  pl.BlockSpec((seq // tv, N_KV_B, HEAD_DIM, tv), lambda bi, i: (bi, 0, 0, 0))],
        out_specs=pl.BlockSpec((1, tq, D_B), lambda bi, i: (bi, i, 0)),
        out_shape=jax.ShapeDtypeStruct((b, seq, D_B), BF16),
        scratch_shapes=[pltpu.VMEM((tk, cols), F32), pltpu.VMEM((tk, cols), F32),
                        pltpu.VMEM((N_KV_B, HEAD_DIM, cols), F32)],
        name="grid_attn",
        compiler_params=_params("parallel", "arbitrary"),
    )(qt, kb, vt)


def _fft_tables(seq):
    n1, n2 = seq // FFT_N2, FFT_N2
    k1 = np.arange(n1)
    ang1 = 2.0 * np.pi * ((k1[:, None] * k1[None, :]) % n1) / n1
    f1 = np.concatenate([np.cos(ang1), -np.sin(ang1)], axis=0)
    k = k1[:, None, None] + n1 * np.arange(n2)[None, :, None]
    ang2 = 2.0 * np.pi * ((k * np.arange(n2)[None, None, :]) % seq) / seq
    g = np.concatenate([np.cos(ang2), np.sin(ang2)], axis=-1)
    c = np.arange(HEAD_DIM)
    ang3 = 2.0 * np.pi * ((c[:, None] * c[None, :]) % HEAD_DIM) / HEAD_DIM
    eye = np.eye(N_GROUPS_C)
    dc, ds = np.kron(eye, np.cos(ang3)), np.kron(eye, np.sin(ang3))
    d = np.concatenate([np.concatenate([dc, -ds], axis=1), np.concatenate([ds, dc], axis=1)], axis=0)
    return tuple(jnp.asarray(t, BF16) for t in (f1, g, d))


def _fft1_body(u_ref, f_ref, a_ref):
    a_ref[0] = jnp.dot(f_ref[...], u_ref[0].astype(BF16), preferred_element_type=F32).astype(BF16)


def _fft2_body(a_ref, g_ref, d_ref, o_ref, *, kb1):
    for j in range(kb1):
        ar, ai = a_ref[0, 0, j], a_ref[0, 1, j]
        p = (jnp.dot(ar, d_ref[:D_C, :], preferred_element_type=F32)
             + jnp.dot(ai, d_ref[D_C:, :], preferred_element_type=F32)).astype(BF16)
        gj = g_ref[j]
        o_ref[0, :, j * D_C:(j + 1) * D_C] = (
            jnp.dot(gj[:, :FFT_N2], p[:, :D_C], preferred_element_type=F32)
            + jnp.dot(gj[:, FFT_N2:], p[:, D_C:], preferred_element_type=F32)).astype(o_ref.dtype)


def _fourier(uc, tables):
    b, seq, _ = uc.shape
    f1, g, d = tables
    n1, n2 = seq // FFT_N2, FFT_N2
    cols = n2 * D_C
    tc = 4096
    a = pl.pallas_call(
        _fft1_body,
        grid=(b, cols // tc),
        in_specs=[pl.BlockSpec((1, n1, tc), lambda bi, j: (bi, 0, j)),
                  pl.BlockSpec((2 * n1, n1), lambda bi, j: (0, 0))],
        out_specs=pl.BlockSpec((1, 2 * n1, tc), lambda bi, j: (bi, 0, j)),
        out_shape=jax.ShapeDtypeStruct((b, 2 * n1, cols), BF16),
        name="fft1",
        compiler_params=_params("parallel", "parallel"),
    )(uc.reshape(b, n1, cols), f1)
    kb1 = 8
    z = pl.pallas_call(
        functools.partial(_fft2_body, kb1=kb1),
        grid=(b, n1 // kb1),
        in_specs=[pl.BlockSpec((1, 2, kb1, n2, D_C), lambda bi, j: (bi, 0, j, 0, 0)),
                  pl.BlockSpec((kb1, n2, 2 * n2), lambda bi, j: (j, 0, 0)),
                  pl.BlockSpec((2 * D_C, 2 * D_C), lambda bi, j: (0, 0))],
        out_specs=pl.BlockSpec((1, n2, kb1 * D_C), lambda bi, j: (bi, 0, j)),
        out_shape=jax.ShapeDtypeStruct((b, n2, n1 * D_C), BF16),
        name="fft2",
        compiler_params=_params("parallel", "parallel"),
    )(a.reshape(b, 2, n1, n2, D_C), g, d)
    return z.reshape(b * seq, D_C)


def _outproj_body(oac_ref, ob_ref, oc_ref, x_ref, gh_ref, w_ref, gpost_ref, gffn_ref, m_ref,
                  x1_ref, hn_ref, oa_ref):
    cls_rows = oa_ref.shape[1] // DIL_CLASSES
    for r in range(DIL_CLASSES):
        for c in range(D_A // LANES):
            lo = r * D_A + c * LANES
            oa_ref[c, pl.ds(r, cls_rows, stride=DIL_CLASSES), :] = oac_ref[0, :, lo:lo + LANES]
    oa = jnp.concatenate([oa_ref[c] for c in range(D_A // LANES)], axis=1)
    ys = []
    col = 0
    width = 2 * LANES
    for o_src in (oa, ob_ref, oc_ref):
        for c in range(o_src.shape[-1] // width):
            o = o_src[:, c * width:(c + 1) * width].astype(F32)
            y = o * lax.rsqrt(_head_mean_sq(o, m_ref) + RMS_EPS) * gh_ref[:, col:col + width]
            ys.append(y.astype(BF16))
            col += width
    mix = jnp.dot(jnp.concatenate(ys, axis=1), w_ref[...], preferred_element_type=F32)
    x1 = x_ref[...] + _rms(mix, gpost_ref[...])
    x1_ref[...] = x1
    hn_ref[...] = _rms(x1, gffn_ref[...]).astype(BF16)


def _outproj(oa_cls, ob, oc, x2d, seq, g_heads, w_out, g_post, g_ffn, head_mask, tm):
    n = x2d.shape[0]
    bps = seq // tm
    row = lambda width: pl.BlockSpec((tm, width), lambda i: (i, 0))
    const = lambda shape: pl.BlockSpec(shape, lambda i: (0, 0))
    return pl.pallas_call(
        _outproj_body,
        grid=(n // tm,),
        in_specs=[pl.BlockSpec((1, tm // DIL_CLASSES, DIL_CLASSES * D_A), lambda i: (i // bps, i % bps, 0)),
                  row(D_B), row(D_C), row(D_MODEL), const((1, D_MIX)),
                  const((D_MIX, D_MODEL)), const((1, D_MODEL)), const((1, D_MODEL)),
                  const((2 * LANES, 2 * LANES))],
        out_specs=[row(D_MODEL), row(D_MODEL)],
        out_shape=[jax.ShapeDtypeStruct((n, D_MODEL), F32), jax.ShapeDtypeStruct((n, D_MODEL), BF16)],
        scratch_shapes=[pltpu.VMEM((D_A // LANES, tm, LANES), F32)],
        name="outproj",
        compiler_params=_params("parallel"),
    )(oa_cls, ob, oc, x2d, g_heads, w_out, g_post, g_ffn, head_mask)


def _ffn_body(h_ref, hp_ref, hx_ref, x_ref, wg_ref, wu_ref, wd_ref, cw_ref, cb_ref, gpost_ref,
              o_ref, hext_ref, acc_ref, *, tm, blocks_per_seq):
    i = pl.program_id(0)
    c = pl.program_id(1)
    pad = BF16_ROWS
    rows = tm + 2 * pad

    @pl.when(c == 0)
    def _():
        first = (i % blocks_per_seq) == 0
        last = (i % blocks_per_seq) == blocks_per_seq - 1
        hext_ref[:pad, :] = jnp.where(first, jnp.zeros_like(hp_ref[0]), hp_ref[0])
        hext_ref[pad:pad + tm, :] = h_ref[...]
        hext_ref[pad + tm:, :] = jnp.where(last, jnp.zeros_like(hx_ref[0]), hx_ref[0])
        acc_ref[...] = jnp.zeros(acc_ref.shape, F32)

    g = jnp.dot(hext_ref[...], wg_ref[...], preferred_element_type=F32)
    cw = cw_ref[...]
    g = (pltpu.roll(g, 1, axis=0)[pad:pad + tm] * cw[0:1, :] + g[pad:pad + tm] * cw[1:2, :]
         + pltpu.roll(g, rows - 1, axis=0)[pad:pad + tm] * cw[2:3, :] + cb_ref[...])
    up = jnp.dot(h_ref[...], wu_ref[...], preferred_element_type=F32)
    inner = g * (GELU_C0 + GELU_C1 * (g * g))
    act = (g * (1.0 + jnp.tanh(inner)) * up).astype(BF16)
    acc_ref[...] += jnp.dot(act, wd_ref[...], preferred_element_type=F32)

    @pl.when(c == pl.num_programs(1) - 1)
    def _():
        o_ref[...] = x_ref[...] + _rms(acc_ref[...], gpost_ref[...])


def _ffn(hn, x1, seq, w_gate, w_up, w_down, conv_w, conv_b, g_post, tm, tf):
    n = x1.shape[0]
    nf = D_FF // tf
    blocks_per_seq = seq // tm
    halo_blocks = tm // BF16_ROWS
    n_halo = n // BF16_ROWS
    hn3 = hn.reshape(n_halo, BF16_ROWS, D_MODEL)
    row = lambda: pl.BlockSpec((tm, D_MODEL), lambda i, c: (i, 0))
    prev = pl.BlockSpec((1, BF16_ROWS, D_MODEL), lambda i, c: (jnp.maximum(i * halo_blocks - 1, 0), 0, 0))
    nxt = pl.BlockSpec((1, BF16_ROWS, D_MODEL),
                       lambda i, c: (jnp.minimum((i + 1) * halo_blocks, n_halo - 1), 0, 0))
    cols = lambda rows: pl.BlockSpec((rows, tf), lambda i, c: (0, c))
    return pl.pallas_call(
        functools.partial(_ffn_body, tm=tm, blocks_per_seq=blocks_per_seq),
        grid=(n // tm, nf),
        in_specs=[row(), prev, nxt, row(), cols(D_MODEL), cols(D_MODEL),
                  pl.BlockSpec((tf, D_MODEL), lambda i, c: (c, 0)), cols(3), cols(1),
                  pl.BlockSpec((1, D_MODEL), lambda i, c: (0, 0))],
        out_specs=row(),
        out_shape=jax.ShapeDtypeStruct((n, D_MODEL), F32),
        scratch_shapes=[pltpu.VMEM((tm + 2 * BF16_ROWS, D_MODEL), BF16), pltpu.VMEM((tm, D_MODEL), F32)],
        name="ffn",
        compiler_params=_params("parallel", "arbitrary"),
    )(hn, hn3, hn3, x1, w_gate, w_up, w_down, conv_w, conv_b, g_post)


def _rope_tables(pos, rot_dim, theta):
    half = rot_dim // 2
    inv = jnp.power(jnp.float32(theta), -jnp.arange(half, dtype=jnp.float32) / half)
    ang = pos.astype(jnp.float32)[:, None] * inv[None, :]
    return jnp.cos(ang), jnp.sin(ang)


def _position_tables(seq):
    rows = seq // GRID_W
    t = jnp.arange(seq, dtype=jnp.int32)
    row = jnp.broadcast_to(jnp.arange(rows, dtype=jnp.int32)[:, None], (rows, GRID_W)).reshape(-1)
    col = jnp.broadcast_to(jnp.arange(GRID_W, dtype=jnp.int32)[None, :], (rows, GRID_W)).reshape(-1)
    cos_t, sin_t = _rope_tables(t, ROT_DIM_PARTIAL, ROPE_THETA_PARTIAL)
    cos_r, sin_r = _rope_tables(row, HEAD_DIM // 2, ROPE_THETA_AXIAL)
    cos_c, sin_c = _rope_tables(col, HEAD_DIM // 2, ROPE_THETA_AXIAL)
    rest = HEAD_DIM - ROT_DIM_PARTIAL
    ca = jnp.concatenate([cos_t, cos_t, jnp.ones((seq, rest), F32)], axis=-1)
    sa = jnp.concatenate([-sin_t, sin_t, jnp.zeros((seq, rest), F32)], axis=-1)
    cb = jnp.concatenate([cos_r, cos_r, cos_c, cos_c], axis=-1)
    sb = jnp.concatenate([-sin_r, sin_r, -sin_c, sin_c], axis=-1)
    reps = LANES // HEAD_DIM
    return tuple(jnp.tile(t_, (1, reps)) for t_ in (ca, sa, cb, sb))


def _head_mask():
    lane = np.arange(2 * LANES)
    same = (lane[:, None] // HEAD_DIM) == (lane[None, :] // HEAD_DIM)
    return jnp.asarray(same / HEAD_DIM, BF16)


def _trunk(x, w):
    b, seq, _ = x.shape
    n = b * seq
    tm, tq, tk = TOKEN_TILE, min(GRID_TQ, seq), min(GRID_TK, seq // 2)
    tabs = _position_tables(seq)
    fft_tabs = _fft_tables(seq)
    head_mask = _head_mask()
    x2d = x.reshape(n, D_MODEL)
    depth = w["w_in"].shape[0]
    for l in range(depth):
        tile4 = lambda g_: jnp.tile(g_[l][None, :], (1, 2 * LANES // HEAD_DIM))
        qa, qa_cls, ka_pad, ka_cls, va_cls, vat_pad, qt, kb, vt, uc = _inproj(
            x2d, b, seq, w["g_mix_pre"][l][None], w["w_in"][l], tile4(w["g_q"]), tile4(w["g_k"]),
            tabs, head_mask, tm, tq)
        oa_cls = _dilated(qa.reshape(b, seq, D_A), qa_cls, ka_pad, ka_cls, va_cls, vat_pad)
        ob = _grid_attn(qt, kb, vt, b, tq, tk).reshape(n, D_B)
        oc = _fourier(uc.reshape(b, seq, D_C), fft_tabs)
        x1, hn = _outproj(oa_cls, ob, oc, x2d, seq, w["g_heads"][l][None], w["w_out"][l], w["g_mix_post"][l][None],
                          w["g_ffn_pre"][l][None], head_mask, tm)
        x2d = _ffn(hn, x1, seq, w["w_gate"][l], w["w_up"][l], w["w_down"][l], w["conv_w"][l],
                   w["conv_b"][l][None], w["g_ffn_post"][l][None], FFN_TILE, FFN_CHUNK)
    return x2d.reshape(b, seq, D_MODEL)


def _prep_weights(g_mix_pre, g_mix_post, w_in, g_q, g_k, g_heads, w_out, g_ffn_pre, g_ffn_post,
                  w_gate, w_up, conv_w, conv_b, w_down):
    return dict(
        g_mix_pre=g_mix_pre, g_mix_post=g_mix_post, g_q=g_q, g_k=g_k, g_heads=g_heads,
        g_ffn_pre=g_ffn_pre, g_ffn_post=g_ffn_post,
        w_in=w_in.astype(BF16), w_out=w_out.astype(BF16),
        w_gate=w_gate.astype(BF16), w_up=(0.5 * w_up).astype(BF16), w_down=w_down.astype(BF16),
        conv_w=conv_w, conv_b=conv_b)


def kernel(x_prompt, x_sample, g_mix_pre, g_mix_post, w_in, g_q, g_k, g_heads, w_out, g_ffn_pre, g_ffn_post,
           w_gate, w_up, conv_w, conv_b, w_down):
    w = _prep_weights(g_mix_pre, g_mix_post, w_in, g_q, g_k, g_heads, w_out, g_ffn_pre, g_ffn_post,
                      w_gate, w_up, conv_w, conv_b, w_down)
    return (_trunk(x_prompt, w), _trunk(x_sample, w))
```
